```python
import math
import jax, jax.numpy as jnp
from jax import lax
import numpy as np

D_MODEL = 4096
BATCH = 2
SEQ = 8192
DEPTH = 1

GRID_W = 64
CTX_LEN = 256
D_MIX = D_MODEL
HY_WIDTH = D_MIX // 2
HY_GROUP = 128
HY_ORDER = 2
HY_POS_BANDS = 16
HY_POS_DIM = 1 + 2 * HY_POS_BANDS
HY_FILT_HIDDEN = 64
HY_DECAY_TARGET = 1e-2
HY_FAST_DECAY_PCT = 0.3
HY_SLOW_DECAY_PCT = 1.5
MLA_NOPE = 128
MLA_ROPE = 64
MLA_QK = MLA_NOPE + MLA_ROPE
MLA_V = 128
MLA_HEADS = (D_MIX - HY_WIDTH) // MLA_V
Q_LORA = 1024
KV_LORA = 512
ROPE_THETA = 10000.0
D_FF = 4 * D_MODEL
Q_BLOCK = 128
NORM_EPS = 1e-6
N_MOD = 6
HY_COLS = 3 * HY_WIDTH
IN_COLS = HY_COLS + Q_LORA + KV_LORA + MLA_ROPE

kernel_name = 'hyena_mla_parallel_heads_dit_block'


def rms_norm(x, g):
    xf = x.astype(jnp.float32)
    y = xf * lax.rsqrt(jnp.mean(xf * xf, axis=-1, keepdims=True) + NORM_EPS)
    return (y * g.astype(jnp.float32)).astype(x.dtype)


def modulate(x, g, shift, scale):
    return rms_norm(x, g) * (1.0 + scale) + shift


def adaln_chunks(cvec, w, b, n):
    m = jax.nn.silu(cvec) @ w[:, :n * D_MODEL] + b[:n * D_MODEL]
    return jnp.split(m, n, axis=-1)


def short_conv_centred(u, w, b):
    L = u.shape[1]
    up = jnp.pad(u, ((0, 0), (1, 1), (0, 0)))
    return up[:, :L] * w[0] + up[:, 1:L + 1] * w[1] + up[:, 2:] * w[2] + b


def hyena_filters(L, w1, b1, w2, b2, w3, freq):
    f32 = jnp.float32
    pos = jnp.arange(L, dtype=f32)
    t = jnp.linspace(0.0, 1.0, L, dtype=f32)[:, None]
    bands = jnp.linspace(1e-4, HY_POS_BANDS - 1, HY_POS_BANDS, dtype=f32)
    ang = (2.0 * math.pi / L) * pos[:, None] * bands[None, :]
    feats = jnp.concatenate([t, jnp.cos(ang), -jnp.sin(ang)], axis=-1)
    fr = freq.astype(f32)
    h = jnp.sin(fr * (feats @ w1.astype(f32) + b1.astype(f32)))
    h = jnp.sin(fr * (h @ w2.astype(f32) + b2.astype(f32)))
    filt = (h @ w3.astype(f32)).reshape(L, 2, HY_ORDER, HY_WIDTH)
    deltas = jnp.abs(jnp.linspace(math.log(HY_DECAY_TARGET) / HY_SLOW_DECAY_PCT,
                                  math.log(HY_DECAY_TARGET) / HY_FAST_DECAY_PCT, HY_WIDTH, dtype=f32))
    decay = jnp.exp(-t * deltas[None, :])
    return filt * decay[:, None, None, :]


def bidir_long_conv(u, h_fwd, h_bwd, bias):
    L = u.shape[1]
    k = jnp.concatenate([h_fwd, jnp.zeros_like(h_fwd[:1]), h_bwd[:0:-1]], axis=0)
    k_f = jnp.fft.rfft(k, n=2 * L, axis=0)
    u_f = jnp.fft.rfft(u.astype(jnp.float32), n=2 * L, axis=1)
    y = jnp.fft.irfft(u_f * k_f[None], n=2 * L, axis=1)[:, :L]
    return (y + u.astype(jnp.float32) * bias.astype(jnp.float32)).astype(u.dtype)


def hyena_mixer(hy_in, conv_w, conv_b, filt, hy_bias):
    u = short_conv_centred(hy_in, conv_w, conv_b)
    v, x1, x2 = jnp.split(u, 3, axis=-1)
    z = x1 * bidir_long_conv(v, filt[:, 0, 0], filt[:, 1, 0], hy_bias[0])
    return x2 * bidir_long_conv(z, filt[:, 0, 1], filt[:, 1, 1], hy_bias[1])


def axial_rope_tables(L):
    rows = L // GRID_W
    row = jnp.repeat(jnp.arange(rows, dtype=jnp.float32), GRID_W)
    col = jnp.tile(jnp.arange(GRID_W, dtype=jnp.float32), rows)
    half = MLA_ROPE // 2
    inv = ROPE_THETA ** (-jnp.arange(0, half, 2, dtype=jnp.float32) / half)
    ang = jnp.concatenate([row[:, None] * inv, col[:, None] * inv], axis=-1)
    return jnp.cos(ang), jnp.sin(ang)


def apply_axial_rope(x, cos, sin):
    B, L, H, _ = x.shape
    nf = MLA_ROPE // 4
    xr = x.reshape(B, L, H, 2, 2, nf)
    x1, x2 = xr[..., 0, :], xr[..., 1, :]
    c = cos.reshape(L, 2, nf)[None, :, None]
    s = sin.reshape(L, 2, nf)[None, :, None]
    out = jnp.stack([x1 * c - x2 * s, x1 * s + x2 * c], axis=-2)
    return out.reshape(B, L, H, MLA_ROPE).astype(x.dtype)


def mla_queries(q_a, g_qa, w_qb, q_norm_g, rope):
    B, L, _ = q_a.shape
    q = (rms_norm(q_a, g_qa) @ w_qb).reshape(B, L, MLA_HEADS, MLA_QK)
    q = rms_norm(q, q_norm_g)
    if rope is not None:
        q = jnp.concatenate([q[..., :MLA_NOPE], apply_axial_rope(q[..., MLA_NOPE:], *rope)], axis=-1)
    return q


def mla_keys_values(kv_a, k_rope, g_kva, w_kvb, k_norm_g, rope):
    B, L, _ = kv_a.shape
    kv = (rms_norm(kv_a, g_kva) @ w_kvb).reshape(B, L, MLA_HEADS, MLA_NOPE + MLA_V)
    k_nope, v = kv[..., :MLA_NOPE], kv[..., MLA_NOPE:]
    k_r = jnp.broadcast_to(k_rope[:, :, None, :], (B, L, MLA_HEADS, MLA_ROPE)).astype(k_nope.dtype)
    k = rms_norm(jnp.concatenate([k_nope, k_r], axis=-1), k_norm_g)
    if rope is not None:
        k = jnp.concatenate([k[..., :MLA_NOPE], apply_axial_rope(k[..., MLA_NOPE:], *rope)], axis=-1)
    return k, v


def attend_latent(q, k_lat, v_lat, k_ctx, v_ctx):
    B, L, H, _ = q.shape
    k = jnp.concatenate([k_ctx, k_lat], axis=1)
    v = jnp.concatenate([v_ctx, v_lat], axis=1)
    nblk = L // Q_BLOCK
    qb = q.reshape(B, nblk, Q_BLOCK, H, MLA_QK).transpose(1, 0, 2, 3, 4)
    scale = MLA_QK ** -0.5

    def one_block(qi):
        s = jnp.einsum('bqhd,bkhd->bhqk', qi, k, preferred_element_type=jnp.float32) * scale
        p = jax.nn.softmax(s, axis=-1)
        return jnp.einsum('bhqk,bkhd->bqhd', p.astype(v.dtype), v)

    o = lax.map(one_block, qb)
    return o.transpose(1, 0, 2, 3, 4).reshape(B, L, H * MLA_V)


def attend_ctx(q, k, v):
    B, Lc, H, _ = q.shape
    s = jnp.einsum('bqhd,bkhd->bhqk', q, k, preferred_element_type=jnp.float32) * (MLA_QK ** -0.5)
    p = jax.nn.softmax(s, axis=-1)
    return jnp.einsum('bhqk,bkhd->bqhd', p.astype(v.dtype), v).reshape(B, Lc, H * MLA_V)


def sq_relu_mlp(h, w1, w2):
    return jnp.square(jax.nn.relu(h @ w1)) @ w2


def setup_inputs(seed: int = 0) -> dict:
    key = jax.random.key(seed)
    ks = jax.random.split(key, 32)
    f32 = jnp.float32

    def nrm(k, shape, scale):
        return jax.random.normal(k, shape, dtype=f32) * scale

    def gain(k, shape):
        return 1.0 + 0.01 * jax.random.normal(k, shape, dtype=f32)

    return {
        'x': nrm(ks[0], (BATCH, SEQ, D_MODEL), 1.0),
        'c': nrm(ks[1], (BATCH, D_MODEL), 1.0),
        'ctx': nrm(ks[2], (BATCH, CTX_LEN, D_MODEL), 1.0),
        'c_ctx': nrm(ks[3], (D_MODEL,), 1.0),
        'norm1_g': gain(ks[4], (DEPTH, D_MODEL)),
        'norm2_g': gain(ks[5], (DEPTH, D_MODEL)),
        'w_ada': nrm(ks[6], (DEPTH, D_MODEL, N_MOD * D_MODEL), D_MODEL ** -0.5),
        'b_ada': nrm(ks[7], (DEPTH, N_MOD * D_MODEL), 0.01),
        'w_in': nrm(ks[8], (DEPTH, D_MODEL, IN_COLS), D_MODEL ** -0.5),
        'hy_conv_w': nrm(ks[9], (DEPTH, 3, HY_COLS), 3 ** -0.5),
        'hy_conv_b': nrm(ks[10], (DEPTH, HY_COLS), 0.01),
        'hy_filt_w1': nrm(ks[11], (DEPTH, HY_POS_DIM, HY_FILT_HIDDEN), HY_POS_DIM ** -0.5),
        'hy_filt_b1': nrm(ks[12], (DEPTH, HY_FILT_HIDDEN), 0.1),
        'hy_filt_w2': nrm(ks[13], (DEPTH, HY_FILT_HIDDEN, HY_FILT_HIDDEN), HY_FILT_HIDDEN ** -0.5),
        'hy_filt_b2': nrm(ks[14], (DEPTH, HY_FILT_HIDDEN), 0.1),
        'hy_filt_w3': nrm(ks[15], (DEPTH, HY_FILT_HIDDEN, 2 * HY_ORDER * HY_WIDTH), 0.004),
        'hy_freq': gain(ks[16], (DEPTH, HY_FILT_HIDDEN)),
        'hy_bias': nrm(ks[17], (DEPTH, HY_ORDER, HY_WIDTH), 0.1),
        'mla_g_qa': gain(ks[18], (DEPTH, Q_LORA)),
        'mla_w_qb': nrm(ks[19], (DEPTH, Q_LORA, MLA_HEADS * MLA_QK), Q_LORA ** -0.5),
        'mla_g_kva': gain(ks[20], (DEPTH, KV_LORA)),
        'mla_w_kvb': nrm(ks[21], (DEPTH, KV_LORA, MLA_HEADS * (MLA_NOPE + MLA_V)), KV_LORA ** -0.5),
        'mla_q_norm_g': gain(ks[22], (DEPTH, MLA_QK)),
        'mla_k_norm_g': gain(ks[23], (DEPTH, MLA_QK)),
        'w_out': nrm(ks[24], (DEPTH, D_MIX, D_MODEL), D_MIX ** -0.5),
        'w_mlp1': nrm(ks[25], (DEPTH, D_MODEL, D_FF), D_MODEL ** -0.5),
        'w_mlp2': nrm(ks[26], (DEPTH, D_FF, D_MODEL), D_FF ** -0.5),
    }


def reference(x, c, ctx, c_ctx, norm1_g, norm2_g, w_ada, b_ada, w_in, hy_conv_w, hy_conv_b,
              hy_filt_w1, hy_filt_b1, hy_filt_w2, hy_filt_b2, hy_filt_w3, hy_freq, hy_bias,
              mla_g_qa, mla_w_qb, mla_g_kva, mla_w_kvb, mla_q_norm_g, mla_k_norm_g,
              w_out, w_mlp1, w_mlp2):
    L = x.shape[1]
    Lc = ctx.shape[1]
    rope = axial_rope_tables(L)
    q_end = HY_COLS + Q_LORA
    kv_end = q_end + KV_LORA

    for l in range(DEPTH):
        last = l == DEPTH - 1
        sh1, sc1, g1, sh2, sc2, g2 = [m[:, None, :] for m in adaln_chunks(c, w_ada[l], b_ada[l], N_MOD)]
        n_ctx_mod = 2 if last else N_MOD
        ctx_mod = adaln_chunks(c_ctx, w_ada[l], b_ada[l], n_ctx_mod)

        hc = modulate(ctx, norm1_g[l], ctx_mod[0], ctx_mod[1])
        if last:
            proj_c = hc @ w_in[l][:, HY_COLS:]
            kv_a_c = proj_c[..., Q_LORA:Q_LORA + KV_LORA]
            k_rope_c = proj_c[..., Q_LORA + KV_LORA:]
        else:
            proj_c = hc @ w_in[l]
            kv_a_c = proj_c[..., q_end:kv_end]
            k_rope_c = proj_c[..., kv_end:]
        k_c, v_c = mla_keys_values(kv_a_c, k_rope_c, mla_g_kva[l], mla_w_kvb[l], mla_k_norm_g[l], None)

        h = modulate(x, norm1_g[l], sh1, sc1)
        proj = h @ w_in[l]
        filt = hyena_filters(L, hy_filt_w1[l], hy_filt_b1[l], hy_filt_w2[l], hy_filt_b2[l],
                             hy_filt_w3[l], hy_freq[l])
        y_hy = hyena_mixer(proj[..., :HY_COLS], hy_conv_w[l], hy_conv_b[l], filt, hy_bias[l])
        q = mla_queries(proj[..., HY_COLS:q_end], mla_g_qa[l], mla_w_qb[l], mla_q_norm_g[l], rope)
        k, v = mla_keys_values(proj[..., q_end:kv_end], proj[..., kv_end:], mla_g_kva[l], mla_w_kvb[l],
                               mla_k_norm_g[l], rope)
        y_att = attend_latent(q, k, v, k_c, v_c)
        x_new = x + g1 * (jnp.concatenate([y_hy, y_att], axis=-1) @ w_out[l])
        x_new = x_new + g2 * sq_relu_mlp(modulate(x_new, norm2_g[l], sh2, sc2), w_mlp1[l], w_mlp2[l])

        if not last:
            csh2, csc2, cg1, cg2 = ctx_mod[3], ctx_mod[4], ctx_mod[2], ctx_mod[5]
            filt_c = hyena_filters(Lc, hy_filt_w1[l], hy_filt_b1[l], hy_filt_w2[l], hy_filt_b2[l],
                                   hy_filt_w3[l], hy_freq[l])
            yc_hy = hyena_mixer(proj_c[..., :HY_COLS], hy_conv_w[l], hy_conv_b[l], filt_c, hy_bias[l])
            q_c = mla_queries(proj_c[..., HY_COLS:q_end], mla_g_qa[l], mla_w_qb[l], mla_q_norm_g[l], None)
            yc_att = attend_ctx(q_c, k_c, v_c)
            ctx_new = ctx + cg1 * (jnp.concatenate([yc_hy, yc_att], axis=-1) @ w_out[l])
            ctx = ctx_new + cg2 * sq_relu_mlp(modulate(ctx_new, norm2_g[l], csh2, csc2), w_mlp1[l], w_mlp2[l])
        x = x_new
    return x
```

```python
import functools
import math

import numpy as np
import jax
import jax.numpy as jnp
from jax import lax
from jax.experimental import pallas as pl
from jax.experimental.pallas import tpu as pltpu

F32 = jnp.float32
BF16 = jnp.bfloat16

NORM_EPS = 1e-6
GRID_W = 64
MLA_ROPE = 64
ROPE_THETA = 10000.0
HY_POS_BANDS = 16
HY_DECAY_TARGET = 1e-2
HY_FAST_DECAY_PCT = 0.3
HY_SLOW_DECAY_PCT = 1.5

LANES = 128
SUBLANES = 8
VMEM_LIMIT = 56 * 1024 * 1024


def _cparams(*sem):
    return pltpu.CompilerParams(dimension_semantics=sem, vmem_limit_bytes=VMEM_LIMIT)


def _pick(dim, pref, align):
    t = min(pref, dim)
    t -= t % align
    while t >= align:
        if dim % t == 0:
            return t
        t -= align
    return dim


def _dot(a, b):
    return jnp.dot(a, b, preferred_element_type=F32)


def _adaln_kernel(c_ref, w_ref, b_ref, o_ref):
    c = c_ref[...]
    s = c * jax.nn.sigmoid(c)
    o_ref[...] = _dot(s.astype(BF16), w_ref[...].astype(BF16)) + b_ref[...]


def _adaln(cc, w, b):
    d, n = w.shape
    tn = _pick(n, 512, LANES)
    return pl.pallas_call(
        _adaln_kernel,
        grid=(n // tn,),
        in_specs=[pl.BlockSpec((8, d), lambda j: (0, 0)),
                  pl.BlockSpec((d, tn), lambda j: (0, j)),
                  pl.BlockSpec((1, tn), lambda j: (0, j))],
        out_specs=pl.BlockSpec((8, tn), lambda j: (0, j)),
        out_shape=jax.ShapeDtypeStruct((8, n), F32),
        compiler_params=_cparams("parallel"),
        name="adaln",
    )(cc, w, b)


def _nmm_kernel(x_ref, g_ref, sh_ref, sc_ref, w_ref, o_ref, h_ref, *, sq_relu):
    @pl.when(pl.program_id(1) == 0)
    def _():
        x = x_ref[...]
        ms = jnp.mean(x * x, axis=-1, keepdims=True)
        y = x * lax.rsqrt(ms + NORM_EPS) * g_ref[...]
        h_ref[...] = (y * (1.0 + sc_ref[0]) + sh_ref[0]).astype(BF16)

    acc = _dot(h_ref[...], w_ref[...])
    if sq_relu:
        acc = jnp.square(jnp.maximum(acc, 0.0))
    o_ref[...] = acc.astype(o_ref.dtype)


def _norm_mod_matmul(x, g, shift, scale, w, rows_per_mod, out_dtype, sq_relu, tm_pref, tn_pref):
    m, k = x.shape
    n = w.shape[1]
    tm = _pick(math.gcd(m, rows_per_mod), tm_pref, 16)
    tn = _pick(n, tn_pref, LANES)
    mod_idx = lambda i, j: ((i * tm) // rows_per_mod, 0, 0)
    return pl.pallas_call(
        functools.partial(_nmm_kernel, sq_relu=sq_relu),
        grid=(m // tm, n // tn),
        in_specs=[pl.BlockSpec((tm, k), lambda i, j: (i, 0)),
                  pl.BlockSpec((1, k), lambda i, j: (0, 0)),
                  pl.BlockSpec((1, 1, k), mod_idx),
                  pl.BlockSpec((1, 1, k), mod_idx),
                  pl.BlockSpec((k, tn), lambda i, j: (0, j))],
        out_specs=pl.BlockSpec((tm, tn), lambda i, j: (i, j)),
        out_shape=jax.ShapeDtypeStruct((m, n), out_dtype),
        scratch_shapes=[pltpu.VMEM((tm, k), BF16)],
        compiler_params=_cparams("parallel", "arbitrary"),
        name="norm_mod_matmul",
    )(x, g, shift, scale, w)


def _outproj_kernel(a0_ref, a1_ref, w_ref, res_ref, gate_ref, o_ref, *, k0):
    acc = _dot(a0_ref[...], w_ref[0:k0, :]) + _dot(a1_ref[...], w_ref[k0:, :])
    o_ref[...] = res_ref[...] + gate_ref[0] * acc


def _out_proj(a0, a1, w, res, gate, rows_per_mod):
    m, k0 = a0.shape
    k1 = a1.shape[1]
    n = w.shape[1]
    tm = _pick(math.gcd(m, rows_per_mod), 1024, 16)
    tn = _pick(n, 1024, LANES)
    return pl.pallas_call(
        functools.partial(_outproj_kernel, k0=k0),
        grid=(m // tm, n // tn),
        in_specs=[pl.BlockSpec((tm, k0), lambda i, j: (i, 0)),
                  pl.BlockSpec((tm, k1), lambda i, j: (i, 0)),
                  pl.BlockSpec((k0 + k1, tn), lambda i, j: (0, j)),
                  pl.BlockSpec((tm, tn), lambda i, j: (i, j)),
                  pl.BlockSpec((1, 1, tn), lambda i, j: ((i * tm) // rows_per_mod, 0, j))],
        out_specs=pl.BlockSpec((tm, tn), lambda i, j: (i, j)),
        out_shape=jax.ShapeDtypeStruct((m, n), F32),
        compiler_params=_cparams("parallel", "parallel"),
        name="out_proj",
    )(a0, a1, w, res, gate)


def _mmres_kernel(a_ref, w_ref, res_ref, gate_ref, o_ref, acc_ref):
    kk = pl.program_id(2)

    @pl.when(kk == 0)
    def _():
        acc_ref[...] = jnp.zeros_like(acc_ref)

    acc_ref[...] += _dot(a_ref[...], w_ref[...])

    @pl.when(kk == pl.num_programs(2) - 1)
    def _():
        o_ref[...] = res_ref[...] + gate_ref[0] * acc_ref[...]


def _matmul_gated_residual(a, w, res, gate, rows_per_mod):
    m, k = a.shape
    n = w.shape[1]
    tm = _pick(math.gcd(m, rows_per_mod), 1024, 16)
    tn = _pick(n, 1024, LANES)
    tk = _pick(k, 2048, LANES)
    return pl.pallas_call(
        _mmres_kernel,
        grid=(m // tm, n // tn, k // tk),
        in_specs=[pl.BlockSpec((tm, tk), lambda i, j, kk: (i, kk)),
                  pl.BlockSpec((tk, tn), lambda i, j, kk: (kk, j)),
                  pl.BlockSpec((tm, tn), lambda i, j, kk: (i, j)),
                  pl.BlockSpec((1, 1, tn), lambda i, j, kk: ((i * tm) // rows_per_mod, 0, j))],
        out_specs=pl.BlockSpec((tm, tn), lambda i, j, kk: (i, j)),
        out_shape=jax.ShapeDtypeStruct((m, n), F32),
        scratch_shapes=[pltpu.VMEM((tm, tn), F32)],
        compiler_params=_cparams("parallel", "parallel", "arbitrary"),
        name="matmul_gated_residual",
    )(a, w, res, gate)


def _sconv_kernel(p_ref, pp_ref, pn_ref, w_ref, b_ref, o_ref):
    i = pl.program_id(1)
    x = p_ref[0]
    tl = x.shape[0]
    rows = lax.broadcasted_iota(jnp.int32, x.shape, 0)
    prev_row = jnp.where(i > 0, pp_ref[0, SUBLANES - 1:SUBLANES, :], 0.0)
    next_row = jnp.where(i < pl.num_programs(1) - 1, pn_ref[0, 0:1, :], 0.0)
    up = jnp.where(rows == 0, prev_row, pltpu.roll(x, 1, 0))
    dn = jnp.where(rows == tl - 1, next_row, pltpu.roll(x, tl - 1, 0))
    o_ref[0] = up * w_ref[0:1, :] + x * w_ref[1:2, :] + dn * w_ref[2:3, :] + b_ref[...]


def _short_conv(proj, w, b):
    bsz, seq, _ = proj.shape
    c = w.shape[1]
    tl = _pick(seq, 512, SUBLANES)
    tc = _pick(c, 512, LANES)
    r = tl // SUBLANES
    last = seq // SUBLANES - 1
    return pl.pallas_call(
        _sconv_kernel,
        grid=(bsz, seq // tl, c // tc),
        in_specs=[pl.BlockSpec((1, tl, tc), lambda bb, i, j: (bb, i, j)),
                  pl.BlockSpec((1, SUBLANES, tc), lambda bb, i, j: (bb, jnp.maximum(i * r - 1, 0), j)),
                  pl.BlockSpec((1, SUBLANES, tc), lambda bb, i, j: (bb, jnp.minimum((i + 1) * r, last), j)),
                  pl.BlockSpec((3, tc), lambda bb, i, j: (0, j)),
                  pl.BlockSpec((1, tc), lambda bb, i, j: (0, j))],
        out_specs=pl.BlockSpec((1, tl, tc), lambda bb, i, j: (bb, i, j)),
        out_shape=jax.ShapeDtypeStruct((bsz, seq, c), F32),
        compiler_params=_cparams("parallel", "parallel", "parallel"),
        name="short_conv",
    )(proj, proj, proj, w, b)


def _filt_kernel(feat_ref, t_ref, w1_ref, b1_ref, w2_ref, b2_ref, fr_ref, w3_ref, dl_ref, o_ref, h_ref,
                 *, half_cols):
    i = pl.program_id(0)
    j = pl.program_id(1)

    @pl.when(j == 0)
    def _():
        fr = fr_ref[...]
        h = jnp.sin(fr * (_dot(feat_ref[...].astype(BF16), w1_ref[...]) + b1_ref[...]))
        h = jnp.sin(fr * (_dot(h.astype(BF16), w2_ref[...]) + b2_ref[...]))
        h_ref[...] = h.astype(BF16)

    f = _dot(h_ref[...], w3_ref[...]) * jnp.exp(-t_ref[...] * dl_ref[...])
    tl, tn = f.shape
    rows = lax.broadcasted_iota(jnp.int32, f.shape, 0) + i * tl
    cols = lax.broadcasted_iota(jnp.int32, f.shape, 1) + j * tn
    o_ref[0] = jnp.where((rows == 0) & (cols >= half_cols), 0.0, f)


def _hyena_filters(seq, w1, b1, w2, b2, w3, freq, width):
    pos = jnp.arange(seq, dtype=F32)
    t = jnp.linspace(0.0, 1.0, seq, dtype=F32)[:, None]
    bands = jnp.linspace(1e-4, HY_POS_BANDS - 1, HY_POS_BANDS, dtype=F32)
    ang = (2.0 * math.pi / seq) * pos[:, None] * bands[None, :]
    feats = jnp.concatenate([t, jnp.cos(ang), -jnp.sin(ang)], axis=-1)
    pd = feats.shape[1]
    hid = w1.shape[1]
    hp = LANES
    feats = jnp.pad(feats, ((0, 0), (0, hp - pd)))
    w1p = jnp.pad(w1, ((0, hp - pd), (0, hp - hid))).astype(BF16)
    w2p = jnp.pad(w2, ((0, hp - hid), (0, hp - hid))).astype(BF16)
    w3p = jnp.pad(w3, ((0, hp - hid), (0, 0))).astype(BF16)
    b1p = jnp.pad(b1, (0, hp - hid))[None, :]
    b2p = jnp.pad(b2, (0, hp - hid))[None, :]
    frp = jnp.pad(freq, (0, hp - hid))[None, :]
    deltas = jnp.abs(jnp.linspace(math.log(HY_DECAY_TARGET) / HY_SLOW_DECAY_PCT,
                                  math.log(HY_DECAY_TARGET) / HY_FAST_DECAY_PCT, width, dtype=F32))
    ncol = w3.shape[1]
    dl = jnp.tile(deltas, ncol // width)[None, :]
    tl = _pick(seq, 512, SUBLANES)
    tn = _pick(ncol, 2048, LANES)
    return pl.pallas_call(
        functools.partial(_filt_kernel, half_cols=ncol // 2),
        grid=(seq // tl, ncol // tn),
        in_specs=[pl.BlockSpec((tl, hp), lambda i, j: (i, 0)),
                  pl.BlockSpec((tl, 1), lambda i, j: (i, 0)),
                  pl.BlockSpec((hp, hp), lambda i, j: (0, 0)),
                  pl.BlockSpec((1, hp), lambda i, j: (0, 0)),
                  pl.BlockSpec((hp, hp), lambda i, j: (0, 0)),
                  pl.BlockSpec((1, hp), lambda i, j: (0, 0)),
                  pl.BlockSpec((1, hp), lambda i, j: (0, 0)),
                  pl.BlockSpec((hp, tn), lambda i, j: (0, j)),
                  pl.BlockSpec((1, tn), lambda i, j: (0, j))],
        out_specs=pl.BlockSpec((1, tl, tn), lambda i, j: (0, i, j)),
        out_shape=jax.ShapeDtypeStruct((1, seq, ncol), F32),
        scratch_shapes=[pltpu.VMEM((tl, hp), BF16)],
        compiler_params=_cparams("parallel", "arbitrary"),
        name="hyena_filters",
    )(feats, t, w1p, b1p, w2p, b2p, frp, w3p, dl)


K1_GROUP = SUBLANES


def _dft_tables(n1, n2):
    n = n1 * n2
    hl = n1 // 2
    k1 = np.arange(n1)[None, :, None]
    l1 = np.arange(hl)[None, None, :]
    l2 = np.arange(n2)[:, None, None]
    ang = -2.0 * np.pi * ((k1 * (n2 * l1 + l2)) % n) / n
    gr, gi = np.cos(ang), np.sin(ang)
    gm = np.concatenate([np.concatenate([gr, -gi], axis=2), np.concatenate([gi, gr], axis=2)], axis=1)
    a2 = -2.0 * np.pi * ((np.arange(n2)[:, None] * np.arange(n2)[None, :]) % n2) / n2
    fr, fi = np.cos(a2), np.sin(a2)
    f2 = np.block([[fr, -fi], [fi, fr]])
    if2 = np.block([[fr, fi], [-fi, fr]])
    ir, ii = np.transpose(gr, (0, 2, 1)) / n, -np.transpose(gi, (0, 2, 1)) / n
    igm = np.concatenate([np.concatenate([ir, -ii], axis=2), np.concatenate([ii, ir], axis=2)], axis=1)
    to = lambda a: jnp.asarray(a.astype(np.float32)).astype(BF16)
    return to(gm), to(f2), to(if2), to(igm)


def _fft_split(seq):
    n = 2 * seq
    n2 = LANES if n % (LANES * 2 * SUBLANES) == 0 and n // LANES >= 2 * SUBLANES else 2 * SUBLANES
    n1 = n // n2
    assert n1 * n2 == n and n1 % (2 * SUBLANES) == 0 and n2 % SUBLANES == 0, (n1, n2)
    return n1, n2


def _f1_kernel(sig_ref, gm_ref, o_ref, *, nb, n1, n2, l2c):
    j = pl.program_id(1)
    hl = n1 // 2

    def body(t, carry):
        l2 = j * l2c + t
        parts = [sig_ref[b, pl.ds(l2, hl, stride=n2), :] for b in range(nb)]
        if nb == 1:
            parts.append(jnp.zeros_like(parts[0]))
        rhs = jnp.concatenate(parts, axis=0).astype(BF16)
        res = _dot(gm_ref[t], rhs)
        row0 = pl.multiple_of(t * K1_GROUP, K1_GROUP)
        for g in range(n1 // K1_GROUP):
            for ri in range(2):
                lo = ri * n1 + g * K1_GROUP
                o_ref[g, ri, pl.ds(row0, K1_GROUP), :] = res[lo:lo + K1_GROUP]
        return carry

    lax.fori_loop(0, l2c, body, 0)


def _fft_stage1(sig, gm, n1, n2, c):
    nb, seq, _ = sig.shape
    l2c = _pick(n2, 16, 1)
    return pl.pallas_call(
        functools.partial(_f1_kernel, nb=nb, n1=n1, n2=n2, l2c=l2c),
        grid=(c // LANES, n2 // l2c),
        in_specs=[pl.BlockSpec((nb, seq, LANES), lambda cb, j: (0, 0, cb)),
                  pl.BlockSpec((l2c, 2 * n1, n1), lambda cb, j: (j, 0, 0))],
        out_specs=pl.BlockSpec((n1 // K1_GROUP, 2, l2c * K1_GROUP, LANES), lambda cb, j: (0, 0, j, cb)),
        out_shape=jax.ShapeDtypeStruct((n1 // K1_GROUP, 2, n2 * K1_GROUP, c), F32),
        compiler_params=_cparams("parallel", "arbitrary"),
        name="fft_stage1",
    )(sig, gm)


def _stage2_rhs(y_ref, t, n2):
    re = y_ref.at[0][pl.ds(t, n2, stride=K1_GROUP), :]
    im = y_ref.at[1][pl.ds(t, n2, stride=K1_GROUP), :]
    return jnp.concatenate([re, im], axis=0).astype(BF16)


def _filter_spectrum_kernel(yf_ref, yb_ref, f2_ref, o_ref, *, n2):
    for t in range(K1_GROUP):
        xf = _dot(f2_ref[...], _stage2_rhs(yf_ref, t, n2))
        xb = _dot(f2_ref[...], _stage2_rhs(yb_ref, t, n2))
        o_ref[t, 0] = xf[:n2] + xb[:n2]
        o_ref[t, 1] = xf[n2:] - xb[n2:]


def _filter_spectrum(yfilt, f2, n1, n2, width_total):
    nblk = width_total // LANES
    return pl.pallas_call(
        functools.partial(_filter_spectrum_kernel, n2=n2),
        grid=(nblk, n1 // K1_GROUP),
        in_specs=[pl.BlockSpec((None, 2, n2 * K1_GROUP, LANES), lambda cb, g: (g, 0, 0, cb)),
                  pl.BlockSpec((None, 2, n2 * K1_GROUP, LANES), lambda cb, g: (g, 0, 0, nblk + cb)),
                  pl.BlockSpec((2 * n2, 2 * n2), lambda cb, g: (0, 0))],
        out_specs=pl.BlockSpec((K1_GROUP, 2, n2, LANES), lambda cb, g: (g, 0, 0, cb)),
        out_shape=jax.ShapeDtypeStruct((n1, 2, n2, width_total), F32),
        compiler_params=_cparams("parallel", "parallel"),
        name="filter_spectrum",
    )(yfilt, yfilt, f2)


def _f2_kernel(y_ref, k_ref, f2_ref, if2_ref, o_ref, *, n2):
    for t in range(K1_GROUP):
        x = _dot(f2_ref[...], _stage2_rhs(y_ref, t, n2))
        xr, xi = x[:n2], x[n2:]
        kr, ki = k_ref[t, 0], k_ref[t, 1]
        p = jnp.concatenate([xr * kr - xi * ki, xr * ki + xi * kr], axis=0).astype(BF16)
        yp = _dot(if2_ref[...], p)
        for grp in range(n2 // SUBLANES):
            for ri in range(2):
                lo = ri * n2 + grp * SUBLANES
                o_ref[grp, ri, t * SUBLANES:(t + 1) * SUBLANES, :] = yp[lo:lo + SUBLANES]


def _fft_stage2_filter(y, kspec, kcol_block0, f2, if2, n1, n2):
    c = y.shape[-1]
    return pl.pallas_call(
        functools.partial(_f2_kernel, n2=n2),
        grid=(c // LANES, n1 // K1_GROUP),
        in_specs=[pl.BlockSpec((None, 2, n2 * K1_GROUP, LANES), lambda cb, g: (g, 0, 0, cb)),
                  pl.BlockSpec((K1_GROUP, 2, n2, LANES), lambda cb, g: (g, 0, 0, kcol_block0 + cb)),
                  pl.BlockSpec((2 * n2, 2 * n2), lambda cb, g: (0, 0)),
                  pl.BlockSpec((2 * n2, 2 * n2), lambda cb, g: (0, 0))],
        out_specs=pl.BlockSpec((n2 // SUBLANES, 2, K1_GROUP * SUBLANES, LANES), lambda cb, g: (0, 0, g, cb)),
        out_shape=jax.ShapeDtypeStruct((n2 // SUBLANES, 2, n1 * SUBLANES, c), F32),
        compiler_params=_cparams("parallel", "parallel"),
        name="fft_stage2_filter",
    )(y, kspec, f2, if2)


def _f3_kernel(y_ref, igm_ref, o_ref, *, n1, n2):
    grp = pl.program_id(1)
    hl = n1 // 2
    for s in range(SUBLANES):
        l2 = grp * SUBLANES + s
        re = y_ref.at[0][pl.ds(s, n1, stride=SUBLANES), :]
        im = y_ref.at[1][pl.ds(s, n1, stride=SUBLANES), :]
        rhs = jnp.concatenate([re, im], axis=0).astype(BF16)
        res = _dot(igm_ref[s], rhs)
        for b in range(2):
            o_ref[b, pl.ds(l2, hl, stride=n2), :] = res[b * hl:(b + 1) * hl]


def _fft_inverse_stage1(yp, igm, n1, n2):
    c = yp.shape[-1]
    seq = n1 * n2 // 2
    return pl.pallas_call(
        functools.partial(_f3_kernel, n1=n1, n2=n2),
        grid=(c // LANES, n2 // SUBLANES),
        in_specs=[pl.BlockSpec((None, 2, n1 * SUBLANES, LANES), lambda cb, g: (g, 0, 0, cb)),
                  pl.BlockSpec((SUBLANES, n1, 2 * n1), lambda cb, g: (g, 0, 0))],
        out_specs=pl.BlockSpec((2, seq, LANES), lambda cb, g: (0, 0, cb)),
        out_shape=jax.ShapeDtypeStruct((2, seq, c), F32),
        compiler_params=_cparams("parallel", "arbitrary"),
        name="fft_inverse_stage1",
    )(yp, igm)


def _gate_kernel(c_ref, u_ref, g_ref, b_ref, o_ref):
    o_ref[...] = (g_ref[...] * (c_ref[...] + b_ref[...] * u_ref[...])).astype(o_ref.dtype)


def _gated(conv, u, u_col0, gate, gate_col0, bias, out_dtype):
    bsz, seq, c = conv.shape
    tl = _pick(seq, 1024, 16)
    tc = _pick(c, 512, LANES)
    assert u_col0 % tc == 0 and gate_col0 % tc == 0
    return pl.pallas_call(
        _gate_kernel,
        grid=(bsz, seq // tl, c // tc),
        in_specs=[pl.BlockSpec((1, tl, tc), lambda bb, i, j: (bb, i, j)),
                  pl.BlockSpec((1, tl, tc), lambda bb, i, j: (bb, i, u_col0 // tc + j)),
                  pl.BlockSpec((1, tl, tc), lambda bb, i, j: (bb, i, gate_col0 // tc + j)),
                  pl.BlockSpec((1, tc), lambda bb, i, j: (0, j))],
        out_specs=pl.BlockSpec((1, tl, tc), lambda bb, i, j: (bb, i, j)),
        out_shape=jax.ShapeDtypeStruct((bsz, seq, c), out_dtype),
        compiler_params=_cparams("parallel", "parallel", "parallel"),
        name="hyena_gate",
    )(conv, u, gate, bias)


def _hyena(proj, conv_w, conv_b, fw1, fb1, fw2, fb2, fw3, freq, hy_bias):
    bsz, seq, _ = proj.shape
    assert bsz == 2, "the FFT convolution packs exactly two batch elements into one complex signal"
    width = hy_bias.shape[1]
    n1, n2 = _fft_split(seq)
    gm, f2, if2, igm = _dft_tables(n1, n2)

    filt = _hyena_filters(seq, fw1, fb1, fw2, fb2, fw3, freq, width)
    kspec = _filter_spectrum(_fft_stage1(filt, gm, n1, n2, 4 * width), f2, n1, n2, 2 * width)

    u = _short_conv(proj, conv_w, conv_b[None, :])
    sig = u
    out = None
    for order in range(2):
        y = _fft_stage1(sig, gm, n1, n2, width)
        yp = _fft_stage2_filter(y, kspec, order * (width // LANES), f2, if2, n1, n2)
        c = _fft_inverse_stage1(yp, igm, n1, n2)
        if order == 0:
            sig = _gated(c, u, 0, u, width, hy_bias[0:1], F32)
        else:
            out = _gated(c, sig, 0, u, 2 * width, hy_bias[1:2], BF16)
    return out


HEAD_LANES = 2 * LANES


def _rope_lane_map():
    lane = np.arange(LANES)
    half = lane // 64
    rem = lane % 64
    valid = rem < 32
    axis = rem // 16
    f = rem % 16
    src = np.where(valid, axis * 32 + half * 16 + f, -1)
    tab = np.where(valid, axis * 16 + f, -1)
    return src, tab, half


def _gather_cols(w, idx):
    wp = jnp.concatenate([w, jnp.zeros(w.shape[:-1] + (1,), w.dtype)], axis=-1)
    return jnp.take(wp, jnp.asarray(np.where(idx < 0, w.shape[-1], idx)), axis=-1)


def _head_col_index(heads, nope, per_head, rope_off):
    src, _, _ = _rope_lane_map()
    idx = []
    for h in range(heads):
        base = h * per_head
        idx.append(base + np.arange(nope))
        idx.append(np.where(src < 0, -1, base + rope_off + src))
    return np.concatenate(idx)


def _rope_tables(seq):
    rows = seq // GRID_W
    row = jnp.repeat(jnp.arange(rows, dtype=F32), GRID_W)
    col = jnp.tile(jnp.arange(GRID_W, dtype=F32), rows)
    half = MLA_ROPE // 2
    inv = ROPE_THETA ** (-jnp.arange(0, half, 2, dtype=F32) / half)
    ang = jnp.concatenate([row[:, None] * inv, col[:, None] * inv], axis=-1)
    _, tab, hf = _rope_lane_map()
    cos = _gather_cols(jnp.cos(ang), tab)
    sin = _gather_cols(jnp.sin(ang), tab) * jnp.asarray(np.where(hf == 0, -1.0, 1.0), F32)
    return cos, sin


def _rms(x, g):
    return x * lax.rsqrt(jnp.mean(x * x, axis=-1, keepdims=True) + NORM_EPS) * g


def _q_kernel(qa_ref, g_ref, w_ref, gn_ref, cos_ref, sin_ref, o_ref, *, heads, qk_dim, out_scale):
    xn = _rms(qa_ref[...], g_ref[...]).astype(BF16)
    q = _dot(xn, w_ref[...])
    gn = gn_ref[...]
    cos, sin = cos_ref[...], sin_ref[...]
    for h in range(heads):
        qh = q[:, h * HEAD_LANES:(h + 1) * HEAD_LANES]
        inv = lax.rsqrt(jnp.sum(qh * qh, axis=-1, keepdims=True) / qk_dim + NORM_EPS)
        qn = qh * inv * gn
        r = qn[:, LANES:]
        r = r * cos + pltpu.roll(r, 64, 1) * sin
        o_ref[0, h, :, 0:LANES] = (qn[:, :LANES] * out_scale).astype(BF16)
        o_ref[0, h, :, LANES:] = (r * out_scale).astype(BF16)


def _mla_queries(proj, col_block, g_qa, w_q, gn, cos, sin, heads, qk_dim, out_scale):
    bsz, seq, _ = proj.shape
    r = g_qa.shape[1]
    tm = _pick(seq, 512, 16)
    return pl.pallas_call(
        functools.partial(_q_kernel, heads=heads, qk_dim=qk_dim, out_scale=out_scale),
        grid=(bsz, seq // tm),
        in_specs=[pl.BlockSpec((None, tm, r), lambda b, i: (b, i, col_block)),
                  pl.BlockSpec((1, r), lambda b, i: (0, 0)),
                  pl.BlockSpec((r, heads * HEAD_LANES), lambda b, i: (0, 0)),
                  pl.BlockSpec((1, HEAD_LANES), lambda b, i: (0, 0)),
                  pl.BlockSpec((tm, LANES), lambda b, i: (i, 0)),
                  pl.BlockSpec((tm, LANES), lambda b, i: (i, 0))],
        out_specs=pl.BlockSpec((1, heads, tm, HEAD_LANES), lambda b, i: (b, 0, i, 0)),
        out_shape=jax.ShapeDtypeStruct((bsz, heads, seq, HEAD_LANES), BF16),
        compiler_params=_cparams("parallel", "parallel"),
        name="mla_queries",
    )(proj, g_qa, w_q, gn, cos, sin)


def _kv_kernel(kva_ref, kr_ref, g_ref, w_ref, gn_ref, *rest, heads, qk_dim, rope):
    if rope:
        cos_ref, sin_ref, k_ref, v_ref = rest
    else:
        k_ref, v_ref = rest
    xn = _rms(kva_ref[...], g_ref[...]).astype(BF16)
    kv = _dot(xn, w_ref[...])
    gn = gn_ref[...]
    kr = kr_ref[...]
    ssr = jnp.sum(kr * kr, axis=-1, keepdims=True)
    krg = kr * gn[:, LANES:]
    if rope:
        krg = krg * cos_ref[...] + pltpu.roll(krg, 64, 1) * sin_ref[...]
    for h in range(heads):
        kn = kv[:, h * LANES:(h + 1) * LANES]
        inv = lax.rsqrt((jnp.sum(kn * kn, axis=-1, keepdims=True) + ssr) / qk_dim + NORM_EPS)
        k_ref[0, h, :, 0:LANES] = (kn * inv * gn[:, :LANES]).astype(BF16)
        k_ref[0, h, :, LANES:] = (krg * inv).astype(BF16)
        v_ref[0, h] = kv[:, (heads + h) * LANES:(heads + h + 1) * LANES].astype(BF16)


def _mla_keys_values(proj, kv_block, kr_block, g_kva, w_kv, gn, rope, heads, qk_dim):
    bsz, seq, _ = proj.shape
    r = g_kva.shape[1]
    tm = _pick(seq, 512, 16)
    in_specs = [pl.BlockSpec((None, tm, r), lambda b, i: (b, i, kv_block)),
                pl.BlockSpec((None, tm, LANES), lambda b, i: (b, i, kr_block)),
                pl.BlockSpec((1, r), lambda b, i: (0, 0)),
                pl.BlockSpec((r, 2 * heads * LANES), lambda b, i: (0, 0)),
                pl.BlockSpec((1, HEAD_LANES), lambda b, i: (0, 0))]
    args = [proj, proj, g_kva, w_kv, gn]
    if rope is not None:
        in_specs += [pl.BlockSpec((tm, LANES), lambda b, i: (i, 0))] * 2
        args += list(rope)
    return pl.pallas_call(
        functools.partial(_kv_kernel, heads=heads, qk_dim=qk_dim, rope=rope is not None),
        grid=(bsz, seq // tm),
        in_specs=in_specs,
        out_specs=[pl.BlockSpec((1, heads, tm, HEAD_LANES), lambda b, i: (b, 0, i, 0)),
                   pl.BlockSpec((1, heads, tm, LANES), lambda b, i: (b, 0, i, 0))],
        out_shape=[jax.ShapeDtypeStruct((bsz, heads, seq, HEAD_LANES), BF16),
                   jax.ShapeDtypeStruct((bsz, heads, seq, LANES), BF16)],
        compiler_params=_cparams("parallel", "parallel"),
        name="mla_keys_values",
    )(*args)


def _attn_kernel(q_ref, k_ref, v_ref, kc_ref, vc_ref, o_ref, m_ref, l_ref, acc_ref):
    kj = pl.program_id(3)
    q = q_ref[0, 0]

    def scores(k):
        return lax.dot_general(q, k, (((1,), (1,)), ((), ())), preferred_element_type=F32)

    @pl.when(kj == 0)
    def _():
        s = scores(kc_ref[0, 0])
        m = jnp.max(s, axis=-1, keepdims=True)
        p = jnp.exp(s - m)
        m_ref[...] = m
        l_ref[...] = jnp.sum(p, axis=-1, keepdims=True)
        acc_ref[...] = _dot(p.astype(BF16), vc_ref[0, 0])

    s = scores(k_ref[0, 0])
    m_old = m_ref[...]
    m_new = jnp.maximum(m_old, jnp.max(s, axis=-1, keepdims=True))
    alpha = jnp.exp(m_old - m_new)
    p = jnp.exp(s - m_new)
    l_ref[...] = alpha * l_ref[...] + jnp.sum(p, axis=-1, keepdims=True)
    acc_ref[...] = alpha * acc_ref[...] + _dot(p.astype(BF16), v_ref[0, 0])
    m_ref[...] = m_new

    @pl.when(kj == pl.num_programs(3) - 1)
    def _():
        o_ref[0] = (acc_ref[...] / l_ref[...]).astype(o_ref.dtype)


def _attention(q, k, v, kc, vc):
    bsz, heads, seq, _ = q.shape
    lc = kc.shape[2]
    tq = _pick(seq, 512, 16)
    tk = _pick(seq, 1024, LANES)
    return pl.pallas_call(
        _attn_kernel,
        grid=(bsz, heads, seq // tq, seq // tk),
        in_specs=[pl.BlockSpec((1, 1, tq, HEAD_LANES), lambda b, h, i, j: (b, h, i, 0)),
                  pl.BlockSpec((1, 1, tk, HEAD_LANES), lambda b, h, i, j: (b, h, j, 0)),
                  pl.BlockSpec((1, 1, tk, LANES), lambda b, h, i, j: (b, h, j, 0)),
                  pl.BlockSpec((1, 1, lc, HEAD_LANES), lambda b, h, i, j: (b, h, 0, 0)),
                  pl.BlockSpec((1, 1, lc, LANES), lambda b, h, i, j: (b, h, 0, 0))],
        out_specs=pl.BlockSpec((1, tq, LANES), lambda b, h, i, j: (b, i, h)),
        out_shape=jax.ShapeDtypeStruct((bsz, seq, heads * LANES), BF16),
        scratch_shapes=[pltpu.VMEM((tq, 1), F32), pltpu.VMEM((tq, 1), F32), pltpu.VMEM((tq, LANES), F32)],
        compiler_params=_cparams("parallel", "parallel", "parallel", "arbitrary"),
        name="attention",
    )(q, k, v, kc, vc)


def kernel(x, c, ctx, c_ctx, norm1_g, norm2_g, w_ada, b_ada, w_in, hy_conv_w, hy_conv_b, hy_filt_w1, hy_filt_b1, hy_filt_w2, hy_filt_b2, hy_filt_w3, hy_freq, hy_bias, mla_g_qa, mla_w_qb, mla_g_kva, mla_w_kvb, mla_q_norm_g, mla_k_norm_g, w_out, w_mlp1, w_mlp2):
    assert w_ada.shape[0] == 1, "single-layer block"
    bsz, seq, d = x.shape
    lc = ctx.shape[1]
    hyc = hy_conv_b.shape[1]
    width = hy_bias.shape[2]
    q_lora = mla_g_qa.shape[1]
    kv_lora = mla_g_kva.shape[1]
    qk_dim = mla_q_norm_g.shape[1]
    nope = qk_dim - MLA_ROPE
    heads = mla_w_qb.shape[2] // qk_dim
    v_dim = mla_w_kvb.shape[2] // heads - nope
    assert nope == LANES and v_dim == LANES and seq % GRID_W == 0
    q0, kv0, kr0 = hyc, hyc + q_lora, hyc + q_lora + kv_lora
    assert q0 % q_lora == 0 and kv0 % kv_lora == 0 and kr0 % LANES == 0

    cc = jnp.zeros((8, d), F32).at[:bsz].set(c).at[bsz].set(c_ctx)
    mod = _adaln(cc, w_ada[0], b_ada)
    chunk = lambda i: mod[:bsz, i * d:(i + 1) * d][:, None, :]
    sh1, sc1, g1, sh2, sc2, g2 = [chunk(i) for i in range(6)]
    csh1 = mod[bsz:bsz + 1, 0:d][:, None, :]
    csc1 = mod[bsz:bsz + 1, d:2 * d][:, None, :]

    rope_src, _, _ = _rope_lane_map()
    w_in0 = w_in[0]
    np_cols = kr0 + LANES
    np_cols += (-np_cols) % 1024
    w_kr = _gather_cols(w_in0[:, kr0:], rope_src)
    w_pad = jnp.concatenate([w_in0[:, :kr0], w_kr, jnp.zeros((d, np_cols - kr0 - LANES), F32)], axis=1).astype(BF16)
    w_q = _gather_cols(mla_w_qb[0], _head_col_index(heads, nope, qk_dim, nope)).astype(BF16)
    kv_idx = np.concatenate([h * (nope + v_dim) + np.arange(nope) for h in range(heads)]
                            + [h * (nope + v_dim) + nope + np.arange(v_dim) for h in range(heads)])
    w_kv = _gather_cols(mla_w_kvb[0], kv_idx).astype(BF16)
    head_idx = _head_col_index(1, nope, qk_dim, nope)
    gq = _gather_cols(mla_q_norm_g[0][None, :], head_idx)
    gk = _gather_cols(mla_k_norm_g[0][None, :], head_idx)
    cos, sin = _rope_tables(seq)

    ctx_cols = kv0 - kv0 % 1024
    proj_c = _norm_mod_matmul(ctx.reshape(bsz * lc, d), norm1_g, csh1, csc1, w_pad[:, ctx_cols:],
                              bsz * lc, F32, False, 512, 1024).reshape(bsz, lc, np_cols - ctx_cols)
    k_c, v_c = _mla_keys_values(proj_c, (kv0 - ctx_cols) // kv_lora, (kr0 - ctx_cols) // LANES,
                                mla_g_kva, w_kv, gk, None, heads, qk_dim)

    x2 = x.reshape(bsz * seq, d)
    proj = _norm_mod_matmul(x2, norm1_g, sh1, sc1, w_pad, seq, F32, False, 512, 1024).reshape(bsz, seq, np_cols)
    y_hy = _hyena(proj, hy_conv_w[0], hy_conv_b[0], hy_filt_w1[0], hy_filt_b1[0], hy_filt_w2[0],
                  hy_filt_b2[0], hy_filt_w3[0], hy_freq[0], hy_bias[0])
    q = _mla_queries(proj, q0 // q_lora, mla_g_qa, w_q, gq, cos, sin, heads, qk_dim, qk_dim ** -0.5)
    k, v = _mla_keys_values(proj, kv0 // kv_lora, kr0 // LANES, mla_g_kva, w_kv, gk, (cos, sin), heads, qk_dim)
    y_att = _attention(q, k, v, k_c, v_c)

    x1 = _out_proj(y_hy.reshape(bsz * seq, width), y_att.reshape(bsz * seq, heads * v_dim),
                   w_out[0].astype(BF16), x2, g1, seq)
    hmid = _norm_mod_matmul(x1, norm2_g, sh2, sc2, w_mlp1[0].astype(BF16), seq, BF16, True, 512, 1024)
    out = _matmul_gated_residual(hmid, w_mlp2[0].astype(BF16), x1, g2, seq)
    return out.reshape(bsz, seq, d)
```

```python
import functools
import math

import numpy as np
import jax
import jax.numpy as jnp
from jax import lax
from jax.experimental import pallas as pl
from jax.experimental.pallas import tpu as pltpu

F32 = jnp.float32
BF16 = jnp.bfloat16

NORM_EPS = 1e-6
GRID_W = 64
MLA_ROPE = 64
ROPE_THETA = 10000.0
HY_POS_BANDS = 16
HY_DECAY_TARGET = 1e-2
HY_FAST_DECAY_PCT = 0.3
HY_SLOW_DECAY_PCT = 1.5

LANES = 128
SUBLANES = 8
VMEM_LIMIT = 56 * 1024 * 1024


def _cparams(*sem):
    return pltpu.CompilerParams(dimension_semantics=sem, vmem_limit_bytes=VMEM_LIMIT)


def _pick(dim, pref, align):
    t = min(pref, dim)
    t -= t % align
    while t >= align:
        if dim % t == 0:
            return t
        t -= align
    return dim


def _dot(a, b):
    return jnp.dot(a, b, preferred_element_type=F32)


def _adaln_kernel(c_ref, w_ref, b_ref, o_ref):
    c = c_ref[...]
    s = c * jax.nn.sigmoid(c)
    o_ref[...] = _dot(s.astype(BF16), w_ref[...].astype(BF16)) + b_ref[...]


def _adaln(cc, w, b):
    d, n = w.shape
    tn = _pick(n, 512, LANES)
    return pl.pallas_call(
        _adaln_kernel,
        grid=(n // tn,),
        in_specs=[pl.BlockSpec((8, d), lambda j: (0, 0)),
                  pl.BlockSpec((d, tn), lambda j: (0, j)),
                  pl.BlockSpec((1, tn), lambda j: (0, j))],
        out_specs=pl.BlockSpec((8, tn), lambda j: (0, j)),
        out_shape=jax.ShapeDtypeStruct((8, n), F32),
        compiler_params=_cparams("parallel"),
        name="adaln",
    )(cc, w, b)


def _nmm_kernel(x_ref, g_ref, sh_ref, sc_ref, w_ref, o_ref, h_ref, *, sq_relu):
    @pl.when(pl.program_id(1) == 0)
    def _():
        x = x_ref[...]
        ms = jnp.mean(x * x, axis=-1, keepdims=True)
        y = x * lax.rsqrt(ms + NORM_EPS) * g_ref[...]
        h_ref[...] = (y * (1.0 + sc_ref[0]) + sh_ref[0]).astype(BF16)

    acc = _dot(h_ref[...], w_ref[...])
    if sq_relu:
        acc = jnp.square(jnp.maximum(acc, 0.0))
    o_ref[...] = acc.astype(o_ref.dtype)


def _norm_mod_matmul(x, g, shift, scale, w, rows_per_mod, out_dtype, sq_relu, tm_pref, tn_pref):
    m, k = x.shape
    n = w.shape[1]
    tm = _pick(math.gcd(m, rows_per_mod), tm_pref, 16)
    tn = _pick(n, tn_pref, LANES)
    mod_idx = lambda i, j: ((i * tm) // rows_per_mod, 0, 0)
    return pl.pallas_call(
        functools.partial(_nmm_kernel, sq_relu=sq_relu),
        grid=(m // tm, n // tn),
        in_specs=[pl.BlockSpec((tm, k), lambda i, j: (i, 0)),
                  pl.BlockSpec((1, k), lambda i, j: (0, 0)),
                  pl.BlockSpec((1, 1, k), mod_idx),
                  pl.BlockSpec((1, 1, k), mod_idx),
                  pl.BlockSpec((k, tn), lambda i, j: (0, j))],
        out_specs=pl.BlockSpec((tm, tn), lambda i, j: (i, j)),
        out_shape=jax.ShapeDtypeStruct((m, n), out_dtype),
        scratch_shapes=[pltpu.VMEM((tm, k), BF16)],
        compiler_params=_cparams("parallel", "arbitrary"),
        name="norm_mod_matmul",
    )(x, g, shift, scale, w)


def _outproj_kernel(a0_ref, a1_ref, w_ref, res_ref, gate_ref, o_ref, *, k0):
    acc = _dot(a0_ref[...], w_ref[0:k0, :]) + _dot(a1_ref[...], w_ref[k0:, :])
    o_ref[...] = res_ref[...] + gate_ref[0] * acc


def _out_proj(a0, a1, w, res, gate, rows_per_mod):
    m, k0 = a0.shape
    k1 = a1.shape[1]
    n = w.shape[1]
    tm = _pick(math.gcd(m, rows_per_mod), 1024, 16)
    tn = _pick(n, 1024, LANES)
    return pl.pallas_call(
        functools.partial(_outproj_kernel, k0=k0),
        grid=(m // tm, n // tn),
        in_specs=[pl.BlockSpec((tm, k0), lambda i, j: (i, 0)),
                  pl.BlockSpec((tm, k1), lambda i, j: (i, 0)),
                  pl.BlockSpec((k0 + k1, tn), lambda i, j: (0, j)),
                  pl.BlockSpec((tm, tn), lambda i, j: (i, j)),
                  pl.BlockSpec((1, 1, tn), lambda i, j: ((i * tm) // rows_per_mod, 0, j))],
        out_specs=pl.BlockSpec((tm, tn), lambda i, j: (i, j)),
        out_shape=jax.ShapeDtypeStruct((m, n), F32),
        compiler_params=_cparams("parallel", "parallel"),
        name="out_proj",
    )(a0, a1, w, res, gate)


def _mmres_kernel(a_ref, w_ref, res_ref, gate_ref, o_ref, acc_ref):
    kk = pl.program_id(2)

    @pl.when(kk == 0)
    def _():
        acc_ref[...] = jnp.zeros_like(acc_ref)

    acc_ref[...] += _dot(a_ref[...], w_ref[...])

    @pl.when(kk == pl.num_programs(2) - 1)
    def _():
        o_ref[...] = res_ref[...] + gate_ref[0] * acc_ref[...]


def _matmul_gated_residual(a, w, res, gate, rows_per_mod):
    m, k = a.shape
    n = w.shape[1]
    tm = _pick(math.gcd(m, rows_per_mod), 1024, 16)
    tn = _pick(n, 1024, LANES)
    tk = _pick(k, 2048, LANES)
    return pl.pallas_call(
        _mmres_kernel,
        grid=(m // tm, n // tn, k // tk),
        in_specs=[pl.BlockSpec((tm, tk), lambda i, j, kk: (i, kk)),
                  pl.BlockSpec((tk, tn), lambda i, j, kk: (kk, j)),
                  pl.BlockSpec((tm, tn), lambda i, j, kk: (i, j)),
                  pl.BlockSpec((1, 1, tn), lambda i, j, kk: ((i * tm) // rows_per_mod, 0, j))],
        out_specs=pl.BlockSpec((tm, tn), lambda i, j, kk: (i, j)),
        out_shape=jax.ShapeDtypeStruct((m, n), F32),
        scratch_shapes=[pltpu.VMEM((tm, tn), F32)],
        compiler_params=_cparams("parallel", "parallel", "arbitrary"),
        name="matmul_gated_residual",
    )(a, w, res, gate)


def _sconv_kernel(p_ref, pp_ref, pn_ref, w_ref, b_ref, o_ref):
    i = pl.program_id(1)
    x = p_ref[0]
    tl = x.shape[0]
    rows = lax.broadcasted_iota(jnp.int32, x.shape, 0)
    prev_row = jnp.where(i > 0, pp_ref[0, SUBLANES - 1:SUBLANES, :], 0.0)
    next_row = jnp.where(i < pl.num_programs(1) - 1, pn_ref[0, 0:1, :], 0.0)
    up = jnp.where(rows == 0, prev_row, pltpu.roll(x, 1, 0))
    dn = jnp.where(rows == tl - 1, next_row, pltpu.roll(x, tl - 1, 0))
    o_ref[0] = up * w_ref[0:1, :] + x * w_ref[1:2, :] + dn * w_ref[2:3, :] + b_ref[...]


def _short_conv(proj, w, b):
    bsz, seq, _ = proj.shape
    c = w.shape[1]
    tl = _pick(seq, 512, SUBLANES)
    tc = _pick(c, 512, LANES)
    r = tl // SUBLANES
    last = seq // SUBLANES - 1
    return pl.pallas_call(
        _sconv_kernel,
        grid=(bsz, seq // tl, c // tc),
        in_specs=[pl.BlockSpec((1, tl, tc), lambda bb, i, j: (bb, i, j)),
                  pl.BlockSpec((1, SUBLANES, tc), lambda bb, i, j: (bb, jnp.maximum(i * r - 1, 0), j)),
                  pl.BlockSpec((1, SUBLANES, tc), lambda bb, i, j: (bb, jnp.minimum((i + 1) * r, last), j)),
                  pl.BlockSpec((3, tc), lambda bb, i, j: (0, j)),
                  pl.BlockSpec((1, tc), lambda bb, i, j: (0, j))],
        out_specs=pl.BlockSpec((1, tl, tc), lambda bb, i, j: (bb, i, j)),
        out_shape=jax.ShapeDtypeStruct((bsz, seq, c), F32),
        compiler_params=_cparams("parallel", "parallel", "parallel"),
        name="short_conv",
    )(proj, proj, proj, w, b)


def _filt_kernel(feat_ref, t_ref, w1_ref, b1_ref, w2_ref, b2_ref, fr_ref, w3_ref, dl_ref, o_ref, h_ref, *, seq):
    i = pl.program_id(0)
    j = pl.program_id(1)

    @pl.when(j == 0)
    def _():
        fr = fr_ref[...]
        h = jnp.sin(fr * (_dot(feat_ref[...].astype(BF16), w1_ref[...]) + b1_ref[...]))
        h = jnp.sin(fr * (_dot(h.astype(BF16), w2_ref[...]) + b2_ref[...]))
        h_ref[...] = h.astype(BF16)

    f = _dot(h_ref[...], w3_ref[...]) * jnp.exp(-t_ref[...] * dl_ref[...])
    rows = lax.broadcasted_iota(jnp.int32, f.shape, 0) + i * f.shape[0]
    o_ref[0] = jnp.where(rows == seq, 0.0, f)


def _hyena_filters(seq, w1, b1, w2, b2, w3, freq, width):
    lag = jnp.arange(2 * seq)
    src = jnp.where(lag < seq, lag, jnp.clip(2 * seq - lag, 0, seq - 1))
    pos = jnp.arange(seq, dtype=F32)
    t = jnp.linspace(0.0, 1.0, seq, dtype=F32)[:, None]
    bands = jnp.linspace(1e-4, HY_POS_BANDS - 1, HY_POS_BANDS, dtype=F32)
    ang = (2.0 * math.pi / seq) * pos[:, None] * bands[None, :]
    feats = jnp.concatenate([t, jnp.cos(ang), -jnp.sin(ang)], axis=-1)
    pd = feats.shape[1]
    hid = w1.shape[1]
    hp = LANES
    feats = jnp.pad(feats, ((0, 0), (0, hp - pd)))[src]
    t = t[src]
    w1p = jnp.pad(w1, ((0, hp - pd), (0, hp - hid))).astype(BF16)
    w2p = jnp.pad(w2, ((0, hp - hid), (0, hp - hid))).astype(BF16)
    w3p = jnp.pad(w3, ((0, hp - hid), (0, 0))).astype(BF16)
    b1p = jnp.pad(b1, (0, hp - hid))[None, :]
    b2p = jnp.pad(b2, (0, hp - hid))[None, :]
    frp = jnp.pad(freq, (0, hp - hid))[None, :]
    deltas = jnp.abs(jnp.linspace(math.log(HY_DECAY_TARGET) / HY_SLOW_DECAY_PCT,
                                  math.log(HY_DECAY_TARGET) / HY_FAST_DECAY_PCT, width, dtype=F32))
    ncol = w3.shape[1] // 2
    dl = jnp.tile(deltas, ncol // width)[None, :]
    tl = _pick(seq, 512, SUBLANES)
    tn = _pick(ncol, 2048, LANES)
    fwd_tiles = seq // tl
    return pl.pallas_call(
        functools.partial(_filt_kernel, seq=seq),
        grid=(2 * seq // tl, ncol // tn),
        in_specs=[pl.BlockSpec((tl, hp), lambda i, j: (i, 0)),
                  pl.BlockSpec((tl, 1), lambda i, j: (i, 0)),
                  pl.BlockSpec((hp, hp), lambda i, j: (0, 0)),
                  pl.BlockSpec((1, hp), lambda i, j: (0, 0)),
                  pl.BlockSpec((hp, hp), lambda i, j: (0, 0)),
                  pl.BlockSpec((1, hp), lambda i, j: (0, 0)),
                  pl.BlockSpec((1, hp), lambda i, j: (0, 0)),
                  pl.BlockSpec((hp, tn), lambda i, j: (0, (i // fwd_tiles) * (ncol // tn) + j)),
                  pl.BlockSpec((1, tn), lambda i, j: (0, j))],
        out_specs=pl.BlockSpec((1, tl, tn), lambda i, j: (0, i, j)),
        out_shape=jax.ShapeDtypeStruct((1, 2 * seq, ncol), F32),
        scratch_shapes=[pltpu.VMEM((tl, hp), BF16)],
        compiler_params=_cparams("parallel", "arbitrary"),
        name="hyena_filters",
    )(feats, t, w1p, b1p, w2p, b2p, frp, w3p, dl)


K1_GROUP = SUBLANES


def _dft_tables(n1, n2):
    n = n1 * n2
    hl = n1 // 2
    k1 = np.arange(n1)[None, :, None]
    l1 = np.arange(hl)[None, None, :]
    l2 = np.arange(n2)[:, None, None]
    ang = -2.0 * np.pi * ((k1 * (n2 * l1 + l2)) % n) / n
    gr, gi = np.cos(ang), np.sin(ang)
    gm = np.concatenate([np.concatenate([gr, -gi], axis=2), np.concatenate([gi, gr], axis=2)], axis=1)
    a2 = -2.0 * np.pi * ((np.arange(n2)[:, None] * np.arange(n2)[None, :]) % n2) / n2
    fr, fi = np.cos(a2), np.sin(a2)
    f2 = np.block([[fr, -fi], [fi, fr]])
    if2 = np.block([[fr, fi], [-fi, fr]])
    ir, ii = np.transpose(gr, (0, 2, 1)) / n, -np.transpose(gi, (0, 2, 1)) / n
    igm = np.concatenate([np.concatenate([ir, -ii], axis=2), np.concatenate([ii, ir], axis=2)], axis=1)
    angf = -2.0 * np.pi * ((k1 * (n2 * np.arange(n1)[None, None, :] + l2)) % n) / n
    gmr = np.concatenate([np.cos(angf), np.sin(angf)], axis=1)
    to = lambda a: jnp.asarray(a.astype(np.float32)).astype(BF16)
    return to(gm), to(f2), to(if2), to(igm), to(gmr)


def _fft_split(seq):
    n = 2 * seq
    n2 = LANES if n % (LANES * 2 * SUBLANES) == 0 and n // LANES >= 2 * SUBLANES else 2 * SUBLANES
    n1 = n // n2
    assert n1 * n2 == n and n1 % (2 * SUBLANES) == 0 and n2 % SUBLANES == 0, (n1, n2)
    return n1, n2


def _f1_kernel(sig_ref, gm_ref, o_ref, *, nb, n1, n2, l2c):
    j = pl.program_id(1)
    rows = n1 // nb

    def body(t, carry):
        l2 = j * l2c + t
        parts = [sig_ref[b, pl.ds(l2, rows, stride=n2), :] for b in range(nb)]
        rhs = jnp.concatenate(parts, axis=0).astype(BF16)
        res = _dot(gm_ref[t], rhs)
        row0 = pl.multiple_of(t * K1_GROUP, K1_GROUP)
        for g in range(n1 // K1_GROUP):
            for ri in range(2):
                lo = ri * n1 + g * K1_GROUP
                o_ref[g, ri, pl.ds(row0, K1_GROUP), :] = res[lo:lo + K1_GROUP]
        return carry

    lax.fori_loop(0, l2c, body, 0, unroll=4)


def _fft_stage1(sig, gm, n1, n2, c):
    nb, seq, _ = sig.shape
    assert nb * seq == n1 * n2
    l2c = _pick(n2, 16, 4)
    return pl.pallas_call(
        functools.partial(_f1_kernel, nb=nb, n1=n1, n2=n2, l2c=l2c),
        grid=(c // LANES, n2 // l2c),
        in_specs=[pl.BlockSpec((nb, seq, LANES), lambda cb, j: (0, 0, cb)),
                  pl.BlockSpec((l2c, 2 * n1, n1), lambda cb, j: (j, 0, 0))],
        out_specs=pl.BlockSpec((n1 // K1_GROUP, 2, l2c * K1_GROUP, LANES), lambda cb, j: (0, 0, j, cb)),
        out_shape=jax.ShapeDtypeStruct((n1 // K1_GROUP, 2, n2 * K1_GROUP, c), F32),
        compiler_params=_cparams("parallel", "arbitrary"),
        name="fft_stage1",
    )(sig, gm)


def _stage2_rhs(y_ref, t, n2):
    part = lambda ri, tt: y_ref.at[ri][pl.ds(tt, n2, stride=K1_GROUP), :]
    re = jnp.concatenate([part(0, t), part(0, t + 1)], axis=1)
    im = jnp.concatenate([part(1, t), part(1, t + 1)], axis=1)
    return jnp.concatenate([re, im], axis=0).astype(BF16)


def _filter_spectrum_kernel(y_ref, f2_ref, o_ref, *, n2):
    for t in range(0, K1_GROUP, 2):
        x = _dot(f2_ref[...], _stage2_rhs(y_ref, t, n2))
        for d in range(2):
            o_ref[t + d, 0] = x[:n2, d * LANES:(d + 1) * LANES]
            o_ref[t + d, 1] = x[n2:, d * LANES:(d + 1) * LANES]


def _filter_spectrum(yfilt, f2, n1, n2):
    c = yfilt.shape[-1]
    return pl.pallas_call(
        functools.partial(_filter_spectrum_kernel, n2=n2),
        grid=(c // LANES, n1 // K1_GROUP),
        in_specs=[pl.BlockSpec((None, 2, n2 * K1_GROUP, LANES), lambda cb, g: (g, 0, 0, cb)),
                  pl.BlockSpec((2 * n2, 2 * n2), lambda cb, g: (0, 0))],
        out_specs=pl.BlockSpec((K1_GROUP, 2, n2, LANES), lambda cb, g: (g, 0, 0, cb)),
        out_shape=jax.ShapeDtypeStruct((n1, 2, n2, c), F32),
        compiler_params=_cparams("parallel", "parallel"),
        name="filter_spectrum",
    )(yfilt, f2)


def _f2_kernel(y_ref, k_ref, f2_ref, if2_ref, o_ref, *, n2):
    for t in range(0, K1_GROUP, 2):
        x = _dot(f2_ref[...], _stage2_rhs(y_ref, t, n2))
        xr, xi = x[:n2], x[n2:]
        kr = jnp.concatenate([k_ref[t, 0], k_ref[t + 1, 0]], axis=1)
        ki = jnp.concatenate([k_ref[t, 1], k_ref[t + 1, 1]], axis=1)
        p = jnp.concatenate([xr * kr - xi * ki, xr * ki + xi * kr], axis=0).astype(BF16)
        yp = _dot(if2_ref[...], p)
        for d in range(2):
            for grp in range(n2 // SUBLANES):
                for ri in range(2):
                    lo = ri * n2 + grp * SUBLANES
                    o_ref[grp, ri, (t + d) * SUBLANES:(t + d + 1) * SUBLANES, :] = (
                        yp[lo:lo + SUBLANES, d * LANES:(d + 1) * LANES])


def _fft_stage2_filter(y, kspec, kcol_block0, f2, if2, n1, n2):
    c = y.shape[-1]
    return pl.pallas_call(
        functools.partial(_f2_kernel, n2=n2),
        grid=(c // LANES, n1 // K1_GROUP),
        in_specs=[pl.BlockSpec((None, 2, n2 * K1_GROUP, LANES), lambda cb, g: (g, 0, 0, cb)),
                  pl.BlockSpec((K1_GROUP, 2, n2, LANES), lambda cb, g: (g, 0, 0, kcol_block0 + cb)),
                  pl.BlockSpec((2 * n2, 2 * n2), lambda cb, g: (0, 0)),
                  pl.BlockSpec((2 * n2, 2 * n2), lambda cb, g: (0, 0))],
        out_specs=pl.BlockSpec((n2 // SUBLANES, 2, K1_GROUP * SUBLANES, LANES), lambda cb, g: (0, 0, g, cb)),
        out_shape=jax.ShapeDtypeStruct((n2 // SUBLANES, 2, n1 * SUBLANES, c), F32),
        compiler_params=_cparams("parallel", "parallel"),
        name="fft_stage2_filter",
    )(y, kspec, f2, if2)


def _f3_kernel(y_ref, igm_ref, o_ref, *, n1, n2):
    grp = pl.program_id(1)
    hl = n1 // 2
    for s in range(SUBLANES):
        l2 = grp * SUBLANES + s
        re = y_ref.at[0][pl.ds(s, n1, stride=SUBLANES), :]
        im = y_ref.at[1][pl.ds(s, n1, stride=SUBLANES), :]
        rhs = jnp.concatenate([re, im], axis=0).astype(BF16)
        res = _dot(igm_ref[s], rhs)
        for b in range(2):
            o_ref[b, pl.ds(l2, hl, stride=n2), :] = res[b * hl:(b + 1) * hl]


def _fft_inverse_stage1(yp, igm, n1, n2):
    c = yp.shape[-1]
    seq = n1 * n2 // 2
    return pl.pallas_call(
        functools.partial(_f3_kernel, n1=n1, n2=n2),
        grid=(c // LANES, n2 // SUBLANES),
        in_specs=[pl.BlockSpec((None, 2, n1 * SUBLANES, LANES), lambda cb, g: (g, 0, 0, cb)),
                  pl.BlockSpec((SUBLANES, n1, 2 * n1), lambda cb, g: (g, 0, 0))],
        out_specs=pl.BlockSpec((2, seq, LANES), lambda cb, g: (0, 0, cb)),
        out_shape=jax.ShapeDtypeStruct((2, seq, c), F32),
        compiler_params=_cparams("parallel", "arbitrary"),
        name="fft_inverse_stage1",
    )(yp, igm)


def _gate_kernel(c_ref, u_ref, g_ref, b_ref, o_ref):
    o_ref[...] = (g_ref[...] * (c_ref[...] + b_ref[...] * u_ref[...])).astype(o_ref.dtype)


def _gated(conv, u, u_col0, gate, gate_col0, bias, out_dtype):
    bsz, seq, c = conv.shape
    tl = _pick(seq, 1024, 16)
    tc = _pick(c, 512, LANES)
    assert u_col0 % tc == 0 and gate_col0 % tc == 0
    return pl.pallas_call(
        _gate_kernel,
        grid=(bsz, seq // tl, c // tc),
        in_specs=[pl.BlockSpec((1, tl, tc), lambda bb, i, j: (bb, i, j)),
                  pl.BlockSpec((1, tl, tc), lambda bb, i, j: (bb, i, u_col0 // tc + j)),
                  pl.BlockSpec((1, tl, tc), lambda bb, i, j: (bb, i, gate_col0 // tc + j)),
                  pl.BlockSpec((1, tc), lambda bb, i, j: (0, j))],
        out_specs=pl.BlockSpec((1, tl, tc), lambda bb, i, j: (bb, i, j)),
        out_shape=jax.ShapeDtypeStruct((bsz, seq, c), out_dtype),
        compiler_params=_cparams("parallel", "parallel", "parallel"),
        name="hyena_gate",
    )(conv, u, gate, bias)


def _hyena(proj, conv_w, conv_b, fw1, fb1, fw2, fb2, fw3, freq, hy_bias):
    bsz, seq, _ = proj.shape
    assert bsz == 2, "the FFT convolution packs exactly two batch elements into one complex signal"
    width = hy_bias.shape[1]
    n1, n2 = _fft_split(seq)
    gm, f2, if2, igm, gmr = _dft_tables(n1, n2)

    filt = _hyena_filters(seq, fw1, fb1, fw2, fb2, fw3, freq, width)
    kspec = _filter_spectrum(_fft_stage1(filt, gmr, n1, n2, 2 * width), f2, n1, n2)

    u = _short_conv(proj, conv_w, conv_b[None, :])
    sig = u
    out = None
    for order in range(2):
        y = _fft_stage1(sig, gm, n1, n2, width)
        yp = _fft_stage2_filter(y, kspec, order * (width // LANES), f2, if2, n1, n2)
        c = _fft_inverse_stage1(yp, igm, n1, n2)
        if order == 0:
            sig = _gated(c, u, 0, u, width, hy_bias[0:1], F32)
        else:
            out = _gated(c, sig, 0, u, 2 * width, hy_bias[1:2], BF16)
    return out


HEAD_LANES = 2 * LANES


def _rope_lane_map():
    lane = np.arange(LANES)
    half = lane // 64
    rem = lane % 64
    valid = rem < 32
    axis = rem // 16
    f = rem % 16
    src = np.where(valid, axis * 32 + half * 16 + f, -1)
    tab = np.where(valid, axis * 16 + f, -1)
    return src, tab, half


def _gather_cols(w, idx):
    wp = jnp.concatenate([w, jnp.zeros(w.shape[:-1] + (1,), w.dtype)], axis=-1)
    return jnp.take(wp, jnp.asarray(np.where(idx < 0, w.shape[-1], idx)), axis=-1)


def _head_col_index(heads, nope, per_head, rope_off):
    src, _, _ = _rope_lane_map()
    idx = []
    for h in range(heads):
        base = h * per_head
        idx.append(base + np.arange(nope))
        idx.append(np.where(src < 0, -1, base + rope_off + src))
    return np.concatenate(idx)


def _rope_tables(seq):
    rows = seq // GRID_W
    row = jnp.repeat(jnp.arange(rows, dtype=F32), GRID_W)
    col = jnp.tile(jnp.arange(GRID_W, dtype=F32), rows)
    half = MLA_ROPE // 2
    inv = ROPE_THETA ** (-jnp.arange(0, half, 2, dtype=F32) / half)
    ang = jnp.concatenate([row[:, None] * inv, col[:, None] * inv], axis=-1)
    _, tab, hf = _rope_lane_map()
    cos = _gather_cols(jnp.cos(ang), tab)
    sin = _gather_cols(jnp.sin(ang), tab) * jnp.asarray(np.where(hf == 0, -1.0, 1.0), F32)
    return cos, sin


def _rms(x, g):
    return x * lax.rsqrt(jnp.mean(x * x, axis=-1, keepdims=True) + NORM_EPS) * g


def _q_kernel(qa_ref, g_ref, w_ref, gn_ref, cos_ref, sin_ref, o_ref, *, heads, qk_dim, out_scale):
    xn = _rms(qa_ref[...], g_ref[...]).astype(BF16)
    q = _dot(xn, w_ref[...])
    gn = gn_ref[...]
    cos, sin = cos_ref[...], sin_ref[...]
    for h in range(heads):
        qh = q[:, h * HEAD_LANES:(h + 1) * HEAD_LANES]
        inv = lax.rsqrt(jnp.sum(qh * qh, axis=-1, keepdims=True) / qk_dim + NORM_EPS)
        qn = qh * inv * gn
        r = qn[:, LANES:]
        r = r * cos + pltpu.roll(r, 64, 1) * sin
        o_ref[0, h, :, 0:LANES] = (qn[:, :LANES] * out_scale).astype(BF16)
        o_ref[0, h, :, LANES:] = (r * out_scale).astype(BF16)


def _mla_queries(proj, col_block, g_qa, w_q, gn, cos, sin, heads, qk_dim, out_scale):
    bsz, seq, _ = proj.shape
    r = g_qa.shape[1]
    tm = _pick(seq, 512, 16)
    return pl.pallas_call(
        functools.partial(_q_kernel, heads=heads, qk_dim=qk_dim, out_scale=out_scale),
        grid=(bsz, seq // tm),
        in_specs=[pl.BlockSpec((None, tm, r), lambda b, i: (b, i, col_block)),
                  pl.BlockSpec((1, r), lambda b, i: (0, 0)),
                  pl.BlockSpec((r, heads * HEAD_LANES), lambda b, i: (0, 0)),
                  pl.BlockSpec((1, HEAD_LANES), lambda b, i: (0, 0)),
                  pl.BlockSpec((tm, LANES), lambda b, i: (i, 0)),
                  pl.BlockSpec((tm, LANES), lambda b, i: (i, 0))],
        out_specs=pl.BlockSpec((1, heads, tm, HEAD_LANES), lambda b, i: (b, 0, i, 0)),
        out_shape=jax.ShapeDtypeStruct((bsz, heads, seq, HEAD_LANES), BF16),
        compiler_params=_cparams("parallel", "parallel"),
        name="mla_queries",
    )(proj, g_qa, w_q, gn, cos, sin)


V_ROWS = LANES + 16


def _kv_kernel(kva_ref, kr_ref, g_ref, wk_ref, wvt_ref, gn_ref, *rest, heads, qk_dim, rope):
    if rope:
        cos_ref, sin_ref, k_ref, v_ref = rest
    else:
        k_ref, v_ref = rest
    xn = _rms(kva_ref[...], g_ref[...]).astype(BF16)
    kk = _dot(xn, wk_ref[...])
    vt = lax.dot_general(wvt_ref[...], xn, (((1,), (1,)), ((), ())), preferred_element_type=F32)
    gn = gn_ref[...]
    kr = kr_ref[...]
    ssr = jnp.sum(kr * kr, axis=-1, keepdims=True)
    krg = kr * gn[:, LANES:]
    if rope:
        krg = krg * cos_ref[...] + pltpu.roll(krg, 64, 1) * sin_ref[...]
    ones_row = (lax.broadcasted_iota(jnp.int32, (V_ROWS - LANES, kr.shape[0]), 0) == 0).astype(BF16)
    for h in range(heads):
        kn = kk[:, h * LANES:(h + 1) * LANES]
        inv = lax.rsqrt((jnp.sum(kn * kn, axis=-1, keepdims=True) + ssr) / qk_dim + NORM_EPS)
        k_ref[0, h, :, 0:LANES] = (kn * inv * gn[:, :LANES]).astype(BF16)
        k_ref[0, h, :, LANES:] = (krg * inv).astype(BF16)
        v_ref[0, h, 0:LANES, :] = vt[h * LANES:(h + 1) * LANES].astype(BF16)
        v_ref[0, h, LANES:, :] = ones_row


def _mla_keys_values(proj, kv_block, kr_block, g_kva, w_k, w_vt, gn, rope, heads, qk_dim):
    bsz, seq, _ = proj.shape
    r = g_kva.shape[1]
    tm = _pick(seq, 512, LANES)
    in_specs = [pl.BlockSpec((None, tm, r), lambda b, i: (b, i, kv_block)),
                pl.BlockSpec((None, tm, LANES), lambda b, i: (b, i, kr_block)),
                pl.BlockSpec((1, r), lambda b, i: (0, 0)),
                pl.BlockSpec((r, heads * LANES), lambda b, i: (0, 0)),
                pl.BlockSpec((heads * LANES, r), lambda b, i: (0, 0)),
                pl.BlockSpec((1, HEAD_LANES), lambda b, i: (0, 0))]
    args = [proj, proj, g_kva, w_k, w_vt, gn]
    if rope is not None:
        in_specs += [pl.BlockSpec((tm, LANES), lambda b, i: (i, 0))] * 2
        args += list(rope)
    return pl.pallas_call(
        functools.partial(_kv_kernel, heads=heads, qk_dim=qk_dim, rope=rope is not None),
        grid=(bsz, seq // tm),
        in_specs=in_specs,
        out_specs=[pl.BlockSpec((1, heads, tm, HEAD_LANES), lambda b, i: (b, 0, i, 0)),
                   pl.BlockSpec((1, heads, V_ROWS, tm), lambda b, i: (b, 0, 0, i))],
        out_shape=[jax.ShapeDtypeStruct((bsz, heads, seq, HEAD_LANES), BF16),
                   jax.ShapeDtypeStruct((bsz, heads, V_ROWS, seq), BF16)],
        compiler_params=_cparams("parallel", "parallel"),
        name="mla_keys_values",
    )(*args)


ATTN_HEADS_PER_STEP = 2


def _attn_kernel(q_ref, k_ref, v_ref, kc_ref, vc_ref, o_ref, m_ref, acc_ref, *, hps, kchunk):
    kj = pl.program_id(3)

    def scores(g, k):
        return lax.dot_general(k, q_ref[0, g], (((1,), (1,)), ((), ())), preferred_element_type=F32)

    def update(g, st, vt):
        m_old = m_ref[g]
        m_new = jnp.maximum(m_old, jnp.max(st, axis=0, keepdims=True))
        p = jnp.exp2(st - m_new)
        acc_ref[g] = jnp.exp2(m_old - m_new) * acc_ref[g] + _dot(vt, p.astype(BF16))
        m_ref[g] = m_new

    def sweep(units):
        st_next = scores(units[0][0], units[0][1]())
        for i, (g, _, vt) in enumerate(units):
            st = st_next
            if i + 1 < len(units):
                st_next = scores(units[i + 1][0], units[i + 1][1]())
            update(g, st, vt())

    @pl.when(kj == 0)
    def _():
        m_ref[...] = jnp.full_like(m_ref, -jnp.inf)
        acc_ref[...] = jnp.zeros_like(acc_ref)
        sweep([(g, functools.partial(lambda g: kc_ref[0, g], g), functools.partial(lambda g: vc_ref[0, g], g))
               for g in range(hps)])

    tk = k_ref.shape[2]
    sweep([(g,
            functools.partial(lambda g, c: k_ref[0, g, c * kchunk:(c + 1) * kchunk, :], g, c),
            functools.partial(lambda g, c: v_ref[0, g, :, c * kchunk:(c + 1) * kchunk], g, c))
           for c in range(tk // kchunk) for g in range(hps)])

    @pl.when(kj == pl.num_programs(3) - 1)
    def _():
        for g in range(hps):
            acc = acc_ref[g]
            out_t = acc[:LANES] / acc[LANES:LANES + 1]
            o_ref[0, :, g * LANES:(g + 1) * LANES] = out_t.T.astype(o_ref.dtype)


def _attention(q, k, vt, kc, vct, tq_pref, tk_pref, kchunk_pref):
    bsz, heads, seq, _ = q.shape
    lc = kc.shape[2]
    hps = ATTN_HEADS_PER_STEP if heads % ATTN_HEADS_PER_STEP == 0 else 1
    tq = _pick(seq, tq_pref, LANES)
    tk = _pick(seq, tk_pref, LANES)
    kchunk = _pick(tk, kchunk_pref, LANES)
    return pl.pallas_call(
        functools.partial(_attn_kernel, hps=hps, kchunk=kchunk),
        grid=(bsz, heads // hps, seq // tq, seq // tk),
        in_specs=[pl.BlockSpec((1, hps, tq, HEAD_LANES), lambda b, h, i, j: (b, h, i, 0)),
                  pl.BlockSpec((1, hps, tk, HEAD_LANES), lambda b, h, i, j: (b, h, j, 0)),
                  pl.BlockSpec((1, hps, V_ROWS, tk), lambda b, h, i, j: (b, h, 0, j)),
                  pl.BlockSpec((1, hps, lc, HEAD_LANES), lambda b, h, i, j: (b, h, 0, 0)),
                  pl.BlockSpec((1, hps, V_ROWS, lc), lambda b, h, i, j: (b, h, 0, 0))],
        out_specs=pl.BlockSpec((1, tq, hps * LANES), lambda b, h, i, j: (b, i, h)),
        out_shape=jax.ShapeDtypeStruct((bsz, seq, heads * LANES), BF16),
        scratch_shapes=[pltpu.VMEM((hps, 1, tq), F32), pltpu.VMEM((hps, V_ROWS, tq), F32)],
        compiler_params=_cparams("parallel", "parallel", "parallel", "arbitrary"),
        name="attention",
    )(q, k, vt, kc, vct)


def kernel(x, c, ctx, c_ctx, norm1_g, norm2_g, w_ada, b_ada, w_in, hy_conv_w, hy_conv_b, hy_filt_w1, hy_filt_b1, hy_filt_w2, hy_filt_b2, hy_filt_w3, hy_freq, hy_bias, mla_g_qa, mla_w_qb, mla_g_kva, mla_w_kvb, mla_q_norm_g, mla_k_norm_g, w_out, w_mlp1, w_mlp2):
    assert w_ada.shape[0] == 1, "single-layer block"
    bsz, seq, d = x.shape
    lc = ctx.shape[1]
    hyc = hy_conv_b.shape[1]
    width = hy_bias.shape[2]
    q_lora = mla_g_qa.shape[1]
    kv_lora = mla_g_kva.shape[1]
    qk_dim = mla_q_norm_g.shape[1]
    nope = qk_dim - MLA_ROPE
    heads = mla_w_qb.shape[2] // qk_dim
    v_dim = mla_w_kvb.shape[2] // heads - nope
    assert nope == LANES and v_dim == LANES and seq % GRID_W == 0
    q0, kv0, kr0 = hyc, hyc + q_lora, hyc + q_lora + kv_lora
    assert q0 % q_lora == 0 and kv0 % kv_lora == 0 and kr0 % LANES == 0

    cc = jnp.zeros((8, d), F32).at[:bsz].set(c).at[bsz].set(c_ctx)
    mod = _adaln(cc, w_ada[0], b_ada)
    chunk = lambda i: mod[:bsz, i * d:(i + 1) * d][:, None, :]
    sh1, sc1, g1, sh2, sc2, g2 = [chunk(i) for i in range(6)]
    csh1 = mod[bsz:bsz + 1, 0:d][:, None, :]
    csc1 = mod[bsz:bsz + 1, d:2 * d][:, None, :]

    rope_src, _, _ = _rope_lane_map()
    w_in0 = w_in[0]
    np_cols = kr0 + LANES
    np_cols += (-np_cols) % 1024
    w_kr = _gather_cols(w_in0[:, kr0:], rope_src)
    w_pad = jnp.concatenate([w_in0[:, :kr0], w_kr, jnp.zeros((d, np_cols - kr0 - LANES), F32)], axis=1).astype(BF16)
    w_q = _gather_cols(mla_w_qb[0], _head_col_index(heads, nope, qk_dim, nope)).astype(BF16)
    kv_idx = np.concatenate([h * (nope + v_dim) + np.arange(nope) for h in range(heads)]
                            + [h * (nope + v_dim) + nope + np.arange(v_dim) for h in range(heads)])
    w_kv = _gather_cols(mla_w_kvb[0], kv_idx).astype(BF16)
    w_k, w_vt = w_kv[:, :heads * nope], w_kv[:, heads * nope:].T
    head_idx = _head_col_index(1, nope, qk_dim, nope)
    gq = _gather_cols(mla_q_norm_g[0][None, :], head_idx)
    gk = _gather_cols(mla_k_norm_g[0][None, :], head_idx)
    cos, sin = _rope_tables(seq)

    ctx_cols = kv0 - kv0 % 1024
    proj_c = _norm_mod_matmul(ctx.reshape(bsz * lc, d), norm1_g, csh1, csc1, w_pad[:, ctx_cols:],
                              bsz * lc, F32, False, 512, 1024).reshape(bsz, lc, np_cols - ctx_cols)
    k_c, v_c = _mla_keys_values(proj_c, (kv0 - ctx_cols) // kv_lora, (kr0 - ctx_cols) // LANES,
                                mla_g_kva, w_k, w_vt, gk, None, heads, qk_dim)

    x2 = x.reshape(bsz * seq, d)
    proj = _norm_mod_matmul(x2, norm1_g, sh1, sc1, w_pad, seq, F32, False, 512, 1024).reshape(bsz, seq, np_cols)
    y_hy = _hyena(proj, hy_conv_w[0], hy_conv_b[0], hy_filt_w1[0], hy_filt_b1[0], hy_filt_w2[0],
                  hy_filt_b2[0], hy_filt_w3[0], hy_freq[0], hy_bias[0])
    q = _mla_queries(proj, q0 // q_lora, mla_g_qa, w_q, gq, cos, sin, heads, qk_dim, qk_dim ** -0.5 * math.log2(math.e))
    k, v = _mla_keys_values(proj, kv0 // kv_lora, kr0 // LANES, mla_g_kva, w_k, w_vt, gk, (cos, sin), heads, qk_dim)
    y_att = _attention(q, k, v, k_c, v_c, 512, 1024, 512)

    x1 = _out_proj(y_hy.reshape(bsz * seq, width), y_att.reshape(bsz * seq, heads * v_dim),
                   w_out[0].astype(BF16), x2, g1, seq)
    hmid = _norm_mod_matmul(x1, norm2_g, sh2, sc2, w_mlp1[0].astype(BF16), seq, BF16, True, 512, 1024)
    out = _matmul_gated_residual(hmid, w_mlp2[0].astype(BF16), x1, g2, seq)
    return out.reshape(bsz, seq, d)
```

```python
import functools
import math

import numpy as np
import jax
import jax.numpy as jnp
from jax import lax
from jax.experimental import pallas as pl
from jax.experimental.pallas import tpu as pltpu

F32 = jnp.float32
BF16 = jnp.bfloat16

NORM_EPS = 1e-6
GRID_W = 64
MLA_ROPE = 64
ROPE_THETA = 10000.0
HY_POS_BANDS = 16
HY_DECAY_TARGET = 1e-2
HY_FAST_DECAY_PCT = 0.3
HY_SLOW_DECAY_PCT = 1.5

LANES = 128
SUBLANES = 8
VMEM_LIMIT = 56 * 1024 * 1024


def _cparams(*sem):
    return pltpu.CompilerParams(dimension_semantics=sem, vmem_limit_bytes=VMEM_LIMIT)


def _pick(dim, pref, align):
    t = min(pref, dim)
    t -= t % align
    while t >= align:
        if dim % t == 0:
            return t
        t -= align
    return dim


def _dot(a, b):
    return jnp.dot(a, b, preferred_element_type=F32)


def _adaln_kernel(c_ref, w_ref, b_ref, o_ref):
    c = c_ref[...]
    s = c * jax.nn.sigmoid(c)
    o_ref[...] = _dot(s.astype(BF16), w_ref[...].astype(BF16)) + b_ref[...]


def _adaln(cc, w, b):
    d, n = w.shape
    tn = _pick(n, 512, LANES)
    return pl.pallas_call(
        _adaln_kernel,
        grid=(n // tn,),
        in_specs=[pl.BlockSpec((8, d), lambda j: (0, 0)),
                  pl.BlockSpec((d, tn), lambda j: (0, j)),
                  pl.BlockSpec((1, tn), lambda j: (0, j))],
        out_specs=pl.BlockSpec((8, tn), lambda j: (0, j)),
        out_shape=jax.ShapeDtypeStruct((8, n), F32),
        compiler_params=_cparams("parallel"),
        name="adaln",
    )(cc, w, b)


def _nmm_kernel(x_ref, g_ref, sh_ref, sc_ref, w_ref, o_ref, h_ref, *, sq_relu):
    @pl.when(pl.program_id(1) == 0)
    def _():
        x = x_ref[...]
        ms = jnp.mean(x * x, axis=-1, keepdims=True)
        y = x * lax.rsqrt(ms + NORM_EPS) * g_ref[...]
        h_ref[...] = (y * (1.0 + sc_ref[0]) + sh_ref[0]).astype(BF16)

    acc = _dot(h_ref[...], w_ref[...])
    if sq_relu:
        acc = jnp.square(jnp.maximum(acc, 0.0))
    o_ref[...] = acc.astype(o_ref.dtype)


def _norm_mod_matmul(x, g, shift, scale, w, rows_per_mod, out_dtype, sq_relu, tm_pref, tn_pref):
    m, k = x.shape
    n = w.shape[1]
    tm = _pick(math.gcd(m, rows_per_mod), tm_pref, 16)
    tn = _pick(n, tn_pref, LANES)
    mod_idx = lambda i, j: ((i * tm) // rows_per_mod, 0, 0)
    return pl.pallas_call(
        functools.partial(_nmm_kernel, sq_relu=sq_relu),
        grid=(m // tm, n // tn),
        in_specs=[pl.BlockSpec((tm, k), lambda i, j: (i, 0)),
                  pl.BlockSpec((1, k), lambda i, j: (0, 0)),
                  pl.BlockSpec((1, 1, k), mod_idx),
                  pl.BlockSpec((1, 1, k), mod_idx),
                  pl.BlockSpec((k, tn), lambda i, j: (0, j))],
        out_specs=pl.BlockSpec((tm, tn), lambda i, j: (i, j)),
        out_shape=jax.ShapeDtypeStruct((m, n), out_dtype),
        scratch_shapes=[pltpu.VMEM((tm, k), BF16)],
        compiler_params=_cparams("parallel", "arbitrary"),
        name="norm_mod_matmul",
    )(x, g, shift, scale, w)


def _outproj_kernel(a0_ref, a1_ref, w_ref, res_ref, gate_ref, o_ref, *, k0):
    acc = _dot(a0_ref[...], w_ref[0:k0, :]) + _dot(a1_ref[...], w_ref[k0:, :])
    o_ref[...] = res_ref[...] + gate_ref[0] * acc


def _out_proj(a0, a1, w, res, gate, rows_per_mod):
    m, k0 = a0.shape
    k1 = a1.shape[1]
    n = w.shape[1]
    tm = _pick(math.gcd(m, rows_per_mod), 1024, 16)
    tn = _pick(n, 1024, LANES)
    return pl.pallas_call(
        functools.partial(_outproj_kernel, k0=k0),
        grid=(m // tm, n // tn),
        in_specs=[pl.BlockSpec((tm, k0), lambda i, j: (i, 0)),
                  pl.BlockSpec((tm, k1), lambda i, j: (i, 0)),
                  pl.BlockSpec((k0 + k1, tn), lambda i, j: (0, j)),
                  pl.BlockSpec((tm, tn), lambda i, j: (i, j)),
                  pl.BlockSpec((1, 1, tn), lambda i, j: ((i * tm) // rows_per_mod, 0, j))],
        out_specs=pl.BlockSpec((tm, tn), lambda i, j: (i, j)),
        out_shape=jax.ShapeDtypeStruct((m, n), F32),
        compiler_params=_cparams("parallel", "parallel"),
        name="out_proj",
    )(a0, a1, w, res, gate)


def _mmres_kernel(a_ref, w_ref, res_ref, gate_ref, o_ref, acc_ref):
    kk = pl.program_id(2)

    @pl.when(kk == 0)
    def _():
        acc_ref[...] = jnp.zeros_like(acc_ref)

    acc_ref[...] += _dot(a_ref[...], w_ref[...])

    @pl.when(kk == pl.num_programs(2) - 1)
    def _():
        o_ref[...] = res_ref[...] + gate_ref[0] * acc_ref[...]


def _matmul_gated_residual(a, w, res, gate, rows_per_mod):
    m, k = a.shape
    n = w.shape[1]
    tm = _pick(math.gcd(m, rows_per_mod), 1024, 16)
    tn = _pick(n, 1024, LANES)
    tk = _pick(k, 2048, LANES)
    return pl.pallas_call(
        _mmres_kernel,
        grid=(m // tm, n // tn, k // tk),
        in_specs=[pl.BlockSpec((tm, tk), lambda i, j, kk: (i, kk)),
                  pl.BlockSpec((tk, tn), lambda i, j, kk: (kk, j)),
                  pl.BlockSpec((tm, tn), lambda i, j, kk: (i, j)),
                  pl.BlockSpec((1, 1, tn), lambda i, j, kk: ((i * tm) // rows_per_mod, 0, j))],
        out_specs=pl.BlockSpec((tm, tn), lambda i, j, kk: (i, j)),
        out_shape=jax.ShapeDtypeStruct((m, n), F32),
        scratch_shapes=[pltpu.VMEM((tm, tn), F32)],
        compiler_params=_cparams("parallel", "parallel", "arbitrary"),
        name="matmul_gated_residual",
    )(a, w, res, gate)


def _sconv_kernel(p_ref, pp_ref, pn_ref, w_ref, b_ref, o_ref):
    i = pl.program_id(1)
    x = p_ref[0]
    tl = x.shape[0]
    rows = lax.broadcasted_iota(jnp.int32, x.shape, 0)
    prev_row = jnp.where(i > 0, pp_ref[0, SUBLANES - 1:SUBLANES, :], 0.0)
    next_row = jnp.where(i < pl.num_programs(1) - 1, pn_ref[0, 0:1, :], 0.0)
    up = jnp.where(rows == 0, prev_row, pltpu.roll(x, 1, 0))
    dn = jnp.where(rows == tl - 1, next_row, pltpu.roll(x, tl - 1, 0))
    o_ref[0] = up * w_ref[0:1, :] + x * w_ref[1:2, :] + dn * w_ref[2:3, :] + b_ref[...]


def _short_conv(proj, w, b):
    bsz, seq, _ = proj.shape
    c = w.shape[1]
    tl = _pick(seq, 512, SUBLANES)
    tc = _pick(c, 512, LANES)
    r = tl // SUBLANES
    last = seq // SUBLANES - 1
    return pl.pallas_call(
        _sconv_kernel,
        grid=(bsz, seq // tl, c // tc),
        in_specs=[pl.BlockSpec((1, tl, tc), lambda bb, i, j: (bb, i, j)),
                  pl.BlockSpec((1, SUBLANES, tc), lambda bb, i, j: (bb, jnp.maximum(i * r - 1, 0), j)),
                  pl.BlockSpec((1, SUBLANES, tc), lambda bb, i, j: (bb, jnp.minimum((i + 1) * r, last), j)),
                  pl.BlockSpec((3, tc), lambda bb, i, j: (0, j)),
                  pl.BlockSpec((1, tc), lambda bb, i, j: (0, j))],
        out_specs=pl.BlockSpec((1, tl, tc), lambda bb, i, j: (bb, i, j)),
        out_shape=jax.ShapeDtypeStruct((bsz, seq, c), F32),
        compiler_params=_cparams("parallel", "parallel", "parallel"),
        name="short_conv",
    )(proj, proj, proj, w, b)


def _filt_kernel(feat_ref, t_ref, w1_ref, b1_ref, w2_ref, b2_ref, fr_ref, w3_ref, dl_ref, o_ref, h_ref, *, seq):
    i = pl.program_id(0)
    j = pl.program_id(1)

    @pl.when(j == 0)
    def _():
        fr = fr_ref[...]
        h = jnp.sin(fr * (_dot(feat_ref[...].astype(BF16), w1_ref[...]) + b1_ref[...]))
        h = jnp.sin(fr * (_dot(h.astype(BF16), w2_ref[...]) + b2_ref[...]))
        h_ref[...] = h.astype(BF16)

    f = _dot(h_ref[...], w3_ref[...]) * jnp.exp(-t_ref[...] * dl_ref[...])
    rows = lax.broadcasted_iota(jnp.int32, f.shape, 0) + i * f.shape[0]
    o_ref[0] = jnp.where(rows == seq, 0.0, f)


def _hyena_filters(seq, w1, b1, w2, b2, w3, freq, width):
    lag = jnp.arange(2 * seq)
    src = jnp.where(lag < seq, lag, jnp.clip(2 * seq - lag, 0, seq - 1))
    pos = jnp.arange(seq, dtype=F32)
    t = jnp.linspace(0.0, 1.0, seq, dtype=F32)[:, None]
    bands = jnp.linspace(1e-4, HY_POS_BANDS - 1, HY_POS_BANDS, dtype=F32)
    ang = (2.0 * math.pi / seq) * pos[:, None] * bands[None, :]
    feats = jnp.concatenate([t, jnp.cos(ang), -jnp.sin(ang)], axis=-1)
    pd = feats.shape[1]
    hid = w1.shape[1]
    hp = LANES
    feats = jnp.pad(feats, ((0, 0), (0, hp - pd)))[src]
    t = t[src]
    w1p = jnp.pad(w1, ((0, hp - pd), (0, hp - hid))).astype(BF16)
    w2p = jnp.pad(w2, ((0, hp - hid), (0, hp - hid))).astype(BF16)
    w3p = jnp.pad(w3, ((0, hp - hid), (0, 0))).astype(BF16)
    b1p = jnp.pad(b1, (0, hp - hid))[None, :]
    b2p = jnp.pad(b2, (0, hp - hid))[None, :]
    frp = jnp.pad(freq, (0, hp - hid))[None, :]
    deltas = jnp.abs(jnp.linspace(math.log(HY_DECAY_TARGET) / HY_SLOW_DECAY_PCT,
                                  math.log(HY_DECAY_TARGET) / HY_FAST_DECAY_PCT, width, dtype=F32))
    ncol = w3.shape[1] // 2
    dl = jnp.tile(deltas, ncol // width)[None, :]
    tl = _pick(seq, 512, SUBLANES)
    tn = _pick(ncol, 2048, LANES)
    fwd_tiles = seq // tl
    return pl.pallas_call(
        functools.partial(_filt_kernel, seq=seq),
        grid=(2 * seq // tl, ncol // tn),
        in_specs=[pl.BlockSpec((tl, hp), lambda i, j: (i, 0)),
                  pl.BlockSpec((tl, 1), lambda i, j: (i, 0)),
                  pl.BlockSpec((hp, hp), lambda i, j: (0, 0)),
                  pl.BlockSpec((1, hp), lambda i, j: (0, 0)),
                  pl.BlockSpec((hp, hp), lambda i, j: (0, 0)),
                  pl.BlockSpec((1, hp), lambda i, j: (0, 0)),
                  pl.BlockSpec((1, hp), lambda i, j: (0, 0)),
                  pl.BlockSpec((hp, tn), lambda i, j: (0, (i // fwd_tiles) * (ncol // tn) + j)),
                  pl.BlockSpec((1, tn), lambda i, j: (0, j))],
        out_specs=pl.BlockSpec((1, tl, tn), lambda i, j: (0, i, j)),
        out_shape=jax.ShapeDtypeStruct((1, 2 * seq, ncol), F32),
        scratch_shapes=[pltpu.VMEM((tl, hp), BF16)],
        compiler_params=_cparams("parallel", "arbitrary"),
        name="hyena_filters",
    )(feats, t, w1p, b1p, w2p, b2p, frp, w3p, dl)


K1_GROUP = SUBLANES


def _dft_tables(n1, n2):
    n = n1 * n2
    hl = n1 // 2
    k1 = np.arange(n1)[None, :, None]
    l1 = np.arange(hl)[None, None, :]
    l2 = np.arange(n2)[:, None, None]
    ang = -2.0 * np.pi * ((k1 * (n2 * l1 + l2)) % n) / n
    gr, gi = np.cos(ang), np.sin(ang)
    gm = np.concatenate([np.concatenate([gr, -gi], axis=2), np.concatenate([gi, gr], axis=2)], axis=1)
    a2 = -2.0 * np.pi * ((np.arange(n2)[:, None] * np.arange(n2)[None, :]) % n2) / n2
    fr, fi = np.cos(a2), np.sin(a2)
    f2 = np.block([[fr, -fi], [fi, fr]])
    if2 = np.block([[fr, fi], [-fi, fr]])
    ir, ii = np.transpose(gr, (0, 2, 1)) / n, -np.transpose(gi, (0, 2, 1)) / n
    igm = np.concatenate([np.concatenate([ir, -ii], axis=2), np.concatenate([ii, ir], axis=2)], axis=1)
    angf = -2.0 * np.pi * ((k1 * (n2 * np.arange(n1)[None, None, :] + l2)) % n) / n
    gmr = np.concatenate([np.cos(angf), np.sin(angf)], axis=1)
    to = lambda a: jnp.asarray(a.astype(np.float32)).astype(BF16)
    return to(gm), to(f2), to(if2), to(igm), to(gmr)


def _fft_split(seq):
    n = 2 * seq
    n2 = LANES if n % (LANES * 2 * SUBLANES) == 0 and n // LANES >= 2 * SUBLANES else 2 * SUBLANES
    n1 = n // n2
    assert n1 * n2 == n and n1 % (2 * SUBLANES) == 0 and n2 % SUBLANES == 0, (n1, n2)
    return n1, n2


def _f1_kernel(sig_ref, gm_ref, o_ref, *, nb, n1, n2, l2c):
    j = pl.program_id(1)
    rows = n1 // nb

    def body(t, carry):
        l2 = j * l2c + t
        parts = [sig_ref[b, pl.ds(l2, rows, stride=n2), :] for b in range(nb)]
        rhs = jnp.concatenate(parts, axis=0).astype(BF16)
        res = _dot(gm_ref[t], rhs)
        row0 = pl.multiple_of(t * K1_GROUP, K1_GROUP)
        for g in range(n1 // K1_GROUP):
            for ri in range(2):
                lo = ri * n1 + g * K1_GROUP
                o_ref[g, ri, pl.ds(row0, K1_GROUP), :] = res[lo:lo + K1_GROUP]
        return carry

    lax.fori_loop(0, l2c, body, 0, unroll=4)


def _fft_stage1(sig, gm, n1, n2, c):
    nb, seq, _ = sig.shape
    assert nb * seq == n1 * n2
    l2c = _pick(n2, 16, 4)
    return pl.pallas_call(
        functools.partial(_f1_kernel, nb=nb, n1=n1, n2=n2, l2c=l2c),
        grid=(c // LANES, n2 // l2c),
        in_specs=[pl.BlockSpec((nb, seq, LANES), lambda cb, j: (0, 0, cb)),
                  pl.BlockSpec((l2c, 2 * n1, n1), lambda cb, j: (j, 0, 0))],
        out_specs=pl.BlockSpec((n1 // K1_GROUP, 2, l2c * K1_GROUP, LANES), lambda cb, j: (0, 0, j, cb)),
        out_shape=jax.ShapeDtypeStruct((n1 // K1_GROUP, 2, n2 * K1_GROUP, c), F32),
        compiler_params=_cparams("parallel", "arbitrary"),
        name="fft_stage1",
    )(sig, gm)


def _stage2_rhs(y_ref, t, n2):
    part = lambda ri, tt: y_ref.at[ri][pl.ds(tt, n2, stride=K1_GROUP), :]
    re = jnp.concatenate([part(0, t), part(0, t + 1)], axis=1)
    im = jnp.concatenate([part(1, t), part(1, t + 1)], axis=1)
    return jnp.concatenate([re, im], axis=0).astype(BF16)


def _filter_spectrum_kernel(y_ref, f2_ref, o_ref, *, n2):
    for t in range(0, K1_GROUP, 2):
        x = _dot(f2_ref[...], _stage2_rhs(y_ref, t, n2))
        for d in range(2):
            o_ref[t + d, 0] = x[:n2, d * LANES:(d + 1) * LANES]
            o_ref[t + d, 1] = x[n2:, d * LANES:(d + 1) * LANES]


def _filter_spectrum(yfilt, f2, n1, n2):
    c = yfilt.shape[-1]
    return pl.pallas_call(
        functools.partial(_filter_spectrum_kernel, n2=n2),
        grid=(c // LANES, n1 // K1_GROUP),
        in_specs=[pl.BlockSpec((None, 2, n2 * K1_GROUP, LANES), lambda cb, g: (g, 0, 0, cb)),
                  pl.BlockSpec((2 * n2, 2 * n2), lambda cb, g: (0, 0))],
        out_specs=pl.BlockSpec((K1_GROUP, 2, n2, LANES), lambda cb, g: (g, 0, 0, cb)),
        out_shape=jax.ShapeDtypeStruct((n1, 2, n2, c), F32),
        compiler_params=_cparams("parallel", "parallel"),
        name="filter_spectrum",
    )(yfilt, f2)


def _fftconv_kernel(sig_ref, gm_ref, k_ref, f2_ref, if2_ref, igm_ref, gate_ref, bias_ref, o_ref, yv_ref, c_ref,
                    *, n1, n2, l2c, g2, l3, rows4, s1, s2, s3):
    s = pl.program_id(1)
    hl = n1 // 2
    ng = n1 // K1_GROUP

    @pl.when(s < s1)
    def _():
        def body(t, carry):
            l2 = s * l2c + t
            rhs = jnp.concatenate([sig_ref[b, pl.ds(l2, hl, stride=n2), :] for b in range(2)], axis=0)
            res = _dot(gm_ref[t], rhs.astype(BF16))
            row0 = pl.multiple_of(l2 * K1_GROUP, K1_GROUP)
            for g in range(ng):
                for ri in range(2):
                    lo = ri * n1 + g * K1_GROUP
                    yv_ref[g * 2 + ri, pl.ds(row0, K1_GROUP), :] = res[lo:lo + K1_GROUP]
            return carry

        lax.fori_loop(0, l2c, body, 0, unroll=4)

    @pl.when((s >= s1) & (s < s1 + s2))
    def _():
        for gg in range(g2):
            g = (s - s1) * g2 + gg
            rows = lambda tt: pl.ds(tt, n2, stride=K1_GROUP)
            for t in range(0, K1_GROUP, 2):
                re = jnp.concatenate([yv_ref[g * 2, rows(t), :], yv_ref[g * 2, rows(t + 1), :]], axis=1)
                im = jnp.concatenate([yv_ref[g * 2 + 1, rows(t), :], yv_ref[g * 2 + 1, rows(t + 1), :]], axis=1)
                x = _dot(f2_ref[...], jnp.concatenate([re, im], axis=0).astype(BF16))
                xr, xi = x[:n2], x[n2:]
                kt = gg * K1_GROUP + t
                kr = jnp.concatenate([k_ref[kt, 0], k_ref[kt + 1, 0]], axis=1)
                ki = jnp.concatenate([k_ref[kt, 1], k_ref[kt + 1, 1]], axis=1)
                p = jnp.concatenate([xr * kr - xi * ki, xr * ki + xi * kr], axis=0).astype(BF16)
                yp = _dot(if2_ref[...], p)
                for d in range(2):
                    for ri in range(2):
                        yv_ref[g * 2 + ri, rows(t + d), :] = yp[ri * n2:(ri + 1) * n2, d * LANES:(d + 1) * LANES]

    @pl.when((s >= s1 + s2) & (s < s1 + s2 + s3))
    def _():
        def body(i, carry):
            l2 = (s - s1 - s2) * (l3 * SUBLANES) + i
            row0 = pl.multiple_of(l2 * K1_GROUP, K1_GROUP)
            tiles = [yv_ref[g * 2 + ri, pl.ds(row0, K1_GROUP), :] for ri in range(2) for g in range(ng)]
            res = _dot(igm_ref[i], jnp.concatenate(tiles, axis=0).astype(BF16))
            for b in range(2):
                c_ref[b, pl.ds(l2, hl, stride=n2), :] = res[b * hl:(b + 1) * hl]
            return carry

        lax.fori_loop(0, l3 * SUBLANES, body, 0, unroll=2)

    @pl.when(s >= s1 + s2 + s3)
    def _():
        rows = pl.ds(pl.multiple_of((s - s1 - s2 - s3) * rows4, rows4), rows4)
        conv = c_ref[:, rows, :] + bias_ref[...] * sig_ref[:, rows, :]
        o_ref[...] = (gate_ref[...] * conv).astype(o_ref.dtype)


def _fft_conv_gated(sig, kspec, kcol_block0, gate, gate_col_block0, bias, tables, n1, n2, out_dtype):
    gm, f2, if2, igm = tables
    c = bias.shape[1]
    seq = sig.shape[1]
    ng = n1 // K1_GROUP
    l2c = _pick(n2, 32, 4)
    g2 = _pick(ng, 2, 1)
    l3 = _pick(n2 // SUBLANES, 2, 1)
    rows4 = _pick(seq, 2048, 16)
    s1, s2, s3, s4 = n2 // l2c, ng // g2, n2 // (l3 * SUBLANES), seq // rows4
    clamp = lambda v, n: jnp.clip(v, 0, n - 1)
    return pl.pallas_call(
        functools.partial(_fftconv_kernel, n1=n1, n2=n2, l2c=l2c, g2=g2, l3=l3, rows4=rows4, s1=s1, s2=s2, s3=s3),
        grid=(c // LANES, s1 + s2 + s3 + s4),
        in_specs=[pl.BlockSpec((2, seq, LANES), lambda cb, s: (0, 0, cb), pipeline_mode=pl.Buffered(1)),
                  pl.BlockSpec((l2c, 2 * n1, n1), lambda cb, s: (clamp(s, s1), 0, 0)),
                  pl.BlockSpec((g2 * K1_GROUP, 2, n2, LANES),
                               lambda cb, s: (clamp(s - s1, s2), 0, 0, kcol_block0 + cb)),
                  pl.BlockSpec((2 * n2, 2 * n2), lambda cb, s: (0, 0)),
                  pl.BlockSpec((2 * n2, 2 * n2), lambda cb, s: (0, 0)),
                  pl.BlockSpec((l3 * SUBLANES, n1, 2 * n1), lambda cb, s: (clamp(s - s1 - s2, s3), 0, 0)),
                  pl.BlockSpec((2, rows4, LANES),
                               lambda cb, s: (0, clamp(s - s1 - s2 - s3, s4), gate_col_block0 + cb)),
                  pl.BlockSpec((1, LANES), lambda cb, s: (0, cb))],
        out_specs=pl.BlockSpec((2, rows4, LANES), lambda cb, s: (0, clamp(s - s1 - s2 - s3, s4), cb)),
        out_shape=jax.ShapeDtypeStruct((2, seq, c), out_dtype),
        scratch_shapes=[pltpu.VMEM((2 * ng, n2 * K1_GROUP, LANES), F32), pltpu.VMEM((2, seq, LANES), F32)],
        compiler_params=_cparams("parallel", "arbitrary"),
        name="fft_conv_gated",
    )(sig, gm, kspec, f2, if2, igm, gate, bias)


def _hyena(proj, conv_w, conv_b, fw1, fb1, fw2, fb2, fw3, freq, hy_bias):
    bsz, seq, _ = proj.shape
    assert bsz == 2, "the FFT convolution packs exactly two batch elements into one complex signal"
    width = hy_bias.shape[1]
    n1, n2 = _fft_split(seq)
    gm, f2, if2, igm, gmr = _dft_tables(n1, n2)

    filt = _hyena_filters(seq, fw1, fb1, fw2, fb2, fw3, freq, width)
    kspec = _filter_spectrum(_fft_stage1(filt, gmr, n1, n2, 2 * width), f2, n1, n2)

    u = _short_conv(proj, conv_w, conv_b[None, :])
    tables = (gm, f2, if2, igm)
    wb = width // LANES
    z = _fft_conv_gated(u, kspec, 0, u, wb, hy_bias[0:1], tables, n1, n2, F32)
    return _fft_conv_gated(z, kspec, wb, u, 2 * wb, hy_bias[1:2], tables, n1, n2, BF16)


HEAD_LANES = 2 * LANES


def _rope_lane_map():
    lane = np.arange(LANES)
    half = lane // 64
    rem = lane % 64
    valid = rem < 32
    axis = rem // 16
    f = rem % 16
    src = np.where(valid, axis * 32 + half * 16 + f, -1)
    tab = np.where(valid, axis * 16 + f, -1)
    return src, tab, half


def _gather_cols(w, idx):
    wp = jnp.concatenate([w, jnp.zeros(w.shape[:-1] + (1,), w.dtype)], axis=-1)
    return jnp.take(wp, jnp.asarray(np.where(idx < 0, w.shape[-1], idx)), axis=-1)


def _head_col_index(heads, nope, per_head, rope_off):
    src, _, _ = _rope_lane_map()
    idx = []
    for h in range(heads):
        base = h * per_head
        idx.append(base + np.arange(nope))
        idx.append(np.where(src < 0, -1, base + rope_off + src))
    return np.concatenate(idx)


def _rope_tables(seq):
    rows = seq // GRID_W
    row = jnp.repeat(jnp.arange(rows, dtype=F32), GRID_W)
    col = jnp.tile(jnp.arange(GRID_W, dtype=F32), rows)
    half = MLA_ROPE // 2
    inv = ROPE_THETA ** (-jnp.arange(0, half, 2, dtype=F32) / half)
    ang = jnp.concatenate([row[:, None] * inv, col[:, None] * inv], axis=-1)
    _, tab, hf = _rope_lane_map()
    cos = _gather_cols(jnp.cos(ang), tab)
    sin = _gather_cols(jnp.sin(ang), tab) * jnp.asarray(np.where(hf == 0, -1.0, 1.0), F32)
    return cos, sin


def _rms(x, g):
    return x * lax.rsqrt(jnp.mean(x * x, axis=-1, keepdims=True) + NORM_EPS) * g


def _q_kernel(qa_ref, g_ref, w_ref, gn_ref, cos_ref, sin_ref, o_ref, *, heads, qk_dim, out_scale):
    xn = _rms(qa_ref[...], g_ref[...]).astype(BF16)
    q = _dot(xn, w_ref[...])
    gn = gn_ref[...]
    cos, sin = cos_ref[...], sin_ref[...]
    for h in range(heads):
        qh = q[:, h * HEAD_LANES:(h + 1) * HEAD_LANES]
        inv = lax.rsqrt(jnp.sum(qh * qh, axis=-1, keepdims=True) / qk_dim + NORM_EPS)
        qn = qh * inv * gn
        r = qn[:, LANES:]
        r = r * cos + pltpu.roll(r, 64, 1) * sin
        o_ref[0, h, :, 0:LANES] = (qn[:, :LANES] * out_scale).astype(BF16)
        o_ref[0, h, :, LANES:] = (r * out_scale).astype(BF16)


def _mla_queries(proj, col_block, g_qa, w_q, gn, cos, sin, heads, qk_dim, out_scale):
    bsz, seq, _ = proj.shape
    r = g_qa.shape[1]
    tm = _pick(seq, 512, 16)
    return pl.pallas_call(
        functools.partial(_q_kernel, heads=heads, qk_dim=qk_dim, out_scale=out_scale),
        grid=(bsz, seq // tm),
        in_specs=[pl.BlockSpec((None, tm, r), lambda b, i: (b, i, col_block)),
                  pl.BlockSpec((1, r), lambda b, i: (0, 0)),
                  pl.BlockSpec((r, heads * HEAD_LANES), lambda b, i: (0, 0)),
                  pl.BlockSpec((1, HEAD_LANES), lambda b, i: (0, 0)),
                  pl.BlockSpec((tm, LANES), lambda b, i: (i, 0)),
                  pl.BlockSpec((tm, LANES), lambda b, i: (i, 0))],
        out_specs=pl.BlockSpec((1, heads, tm, HEAD_LANES), lambda b, i: (b, 0, i, 0)),
        out_shape=jax.ShapeDtypeStruct((bsz, heads, seq, HEAD_LANES), BF16),
        compiler_params=_cparams("parallel", "parallel"),
        name="mla_queries",
    )(proj, g_qa, w_q, gn, cos, sin)


V_ROWS = LANES + 16


def _kv_kernel(kva_ref, kr_ref, g_ref, wk_ref, wvt_ref, gn_ref, *rest, heads, qk_dim, rope):
    if rope:
        cos_ref, sin_ref, k_ref, v_ref = rest
    else:
        k_ref, v_ref = rest
    xn = _rms(kva_ref[...], g_ref[...]).astype(BF16)
    kk = _dot(xn, wk_ref[...])
    vt = lax.dot_general(wvt_ref[...], xn, (((1,), (1,)), ((), ())), preferred_element_type=F32)
    gn = gn_ref[...]
    kr = kr_ref[...]
    ssr = jnp.sum(kr * kr, axis=-1, keepdims=True)
    krg = kr * gn[:, LANES:]
    if rope:
        krg = krg * cos_ref[...] + pltpu.roll(krg, 64, 1) * sin_ref[...]
    ones_row = (lax.broadcasted_iota(jnp.int32, (V_ROWS - LANES, kr.shape[0]), 0) == 0).astype(BF16)
    for h in range(heads):
        kn = kk[:, h * LANES:(h + 1) * LANES]
        inv = lax.rsqrt((jnp.sum(kn * kn, axis=-1, keepdims=True) + ssr) / qk_dim + NORM_EPS)
        k_ref[0, h, :, 0:LANES] = (kn * inv * gn[:, :LANES]).astype(BF16)
        k_ref[0, h, :, LANES:] = (krg * inv).astype(BF16)
        v_ref[0, h, 0:LANES, :] = vt[h * LANES:(h + 1) * LANES].astype(BF16)
        v_ref[0, h, LANES:, :] = ones_row


def _mla_keys_values(proj, kv_block, kr_block, g_kva, w_k, w_vt, gn, rope, heads, qk_dim):
    bsz, seq, _ = proj.shape
    r = g_kva.shape[1]
    tm = _pick(seq, 512, LANES)
    in_specs = [pl.BlockSpec((None, tm, r), lambda b, i: (b, i, kv_block)),
                pl.BlockSpec((None, tm, LANES), lambda b, i: (b, i, kr_block)),
                pl.BlockSpec((1, r), lambda b, i: (0, 0)),
                pl.BlockSpec((r, heads * LANES), lambda b, i: (0, 0)),
                pl.BlockSpec((heads * LANES, r), lambda b, i: (0, 0)),
                pl.BlockSpec((1, HEAD_LANES), lambda b, i: (0, 0))]
    args = [proj, proj, g_kva, w_k, w_vt, gn]
    if rope is not None:
        in_specs += [pl.BlockSpec((tm, LANES), lambda b, i: (i, 0))] * 2
        args += list(rope)
    return pl.pallas_call(
        functools.partial(_kv_kernel, heads=heads, qk_dim=qk_dim, rope=rope is not None),
        grid=(bsz, seq // tm),
        in_specs=in_specs,
        out_specs=[pl.BlockSpec((1, heads, tm, HEAD_LANES), lambda b, i: (b, 0, i, 0)),
                   pl.BlockSpec((1, heads, V_ROWS, tm), lambda b, i: (b, 0, 0, i))],
        out_shape=[jax.ShapeDtypeStruct((bsz, heads, seq, HEAD_LANES), BF16),
                   jax.ShapeDtypeStruct((bsz, heads, V_ROWS, seq), BF16)],
        compiler_params=_cparams("parallel", "parallel"),
        name="mla_keys_values",
    )(*args)


ATTN_HEADS_PER_STEP = 2
ATTN_LOOKAHEAD = 2


def _attn_kernel(q_ref, k_ref, v_ref, kc_ref, vc_ref, o_ref, m_ref, acc_ref, *, hps, kchunk):
    kj = pl.program_id(3)

    def scores(g, k):
        return lax.dot_general(k, q_ref[0, g], (((1,), (1,)), ((), ())), preferred_element_type=F32)

    def update(g, st, vt):
        m_old = m_ref[g]
        m_new = jnp.maximum(m_old, jnp.max(st, axis=0, keepdims=True))
        p = jnp.exp2(st - m_new)
        acc_ref[g] = jnp.exp2(m_old - m_new) * acc_ref[g] + _dot(vt, p.astype(BF16))
        m_ref[g] = m_new

    def sweep(units):
        ahead = [scores(g, k()) for g, k, _ in units[:ATTN_LOOKAHEAD]]
        for i, (g, _, vt) in enumerate(units):
            st = ahead.pop(0)
            if i + ATTN_LOOKAHEAD < len(units):
                nxt = units[i + ATTN_LOOKAHEAD]
                ahead.append(scores(nxt[0], nxt[1]()))
            update(g, st, vt())

    @pl.when(kj == 0)
    def _():
        m_ref[...] = jnp.full_like(m_ref, -jnp.inf)
        acc_ref[...] = jnp.zeros_like(acc_ref)
        sweep([(g, functools.partial(lambda g: kc_ref[0, g], g), functools.partial(lambda g: vc_ref[0, g], g))
               for g in range(hps)])

    tk = k_ref.shape[2]
    sweep([(g,
            functools.partial(lambda g, c: k_ref[0, g, c * kchunk:(c + 1) * kchunk, :], g, c),
            functools.partial(lambda g, c: v_ref[0, g, :, c * kchunk:(c + 1) * kchunk], g, c))
           for c in range(tk // kchunk) for g in range(hps)])

    @pl.when(kj == pl.num_programs(3) - 1)
    def _():
        for g in range(hps):
            acc = acc_ref[g]
            out_t = acc[:LANES] / acc[LANES:LANES + 1]
            o_ref[0, :, g * LANES:(g + 1) * LANES] = out_t.T.astype(o_ref.dtype)


def _attention(q, k, vt, kc, vct, tq_pref, tk_pref, kchunk_pref):
    bsz, heads, seq, _ = q.shape
    lc = kc.shape[2]
    hps = ATTN_HEADS_PER_STEP if heads % ATTN_HEADS_PER_STEP == 0 else 1
    tq = _pick(seq, tq_pref, LANES)
    tk = _pick(seq, tk_pref, LANES)
    kchunk = _pick(tk, kchunk_pref, LANES)
    return pl.pallas_call(
        functools.partial(_attn_kernel, hps=hps, kchunk=kchunk),
        grid=(bsz, heads // hps, seq // tq, seq // tk),
        in_specs=[pl.BlockSpec((1, hps, tq, HEAD_LANES), lambda b, h, i, j: (b, h, i, 0)),
                  pl.BlockSpec((1, hps, tk, HEAD_LANES), lambda b, h, i, j: (b, h, j, 0)),
                  pl.BlockSpec((1, hps, V_ROWS, tk), lambda b, h, i, j: (b, h, 0, j)),
                  pl.BlockSpec((1, hps, lc, HEAD_LANES), lambda b, h, i, j: (b, h, 0, 0)),
                  pl.BlockSpec((1, hps, V_ROWS, lc), lambda b, h, i, j: (b, h, 0, 0))],
        out_specs=pl.BlockSpec((1, tq, hps * LANES), lambda b, h, i, j: (b, i, h)),
        out_shape=jax.ShapeDtypeStruct((bsz, seq, heads * LANES), BF16),
        scratch_shapes=[pltpu.VMEM((hps, 1, tq), F32), pltpu.VMEM((hps, V_ROWS, tq), F32)],
        compiler_params=_cparams("parallel", "parallel", "parallel", "arbitrary"),
        name="attention",
    )(q, k, vt, kc, vct)


def kernel(x, c, ctx, c_ctx, norm1_g, norm2_g, w_ada, b_ada, w_in, hy_conv_w, hy_conv_b, hy_filt_w1, hy_filt_b1, hy_filt_w2, hy_filt_b2, hy_filt_w3, hy_freq, hy_bias, mla_g_qa, mla_w_qb, mla_g_kva, mla_w_kvb, mla_q_norm_g, mla_k_norm_g, w_out, w_mlp1, w_mlp2):
    assert w_ada.shape[0] == 1, "single-layer block"
    bsz, seq, d = x.shape
    lc = ctx.shape[1]
    hyc = hy_conv_b.shape[1]
    width = hy_bias.shape[2]
    q_lora = mla_g_qa.shape[1]
    kv_lora = mla_g_kva.shape[1]
    qk_dim = mla_q_norm_g.shape[1]
    nope = qk_dim - MLA_ROPE
    heads = mla_w_qb.shape[2] // qk_dim
    v_dim = mla_w_kvb.shape[2] // heads - nope
    assert nope == LANES and v_dim == LANES and seq % GRID_W == 0
    q0, kv0, kr0 = hyc, hyc + q_lora, hyc + q_lora + kv_lora
    assert q0 % q_lora == 0 and kv0 % kv_lora == 0 and kr0 % LANES == 0

    cc = jnp.zeros((8, d), F32).at[:bsz].set(c).at[bsz].set(c_ctx)
    mod = _adaln(cc, w_ada[0], b_ada)
    chunk = lambda i: mod[:bsz, i * d:(i + 1) * d][:, None, :]
    sh1, sc1, g1, sh2, sc2, g2 = [chunk(i) for i in range(6)]
    csh1 = mod[bsz:bsz + 1, 0:d][:, None, :]
    csc1 = mod[bsz:bsz + 1, d:2 * d][:, None, :]

    rope_src, _, _ = _rope_lane_map()
    w_in0 = w_in[0]
    np_cols = kr0 + LANES
    np_cols += (-np_cols) % 1024
    w_kr = _gather_cols(w_in0[:, kr0:], rope_src)
    w_pad = jnp.concatenate([w_in0[:, :kr0], w_kr, jnp.zeros((d, np_cols - kr0 - LANES), F32)], axis=1).astype(BF16)
    w_q = _gather_cols(mla_w_qb[0], _head_col_index(heads, nope, qk_dim, nope)).astype(BF16)
    kv_idx = np.concatenate([h * (nope + v_dim) + np.arange(nope) for h in range(heads)]
                            + [h * (nope + v_dim) + nope + np.arange(v_dim) for h in range(heads)])
    w_kv = _gather_cols(mla_w_kvb[0], kv_idx).astype(BF16)
    w_k, w_vt = w_kv[:, :heads * nope], w_kv[:, heads * nope:].T
    head_idx = _head_col_index(1, nope, qk_dim, nope)
    gq = _gather_cols(mla_q_norm_g[0][None, :], head_idx)
    gk = _gather_cols(mla_k_norm_g[0][None, :], head_idx)
    cos, sin = _rope_tables(seq)

    ctx_cols = kv0 - kv0 % 1024
    proj_c = _norm_mod_matmul(ctx.reshape(bsz * lc, d), norm1_g, csh1, csc1, w_pad[:, ctx_cols:],
                              bsz * lc, F32, False, 512, 1024).reshape(bsz, lc, np_cols - ctx_cols)
    k_c, v_c = _mla_keys_values(proj_c, (kv0 - ctx_cols) // kv_lora, (kr0 - ctx_cols) // LANES,
                                mla_g_kva, w_k, w_vt, gk, None, heads, qk_dim)

    x2 = x.reshape(bsz * seq, d)
    proj = _norm_mod_matmul(x2, norm1_g, sh1, sc1, w_pad, seq, F32, False, 512, 1024).reshape(bsz, seq, np_cols)
    y_hy = _hyena(proj, hy_conv_w[0], hy_conv_b[0], hy_filt_w1[0], hy_filt_b1[0], hy_filt_w2[0],
                  hy_filt_b2[0], hy_filt_w3[0], hy_freq[0], hy_bias[0])
    q = _mla_queries(proj, q0 // q_lora, mla_g_qa, w_q, gq, cos, sin, heads, qk_dim, qk_dim ** -0.5 * math.log2(math.e))
    k, v = _mla_keys_values(proj, kv0 // kv_lora, kr0 // LANES, mla_g_kva, w_k, w_vt, gk, (cos, sin), heads, qk_dim)
    y_att = _attention(q, k, v, k_c, v_c, 512, 2048, 512)

    x1 = _out_proj(y_hy.reshape(bsz * seq, width), y_att.reshape(bsz * seq, heads * v_dim),
                   w_out[0].astype(BF16), x2, g1, seq)
    hmid = _norm_mod_matmul(x1, norm2_g, sh2, sc2, w_mlp1[0].astype(BF16), seq, BF16, True, 512, 1024)
    out = _matmul_gated_residual(hmid, w_mlp2[0].astype(BF16), x1, g2, seq)
    return out.reshape(bsz, seq, d)
```

```python
import functools
import math

import numpy as np
import jax
import jax.numpy as jnp
from jax import lax
from jax.experimental import pallas as pl
from jax.experimental.pallas import tpu as pltpu

F32 = jnp.float32
BF16 = jnp.bfloat16

NORM_EPS = 1e-6
GRID_W = 64
MLA_ROPE = 64
ROPE_THETA = 10000.0
HY_POS_BANDS = 16
HY_DECAY_TARGET = 1e-2
HY_FAST_DECAY_PCT = 0.3
HY_SLOW_DECAY_PCT = 1.5

LANES = 128
SUBLANES = 8
VMEM_LIMIT = 56 * 1024 * 1024


def _cparams(*sem):
    return pltpu.CompilerParams(dimension_semantics=sem, vmem_limit_bytes=VMEM_LIMIT)


def _pick(dim, pref, align):
    t = min(pref, dim)
    t -= t % align
    while t >= align:
        if dim % t == 0:
            return t
        t -= align
    return dim


def _dot(a, b):
    return jnp.dot(a, b, preferred_element_type=F32)


def _adaln_kernel(c_ref, w_ref, b_ref, o_ref):
    c = c_ref[...]
    s = c * jax.nn.sigmoid(c)
    o_ref[...] = _dot(s.astype(BF16), w_ref[...].astype(BF16)) + b_ref[...]


def _adaln(cc, w, b):
    d, n = w.shape
    tn = _pick(n, 512, LANES)
    return pl.pallas_call(
        _adaln_kernel,
        grid=(n // tn,),
        in_specs=[pl.BlockSpec((8, d), lambda j: (0, 0)),
                  pl.BlockSpec((d, tn), lambda j: (0, j)),
                  pl.BlockSpec((1, tn), lambda j: (0, j))],
        out_specs=pl.BlockSpec((8, tn), lambda j: (0, j)),
        out_shape=jax.ShapeDtypeStruct((8, n), F32),
        compiler_params=_cparams("parallel"),
        name="adaln",
    )(cc, w, b)


def _nmm_kernel(x_ref, g_ref, sh_ref, sc_ref, w_ref, o_ref, h_ref, *, sq_relu):
    @pl.when(pl.program_id(1) == 0)
    def _():
        x = x_ref[...]
        ms = jnp.mean(x * x, axis=-1, keepdims=True)
        y = x * lax.rsqrt(ms + NORM_EPS) * g_ref[...]
        h_ref[...] = (y * (1.0 + sc_ref[0]) + sh_ref[0]).astype(BF16)

    acc = _dot(h_ref[...], w_ref[...])
    if sq_relu:
        acc = jnp.square(jnp.maximum(acc, 0.0))
    o_ref[...] = acc.astype(o_ref.dtype)


def _norm_mod_matmul(x, g, shift, scale, w, rows_per_mod, out_dtype, sq_relu, tm_pref, tn_pref):
    m, k = x.shape
    n = w.shape[1]
    tm = _pick(math.gcd(m, rows_per_mod), tm_pref, 16)
    tn = _pick(n, tn_pref, LANES)
    mod_idx = lambda i, j: ((i * tm) // rows_per_mod, 0, 0)
    return pl.pallas_call(
        functools.partial(_nmm_kernel, sq_relu=sq_relu),
        grid=(m // tm, n // tn),
        in_specs=[pl.BlockSpec((tm, k), lambda i, j: (i, 0)),
                  pl.BlockSpec((1, k), lambda i, j: (0, 0)),
                  pl.BlockSpec((1, 1, k), mod_idx),
                  pl.BlockSpec((1, 1, k), mod_idx),
                  pl.BlockSpec((k, tn), lambda i, j: (0, j))],
        out_specs=pl.BlockSpec((tm, tn), lambda i, j: (i, j)),
        out_shape=jax.ShapeDtypeStruct((m, n), out_dtype),
        scratch_shapes=[pltpu.VMEM((tm, k), BF16)],
        compiler_params=_cparams("parallel", "arbitrary"),
        name="norm_mod_matmul",
    )(x, g, shift, scale, w)


HALO = 16


def _nmm_conv_kernel(x_ref, xp_ref, xn_ref, g_ref, sh_ref, sc_ref, w_ref, cw_ref, cb_ref, o_ref, h_ref,
                     *, tm, rows_per_seq):
    i = pl.program_id(0)

    def normed(x):
        ms = jnp.mean(x * x, axis=-1, keepdims=True)
        y = x * lax.rsqrt(ms + NORM_EPS) * g_ref[...]
        return y * (1.0 + sc_ref[0]) + sh_ref[0]

    @pl.when(pl.program_id(1) == 0)
    def _():
        first = (i * tm) % rows_per_seq == 0
        last = ((i + 1) * tm) % rows_per_seq == 0
        h_ref[0:HALO, :] = jnp.where(first, 0.0, normed(xp_ref[...])).astype(BF16)
        h_ref[HALO:HALO + tm, :] = normed(x_ref[...]).astype(BF16)
        h_ref[HALO + tm:, :] = jnp.where(last, 0.0, normed(xn_ref[...])).astype(BF16)

    acc = _dot(h_ref[...], w_ref[...])
    rows = acc.shape[0]
    up = pltpu.roll(acc, 1, 0)[HALO:HALO + tm]
    dn = pltpu.roll(acc, rows - 1, 0)[HALO:HALO + tm]
    mid = acc[HALO:HALO + tm]
    o_ref[...] = up * cw_ref[0:1, :] + mid * cw_ref[1:2, :] + dn * cw_ref[2:3, :] + cb_ref[...]


def _norm_mod_matmul_conv(x, g, shift, scale, w, conv_w, conv_b, rows_per_seq, tm_pref, tn_pref):
    m, k = x.shape
    n = w.shape[1]
    tm = _pick(rows_per_seq, tm_pref, HALO)
    tn = _pick(n, tn_pref, LANES)
    r = tm // HALO
    last = m // HALO - 1
    mod_idx = lambda i, j: ((i * tm) // rows_per_seq, 0, 0)
    return pl.pallas_call(
        functools.partial(_nmm_conv_kernel, tm=tm, rows_per_seq=rows_per_seq),
        grid=(m // tm, n // tn),
        in_specs=[pl.BlockSpec((tm, k), lambda i, j: (i, 0)),
                  pl.BlockSpec((HALO, k), lambda i, j: (jnp.maximum(i * r - 1, 0), 0)),
                  pl.BlockSpec((HALO, k), lambda i, j: (jnp.minimum((i + 1) * r, last), 0)),
                  pl.BlockSpec((1, k), lambda i, j: (0, 0)),
                  pl.BlockSpec((1, 1, k), mod_idx),
                  pl.BlockSpec((1, 1, k), mod_idx),
                  pl.BlockSpec((k, tn), lambda i, j: (0, j)),
                  pl.BlockSpec((3, tn), lambda i, j: (0, j)),
                  pl.BlockSpec((1, tn), lambda i, j: (0, j))],
        out_specs=pl.BlockSpec((tm, tn), lambda i, j: (i, j)),
        out_shape=jax.ShapeDtypeStruct((m, n), F32),
        scratch_shapes=[pltpu.VMEM((tm + 2 * HALO, k), BF16)],
        compiler_params=_cparams("parallel", "arbitrary"),
        name="norm_mod_matmul_conv",
    )(x, x, x, g, shift, scale, w, conv_w, conv_b)


def _outproj_kernel(a0_ref, a1_ref, w_ref, res_ref, gate_ref, o_ref, *, k0):
    acc = _dot(a0_ref[...], w_ref[0:k0, :]) + _dot(a1_ref[...], w_ref[k0:, :])
    o_ref[...] = res_ref[...] + gate_ref[0] * acc


def _out_proj(a0, a1, w, res, gate, rows_per_mod):
    m, k0 = a0.shape
    k1 = a1.shape[1]
    n = w.shape[1]
    tm = _pick(math.gcd(m, rows_per_mod), 1024, 16)
    tn = _pick(n, 1024, LANES)
    return pl.pallas_call(
        functools.partial(_outproj_kernel, k0=k0),
        grid=(m // tm, n // tn),
        in_specs=[pl.BlockSpec((tm, k0), lambda i, j: (i, 0)),
                  pl.BlockSpec((tm, k1), lambda i, j: (i, 0)),
                  pl.BlockSpec((k0 + k1, tn), lambda i, j: (0, j)),
                  pl.BlockSpec((tm, tn), lambda i, j: (i, j)),
                  pl.BlockSpec((1, 1, tn), lambda i, j: ((i * tm) // rows_per_mod, 0, j))],
        out_specs=pl.BlockSpec((tm, tn), lambda i, j: (i, j)),
        out_shape=jax.ShapeDtypeStruct((m, n), F32),
        compiler_params=_cparams("parallel", "parallel"),
        name="out_proj",
    )(a0, a1, w, res, gate)


def _mmres_kernel(a_ref, w_ref, res_ref, gate_ref, o_ref, acc_ref):
    kk = pl.program_id(2)

    @pl.when(kk == 0)
    def _():
        acc_ref[...] = jnp.zeros_like(acc_ref)

    acc_ref[...] += _dot(a_ref[...], w_ref[...])

    @pl.when(kk == pl.num_programs(2) - 1)
    def _():
        o_ref[...] = res_ref[...] + gate_ref[0] * acc_ref[...]


def _matmul_gated_residual(a, w, res, gate, rows_per_mod):
    m, k = a.shape
    n = w.shape[1]
    tm = _pick(math.gcd(m, rows_per_mod), 1024, 16)
    tn = _pick(n, 1024, LANES)
    tk = _pick(k, 2048, LANES)
    return pl.pallas_call(
        _mmres_kernel,
        grid=(m // tm, n // tn, k // tk),
        in_specs=[pl.BlockSpec((tm, tk), lambda i, j, kk: (i, kk)),
                  pl.BlockSpec((tk, tn), lambda i, j, kk: (kk, j)),
                  pl.BlockSpec((tm, tn), lambda i, j, kk: (i, j)),
                  pl.BlockSpec((1, 1, tn), lambda i, j, kk: ((i * tm) // rows_per_mod, 0, j))],
        out_specs=pl.BlockSpec((tm, tn), lambda i, j, kk: (i, j)),
        out_shape=jax.ShapeDtypeStruct((m, n), F32),
        scratch_shapes=[pltpu.VMEM((tm, tn), F32)],
        compiler_params=_cparams("parallel", "parallel", "arbitrary"),
        name="matmul_gated_residual",
    )(a, w, res, gate)


def _filt_kernel(feat_ref, t_ref, w1_ref, b1_ref, w2_ref, b2_ref, fr_ref, w3_ref, dl_ref, o_ref, h_ref, *, seq):
    i = pl.program_id(0)
    j = pl.program_id(1)

    @pl.when(j == 0)
    def _():
        fr = fr_ref[...]
        h = jnp.sin(fr * (_dot(feat_ref[...].astype(BF16), w1_ref[...]) + b1_ref[...]))
        h = jnp.sin(fr * (_dot(h.astype(BF16), w2_ref[...]) + b2_ref[...]))
        h_ref[...] = h.astype(BF16)

    f = _dot(h_ref[...], w3_ref[...]) * jnp.exp(-t_ref[...] * dl_ref[...])
    rows = lax.broadcasted_iota(jnp.int32, f.shape, 0) + i * f.shape[0]
    o_ref[0] = jnp.where(rows == seq, 0.0, f)


def _hyena_filters(seq, w1, b1, w2, b2, w3, freq, width):
    lag = jnp.arange(2 * seq)
    src = jnp.where(lag < seq, lag, jnp.clip(2 * seq - lag, 0, seq - 1))
    pos = jnp.arange(seq, dtype=F32)
    t = jnp.linspace(0.0, 1.0, seq, dtype=F32)[:, None]
    bands = jnp.linspace(1e-4, HY_POS_BANDS - 1, HY_POS_BANDS, dtype=F32)
    ang = (2.0 * math.pi / seq) * pos[:, None] * bands[None, :]
    feats = jnp.concatenate([t, jnp.cos(ang), -jnp.sin(ang)], axis=-1)
    pd = feats.shape[1]
    hid = w1.shape[1]
    hp = LANES
    feats = jnp.pad(feats, ((0, 0), (0, hp - pd)))[src]
    t = t[src]
    w1p = jnp.pad(w1, ((0, hp - pd), (0, hp - hid))).astype(BF16)
    w2p = jnp.pad(w2, ((0, hp - hid), (0, hp - hid))).astype(BF16)
    w3p = jnp.pad(w3, ((0, hp - hid), (0, 0))).astype(BF16)
    b1p = jnp.pad(b1, (0, hp - hid))[None, :]
    b2p = jnp.pad(b2, (0, hp - hid))[None, :]
    frp = jnp.pad(freq, (0, hp - hid))[None, :]
    deltas = jnp.abs(jnp.linspace(math.log(HY_DECAY_TARGET) / HY_SLOW_DECAY_PCT,
                                  math.log(HY_DECAY_TARGET) / HY_FAST_DECAY_PCT, width, dtype=F32))
    ncol = w3.shape[1] // 2
    dl = jnp.tile(deltas, ncol // width)[None, :]
    tl = _pick(seq, 512, SUBLANES)
    tn = _pick(ncol, 2048, LANES)
    fwd_tiles = seq // tl
    return pl.pallas_call(
        functools.partial(_filt_kernel, seq=seq),
        grid=(2 * seq // tl, ncol // tn),
        in_specs=[pl.BlockSpec((tl, hp), lambda i, j: (i, 0)),
                  pl.BlockSpec((tl, 1), lambda i, j: (i, 0)),
                  pl.BlockSpec((hp, hp), lambda i, j: (0, 0)),
                  pl.BlockSpec((1, hp), lambda i, j: (0, 0)),
                  pl.BlockSpec((hp, hp), lambda i, j: (0, 0)),
                  pl.BlockSpec((1, hp), lambda i, j: (0, 0)),
                  pl.BlockSpec((1, hp), lambda i, j: (0, 0)),
                  pl.BlockSpec((hp, tn), lambda i, j: (0, (i // fwd_tiles) * (ncol // tn) + j)),
                  pl.BlockSpec((1, tn), lambda i, j: (0, j))],
        out_specs=pl.BlockSpec((1, tl, tn), lambda i, j: (0, i, j)),
        out_shape=jax.ShapeDtypeStruct((1, 2 * seq, ncol), F32),
        scratch_shapes=[pltpu.VMEM((tl, hp), BF16)],
        compiler_params=_cparams("parallel", "arbitrary"),
        name="hyena_filters",
    )(feats, t, w1p, b1p, w2p, b2p, frp, w3p, dl)


K1_GROUP = SUBLANES


def _dft_tables(n1, n2):
    n = n1 * n2
    hl = n1 // 2
    k1 = np.arange(n1)[None, :, None]
    l1 = np.arange(hl)[None, None, :]
    l2 = np.arange(n2)[:, None, None]
    ang = -2.0 * np.pi * ((k1 * (n2 * l1 + l2)) % n) / n
    gr, gi = np.cos(ang), np.sin(ang)
    gm = np.concatenate([np.concatenate([gr, -gi], axis=2), np.concatenate([gi, gr], axis=2)], axis=1)
    a2 = -2.0 * np.pi * ((np.arange(n2)[:, None] * np.arange(n2)[None, :]) % n2) / n2
    fr, fi = np.cos(a2), np.sin(a2)
    f2 = np.block([[fr, -fi], [fi, fr]])
    if2 = np.block([[fr, fi], [-fi, fr]])
    ir, ii = np.transpose(gr, (0, 2, 1)) / n, -np.transpose(gi, (0, 2, 1)) / n
    igm = np.concatenate([np.concatenate([ir, -ii], axis=2), np.concatenate([ii, ir], axis=2)], axis=1)
    angf = -2.0 * np.pi * ((k1 * (n2 * np.arange(n1)[None, None, :] + l2)) % n) / n
    gmr = np.concatenate([np.cos(angf), np.sin(angf)], axis=1)
    to = lambda a: jnp.asarray(a.astype(np.float32)).astype(BF16)
    return to(gm), to(f2), to(if2), to(igm), to(gmr)


def _fft_split(seq):
    n = 2 * seq
    n2 = LANES if n % (LANES * 2 * SUBLANES) == 0 and n // LANES >= 2 * SUBLANES else 2 * SUBLANES
    n1 = n // n2
    assert n1 * n2 == n and n1 % (2 * SUBLANES) == 0 and n2 % SUBLANES == 0, (n1, n2)
    return n1, n2


def _fill_pitched(src_ref, p_ref, *, nb, rows, n2, pitch):
    def body(l1, carry):
        for b in range(nb):
            p_ref[b, pl.ds(l1, n2, stride=pitch), :] = src_ref[b, pl.ds(pl.multiple_of(l1 * n2, n2), n2), :]
        return carry

    lax.fori_loop(0, rows, body, 0, unroll=2)


def _stage1_to_yv(p_ref, gm_ref, yv_ref, step, *, nb, n1, l2c, pitch):
    rows = n1 // nb

    def body(t, carry):
        l2 = step * l2c + t
        base = pl.multiple_of(l2 * pitch, SUBLANES)
        rhs = jnp.concatenate([p_ref[b, pl.ds(base, rows), :] for b in range(nb)], axis=0)
        res = _dot(gm_ref[t], rhs.astype(BF16))
        row0 = pl.multiple_of(l2 * K1_GROUP, K1_GROUP)
        for g in range(n1 // K1_GROUP):
            for ri in range(2):
                lo = ri * n1 + g * K1_GROUP
                yv_ref[g * 2 + ri, pl.ds(row0, K1_GROUP), :] = res[lo:lo + K1_GROUP]
        return carry

    lax.fori_loop(0, l2c, body, 0, unroll=8)


def _yv_pair(yv_ref, g, t, n2):
    rows = lambda tt: pl.ds(tt, n2, stride=K1_GROUP)
    re = jnp.concatenate([yv_ref[g * 2, rows(t), :], yv_ref[g * 2, rows(t + 1), :]], axis=1)
    im = jnp.concatenate([yv_ref[g * 2 + 1, rows(t), :], yv_ref[g * 2 + 1, rows(t + 1), :]], axis=1)
    return jnp.concatenate([re, im], axis=0).astype(BF16)


def _filtspec_kernel(filt_ref, gmr_ref, f2_ref, o_ref, yv_ref, p_ref, *, n1, n2, l2c, g2, s1, pitch):
    s = pl.program_id(1)

    @pl.when(s == 0)
    def _():
        _fill_pitched(filt_ref, p_ref, nb=1, rows=n1, n2=n2, pitch=pitch)

    @pl.when(s < s1)
    def _():
        _stage1_to_yv(p_ref, gmr_ref, yv_ref, s, nb=1, n1=n1, l2c=l2c, pitch=pitch)

    @pl.when(s >= s1)
    def _():
        for gg in range(g2):
            g = (s - s1) * g2 + gg
            for t in range(0, K1_GROUP, 2):
                x = _dot(f2_ref[...], _yv_pair(yv_ref, g, t, n2))
                for d in range(2):
                    o_ref[gg * K1_GROUP + t + d, 0] = x[:n2, d * LANES:(d + 1) * LANES]
                    o_ref[gg * K1_GROUP + t + d, 1] = x[n2:, d * LANES:(d + 1) * LANES]


def _filter_spectrum(filt, gmr, f2, n1, n2):
    c = filt.shape[-1]
    ng = n1 // K1_GROUP
    l2c = _pick(n2, 32, 4)
    g2 = _pick(ng, 2, 1)
    s1, s2 = n2 // l2c, ng // g2
    pitch = n1 + SUBLANES
    return pl.pallas_call(
        functools.partial(_filtspec_kernel, n1=n1, n2=n2, l2c=l2c, g2=g2, s1=s1, pitch=pitch),
        grid=(c // LANES, s1 + s2),
        in_specs=[pl.BlockSpec((1, n1 * n2, LANES), lambda cb, s: (0, 0, cb), pipeline_mode=pl.Buffered(1)),
                  pl.BlockSpec((l2c, 2 * n1, n1), lambda cb, s: (jnp.minimum(s, s1 - 1), 0, 0)),
                  pl.BlockSpec((2 * n2, 2 * n2), lambda cb, s: (0, 0))],
        out_specs=pl.BlockSpec((g2 * K1_GROUP, 2, n2, LANES), lambda cb, s: (jnp.maximum(s - s1, 0), 0, 0, cb)),
        out_shape=jax.ShapeDtypeStruct((n1, 2, n2, c), F32),
        scratch_shapes=[pltpu.VMEM((2 * ng, n2 * K1_GROUP, LANES), F32), pltpu.VMEM((1, n2 * pitch, LANES), F32)],
        compiler_params=_cparams("parallel", "arbitrary"),
        name="filter_spectrum",
    )(filt, gmr, f2)


def _fftconv_kernel(sig_ref, gm_ref, k_ref, f2_ref, if2_ref, igm_ref, gate_ref, bias_ref, o_ref, yv_ref, p_ref,
                    *, n1, n2, l2c, g2, l3, rows4, s1, s2, s3, pitch):
    s = pl.program_id(1)
    hl = n1 // 2
    ng = n1 // K1_GROUP

    @pl.when(s == 0)
    def _():
        _fill_pitched(sig_ref, p_ref, nb=2, rows=hl, n2=n2, pitch=pitch)

    @pl.when(s < s1)
    def _():
        _stage1_to_yv(p_ref, gm_ref, yv_ref, s, nb=2, n1=n1, l2c=l2c, pitch=pitch)

    @pl.when((s >= s1) & (s < s1 + s2))
    def _():
        for gg in range(g2):
            g = (s - s1) * g2 + gg
            rows = lambda tt: pl.ds(tt, n2, stride=K1_GROUP)
            for t in range(0, K1_GROUP, 2):
                x = _dot(f2_ref[...], _yv_pair(yv_ref, g, t, n2))
                xr, xi = x[:n2], x[n2:]
                kt = gg * K1_GROUP + t
                kr = jnp.concatenate([k_ref[kt, 0], k_ref[kt + 1, 0]], axis=1)
                ki = jnp.concatenate([k_ref[kt, 1], k_ref[kt + 1, 1]], axis=1)
                p = jnp.concatenate([xr * kr - xi * ki, xr * ki + xi * kr], axis=0).astype(BF16)
                yp = _dot(if2_ref[...], p)
                for d in range(2):
                    for ri in range(2):
                        yv_ref[g * 2 + ri, rows(t + d), :] = yp[ri * n2:(ri + 1) * n2, d * LANES:(d + 1) * LANES]

    @pl.when((s >= s1 + s2) & (s < s1 + s2 + s3))
    def _():
        def body(i, carry):
            l2 = (s - s1 - s2) * (l3 * SUBLANES) + i
            row0 = pl.multiple_of(l2 * K1_GROUP, K1_GROUP)
            tiles = [yv_ref[g * 2 + ri, pl.ds(row0, K1_GROUP), :] for ri in range(2) for g in range(ng)]
            res = _dot(igm_ref[i], jnp.concatenate(tiles, axis=0).astype(BF16))
            base = pl.multiple_of(l2 * pitch, SUBLANES)
            for b in range(2):
                p_ref[b, pl.ds(base, hl), :] = res[b * hl:(b + 1) * hl]
            return carry

        lax.fori_loop(0, l3 * SUBLANES, body, 0, unroll=8)

    @pl.when(s >= s1 + s2 + s3)
    def _():
        r = s - s1 - s2 - s3
        for j in range(rows4 // n2):
            l1 = r * (rows4 // n2) + j
            nat = pl.ds(pl.multiple_of(l1 * n2, n2), n2)
            for b in range(2):
                conv = p_ref[b, pl.ds(l1, n2, stride=pitch), :] + bias_ref[...] * sig_ref[b, nat, :]
                o_ref[b, j * n2:(j + 1) * n2, :] = (gate_ref[b, j * n2:(j + 1) * n2, :] * conv).astype(o_ref.dtype)


def _fft_conv_gated(sig, kspec, kcol_block0, gate, gate_col_block0, bias, tables, n1, n2, out_dtype):
    gm, f2, if2, igm = tables
    c = bias.shape[1]
    seq = sig.shape[1]
    ng = n1 // K1_GROUP
    l2c = _pick(n2, 32, 4)
    g2 = _pick(ng, 2, 1)
    l3 = _pick(n2 // SUBLANES, 2, 1)
    rows4 = _pick(seq, 2048, n2)
    s1, s2, s3, s4 = n2 // l2c, ng // g2, n2 // (l3 * SUBLANES), seq // rows4
    pitch = n1 // 2 + SUBLANES
    clamp = lambda v, n: jnp.clip(v, 0, n - 1)
    return pl.pallas_call(
        functools.partial(_fftconv_kernel, n1=n1, n2=n2, l2c=l2c, g2=g2, l3=l3, rows4=rows4, s1=s1, s2=s2, s3=s3,
                          pitch=pitch),
        grid=(c // LANES, s1 + s2 + s3 + s4),
        in_specs=[pl.BlockSpec((2, seq, LANES), lambda cb, s: (0, 0, cb), pipeline_mode=pl.Buffered(1)),
                  pl.BlockSpec((l2c, 2 * n1, n1), lambda cb, s: (clamp(s, s1), 0, 0)),
                  pl.BlockSpec((g2 * K1_GROUP, 2, n2, LANES),
                               lambda cb, s: (clamp(s - s1, s2), 0, 0, kcol_block0 + cb)),
                  pl.BlockSpec((2 * n2, 2 * n2), lambda cb, s: (0, 0)),
                  pl.BlockSpec((2 * n2, 2 * n2), lambda cb, s: (0, 0)),
                  pl.BlockSpec((l3 * SUBLANES, n1, 2 * n1), lambda cb, s: (clamp(s - s1 - s2, s3), 0, 0)),
                  pl.BlockSpec((2, rows4, LANES),
                               lambda cb, s: (0, clamp(s - s1 - s2 - s3, s4), gate_col_block0 + cb)),
                  pl.BlockSpec((1, LANES), lambda cb, s: (0, cb))],
        out_specs=pl.BlockSpec((2, rows4, LANES), lambda cb, s: (0, clamp(s - s1 - s2 - s3, s4), cb)),
        out_shape=jax.ShapeDtypeStruct((2, seq, c), out_dtype),
        scratch_shapes=[pltpu.VMEM((2 * ng, n2 * K1_GROUP, LANES), F32), pltpu.VMEM((2, n2 * pitch, LANES), F32)],
        compiler_params=_cparams("parallel", "arbitrary"),
        name="fft_conv_gated",
    )(sig, gm, kspec, f2, if2, igm, gate, bias)


def _hyena(u, fw1, fb1, fw2, fb2, fw3, freq, hy_bias):
    bsz, seq, _ = u.shape
    assert bsz == 2, "the FFT convolution packs exactly two batch elements into one complex signal"
    width = hy_bias.shape[1]
    n1, n2 = _fft_split(seq)
    gm, f2, if2, igm, gmr = _dft_tables(n1, n2)

    filt = _hyena_filters(seq, fw1, fb1, fw2, fb2, fw3, freq, width)
    kspec = _filter_spectrum(filt, gmr, f2, n1, n2)

    tables = (gm, f2, if2, igm)
    wb = width // LANES
    z = _fft_conv_gated(u, kspec, 0, u, wb, hy_bias[0:1], tables, n1, n2, F32)
    return _fft_conv_gated(z, kspec, wb, u, 2 * wb, hy_bias[1:2], tables, n1, n2, BF16)


HEAD_LANES = 2 * LANES


def _rope_lane_map():
    lane = np.arange(LANES)
    half = lane // 64
    rem = lane % 64
    valid = rem < 32
    axis = rem // 16
    f = rem % 16
    src = np.where(valid, axis * 32 + half * 16 + f, -1)
    tab = np.where(valid, axis * 16 + f, -1)
    return src, tab, half


def _gather_cols(w, idx):
    wp = jnp.concatenate([w, jnp.zeros(w.shape[:-1] + (1,), w.dtype)], axis=-1)
    return jnp.take(wp, jnp.asarray(np.where(idx < 0, w.shape[-1], idx)), axis=-1)


def _head_col_index(heads, nope, per_head, rope_off):
    src, _, _ = _rope_lane_map()
    idx = []
    for h in range(heads):
        base = h * per_head
        idx.append(base + np.arange(nope))
        idx.append(np.where(src < 0, -1, base + rope_off + src))
    return np.concatenate(idx)


def _rope_tables(seq):
    rows = seq // GRID_W
    row = jnp.repeat(jnp.arange(rows, dtype=F32), GRID_W)
    col = jnp.tile(jnp.arange(GRID_W, dtype=F32), rows)
    half = MLA_ROPE // 2
    inv = ROPE_THETA ** (-jnp.arange(0, half, 2, dtype=F32) / half)
    ang = jnp.concatenate([row[:, None] * inv, col[:, None] * inv], axis=-1)
    _, tab, hf = _rope_lane_map()
    cos = _gather_cols(jnp.cos(ang), tab)
    sin = _gather_cols(jnp.sin(ang), tab) * jnp.asarray(np.where(hf == 0, -1.0, 1.0), F32)
    return cos, sin


def _rms(x, g):
    return x * lax.rsqrt(jnp.mean(x * x, axis=-1, keepdims=True) + NORM_EPS) * g


def _q_kernel(qa_ref, g_ref, w_ref, gn_ref, cos_ref, sin_ref, o_ref, *, heads, qk_dim, out_scale):
    xn = _rms(qa_ref[...], g_ref[...]).astype(BF16)
    q = _dot(xn, w_ref[...])
    gn = gn_ref[...]
    cos, sin = cos_ref[...], sin_ref[...]
    for h in range(heads):
        qh = q[:, h * HEAD_LANES:(h + 1) * HEAD_LANES]
        inv = lax.rsqrt(jnp.sum(qh * qh, axis=-1, keepdims=True) / qk_dim + NORM_EPS)
        qn = qh * inv * gn
        r = qn[:, LANES:]
        r = r * cos + pltpu.roll(r, 64, 1) * sin
        o_ref[0, h, :, 0:LANES] = (qn[:, :LANES] * out_scale).astype(BF16)
        o_ref[0, h, :, LANES:] = (r * out_scale).astype(BF16)


def _mla_queries(proj, col_block, g_qa, w_q, gn, cos, sin, heads, qk_dim, out_scale):
    bsz, seq, _ = proj.shape
    r = g_qa.shape[1]
    tm = _pick(seq, 512, 16)
    return pl.pallas_call(
        functools.partial(_q_kernel, heads=heads, qk_dim=qk_dim, out_scale=out_scale),
        grid=(bsz, seq // tm),
        in_specs=[pl.BlockSpec((None, tm, r), lambda b, i: (b, i, col_block)),
                  pl.BlockSpec((1, r), lambda b, i: (0, 0)),
                  pl.BlockSpec((r, heads * HEAD_LANES), lambda b, i: (0, 0)),
                  pl.BlockSpec((1, HEAD_LANES), lambda b, i: (0, 0)),
                  pl.BlockSpec((tm, LANES), lambda b, i: (i, 0)),
                  pl.BlockSpec((tm, LANES), lambda b, i: (i, 0))],
        out_specs=pl.BlockSpec((1, heads, tm, HEAD_LANES), lambda b, i: (b, 0, i, 0)),
        out_shape=jax.ShapeDtypeStruct((bsz, heads, seq, HEAD_LANES), BF16),
        compiler_params=_cparams("parallel", "parallel"),
        name="mla_queries",
    )(proj, g_qa, w_q, gn, cos, sin)


V_ROWS = LANES + 16


def _kv_kernel(kva_ref, kr_ref, g_ref, wk_ref, wvt_ref, gn_ref, *rest, heads, qk_dim, rope):
    if rope:
        cos_ref, sin_ref, k_ref, v_ref = rest
    else:
        k_ref, v_ref = rest
    xn = _rms(kva_ref[...], g_ref[...]).astype(BF16)
    kk = _dot(xn, wk_ref[...])
    vt = lax.dot_general(wvt_ref[...], xn, (((1,), (1,)), ((), ())), preferred_element_type=F32)
    gn = gn_ref[...]
    kr = kr_ref[...]
    ssr = jnp.sum(kr * kr, axis=-1, keepdims=True)
    krg = kr * gn[:, LANES:]
    if rope:
        krg = krg * cos_ref[...] + pltpu.roll(krg, 64, 1) * sin_ref[...]
    ones_row = (lax.broadcasted_iota(jnp.int32, (V_ROWS - LANES, kr.shape[0]), 0) == 0).astype(BF16)
    for h in range(heads):
        kn = kk[:, h * LANES:(h + 1) * LANES]
        inv = lax.rsqrt((jnp.sum(kn * kn, axis=-1, keepdims=True) + ssr) / qk_dim + NORM_EPS)
        k_ref[0, h, :, 0:LANES] = (kn * inv * gn[:, :LANES]).astype(BF16)
        k_ref[0, h, :, LANES:] = (krg * inv).astype(BF16)
        v_ref[0, h, 0:LANES, :] = vt[h * LANES:(h + 1) * LANES].astype(BF16)
        v_ref[0, h, LANES:, :] = ones_row


def _mla_keys_values(proj, kv_block, kr_block, g_kva, w_k, w_vt, gn, rope, heads, qk_dim):
    bsz, seq, _ = proj.shape
    r = g_kva.shape[1]
    tm = _pick(seq, 512, LANES)
    in_specs = [pl.BlockSpec((None, tm, r), lambda b, i: (b, i, kv_block)),
                pl.BlockSpec((None, tm, LANES), lambda b, i: (b, i, kr_block)),
                pl.BlockSpec((1, r), lambda b, i: (0, 0)),
                pl.BlockSpec((r, heads * LANES), lambda b, i: (0, 0)),
                pl.BlockSpec((heads * LANES, r), lambda b, i: (0, 0)),
                pl.BlockSpec((1, HEAD_LANES), lambda b, i: (0, 0))]
    args = [proj, proj, g_kva, w_k, w_vt, gn]
    if rope is not None:
        in_specs += [pl.BlockSpec((tm, LANES), lambda b, i: (i, 0))] * 2
        args += list(rope)
    return pl.pallas_call(
        functools.partial(_kv_kernel, heads=heads, qk_dim=qk_dim, rope=rope is not None),
        grid=(bsz, seq // tm),
        in_specs=in_specs,
        out_specs=[pl.BlockSpec((1, heads, tm, HEAD_LANES), lambda b, i: (b, 0, i, 0)),
                   pl.BlockSpec((1, heads, V_ROWS, tm), lambda b, i: (b, 0, 0, i))],
        out_shape=[jax.ShapeDtypeStruct((bsz, heads, seq, HEAD_LANES), BF16),
                   jax.ShapeDtypeStruct((bsz, heads, V_ROWS, seq), BF16)],
        compiler_params=_cparams("parallel", "parallel"),
        name="mla_keys_values",
    )(*args)


ATTN_HEADS_PER_STEP = 2
ATTN_LOOKAHEAD = 2


def _attn_kernel(q_ref, k_ref, v_ref, kc_ref, vc_ref, o_ref, m_ref, acc_ref, *, hps, kchunk):
    kj = pl.program_id(3)

    def scores(g, k):
        return lax.dot_general(k, q_ref[0, g], (((1,), (1,)), ((), ())), preferred_element_type=F32)

    def update(g, st, vt):
        m_old = m_ref[g]
        m_new = jnp.maximum(m_old, jnp.max(st, axis=0, keepdims=True))
        p = jnp.exp2(st - m_new)
        acc_ref[g] = jnp.exp2(m_old - m_new) * acc_ref[g] + _dot(vt, p.astype(BF16))
        m_ref[g] = m_new

    def sweep(units):
        ahead = [scores(g, k()) for g, k, _ in units[:ATTN_LOOKAHEAD]]
        for i, (g, _, vt) in enumerate(units):
            st = ahead.pop(0)
            if i + ATTN_LOOKAHEAD < len(units):
                nxt = units[i + ATTN_LOOKAHEAD]
                ahead.append(scores(nxt[0], nxt[1]()))
            update(g, st, vt())

    @pl.when(kj == 0)
    def _():
        m_ref[...] = jnp.full_like(m_ref, -jnp.inf)
        acc_ref[...] = jnp.zeros_like(acc_ref)
        sweep([(g, functools.partial(lambda g: kc_ref[0, g], g), functools.partial(lambda g: vc_ref[0, g], g))
               for g in range(hps)])

    tk = k_ref.shape[2]
    sweep([(g,
            functools.partial(lambda g, c: k_ref[0, g, c * kchunk:(c + 1) * kchunk, :], g, c),
            functools.partial(lambda g, c: v_ref[0, g, :, c * kchunk:(c + 1) * kchunk], g, c))
           for c in range(tk // kchunk) for g in range(hps)])

    @pl.when(kj == pl.num_programs(3) - 1)
    def _():
        for g in range(hps):
            acc = acc_ref[g]
            out_t = acc[:LANES] / acc[LANES:LANES + 1]
            o_ref[0, :, g * LANES:(g + 1) * LANES] = out_t.T.astype(o_ref.dtype)


def _attention(q, k, vt, kc, vct, tq_pref, tk_pref, kchunk_pref):
    bsz, heads, seq, _ = q.shape
    lc = kc.shape[2]
    hps = ATTN_HEADS_PER_STEP if heads % ATTN_HEADS_PER_STEP == 0 else 1
    tq = _pick(seq, tq_pref, LANES)
    tk = _pick(seq, tk_pref, LANES)
    kchunk = _pick(tk, kchunk_pref, LANES)
    return pl.pallas_call(
        functools.partial(_attn_kernel, hps=hps, kchunk=kchunk),
        grid=(bsz, heads // hps, seq // tq, seq // tk),
        in_specs=[pl.BlockSpec((1, hps, tq, HEAD_LANES), lambda b, h, i, j: (b, h, i, 0)),
                  pl.BlockSpec((1, hps, tk, HEAD_LANES), lambda b, h, i, j: (b, h, j, 0)),
                  pl.BlockSpec((1, hps, V_ROWS, tk), lambda b, h, i, j: (b, h, 0, j)),
                  pl.BlockSpec((1, hps, lc, HEAD_LANES), lambda b, h, i, j: (b, h, 0, 0)),
                  pl.BlockSpec((1, hps, V_ROWS, lc), lambda b, h, i, j: (b, h, 0, 0))],
        out_specs=pl.BlockSpec((1, tq, hps * LANES), lambda b, h, i, j: (b, i, h)),
        out_shape=jax.ShapeDtypeStruct((bsz, seq, heads * LANES), BF16),
        scratch_shapes=[pltpu.VMEM((hps, 1, tq), F32), pltpu.VMEM((hps, V_ROWS, tq), F32)],
        compiler_params=_cparams("parallel", "parallel", "parallel", "arbitrary"),
        name="attention",
    )(q, k, vt, kc, vct)


def kernel(x, c, ctx, c_ctx, norm1_g, norm2_g, w_ada, b_ada, w_in, hy_conv_w, hy_conv_b, hy_filt_w1, hy_filt_b1, hy_filt_w2, hy_filt_b2, hy_filt_w3, hy_freq, hy_bias, mla_g_qa, mla_w_qb, mla_g_kva, mla_w_kvb, mla_q_norm_g, mla_k_norm_g, w_out, w_mlp1, w_mlp2):
    assert w_ada.shape[0] == 1, "single-layer block"
    bsz, seq, d = x.shape
    lc = ctx.shape[1]
    hyc = hy_conv_b.shape[1]
    width = hy_bias.shape[2]
    q_lora = mla_g_qa.shape[1]
    kv_lora = mla_g_kva.shape[1]
    qk_dim = mla_q_norm_g.shape[1]
    nope = qk_dim - MLA_ROPE
    heads = mla_w_qb.shape[2] // qk_dim
    v_dim = mla_w_kvb.shape[2] // heads - nope
    assert nope == LANES and v_dim == LANES and seq % GRID_W == 0
    q0, kv0, kr0 = hyc, hyc + q_lora, hyc + q_lora + kv_lora
    assert q0 % q_lora == 0 and kv0 % kv_lora == 0 and kr0 % LANES == 0

    cc = jnp.zeros((8, d), F32).at[:bsz].set(c).at[bsz].set(c_ctx)
    mod = _adaln(cc, w_ada[0], b_ada)
    chunk = lambda i: mod[:bsz, i * d:(i + 1) * d][:, None, :]
    sh1, sc1, g1, sh2, sc2, g2 = [chunk(i) for i in range(6)]
    csh1 = mod[bsz:bsz + 1, 0:d][:, None, :]
    csc1 = mod[bsz:bsz + 1, d:2 * d][:, None, :]

    rope_src, _, _ = _rope_lane_map()
    w_in0 = w_in[0]
    np_cols = kr0 + LANES
    np_cols += (-np_cols) % 1024
    w_kr = _gather_cols(w_in0[:, kr0:], rope_src)
    w_pad = jnp.concatenate([w_in0[:, :kr0], w_kr, jnp.zeros((d, np_cols - kr0 - LANES), F32)], axis=1).astype(BF16)
    w_q = _gather_cols(mla_w_qb[0], _head_col_index(heads, nope, qk_dim, nope)).astype(BF16)
    kv_idx = np.concatenate([h * (nope + v_dim) + np.arange(nope) for h in range(heads)]
                            + [h * (nope + v_dim) + nope + np.arange(v_dim) for h in range(heads)])
    w_kv = _gather_cols(mla_w_kvb[0], kv_idx).astype(BF16)
    w_k, w_vt = w_kv[:, :heads * nope], w_kv[:, heads * nope:].T
    head_idx = _head_col_index(1, nope, qk_dim, nope)
    gq = _gather_cols(mla_q_norm_g[0][None, :], head_idx)
    gk = _gather_cols(mla_k_norm_g[0][None, :], head_idx)
    cos, sin = _rope_tables(seq)

    ctx_cols = kv0 - kv0 % 1024
    proj_c = _norm_mod_matmul(ctx.reshape(bsz * lc, d), norm1_g, csh1, csc1, w_pad[:, ctx_cols:],
                              bsz * lc, F32, False, 512, 1024).reshape(bsz, lc, np_cols - ctx_cols)
    k_c, v_c = _mla_keys_values(proj_c, (kv0 - ctx_cols) // kv_lora, (kr0 - ctx_cols) // LANES,
                                mla_g_kva, w_k, w_vt, gk, None, heads, qk_dim)

    x2 = x.reshape(bsz * seq, d)
    taps = jnp.zeros((3, np_cols), F32).at[1].set(1.0).at[:, :hyc].set(hy_conv_w[0])
    tap_bias = jnp.zeros((1, np_cols), F32).at[0, :hyc].set(hy_conv_b[0])
    proj = _norm_mod_matmul_conv(x2, norm1_g, sh1, sc1, w_pad, taps, tap_bias, seq, 512, 1024)
    proj = proj.reshape(bsz, seq, np_cols)
    y_hy = _hyena(proj, hy_filt_w1[0], hy_filt_b1[0], hy_filt_w2[0], hy_filt_b2[0], hy_filt_w3[0], hy_freq[0],
                  hy_bias[0])
    q = _mla_queries(proj, q0 // q_lora, mla_g_qa, w_q, gq, cos, sin, heads, qk_dim, qk_dim ** -0.5 * math.log2(math.e))
    k, v = _mla_keys_values(proj, kv0 // kv_lora, kr0 // LANES, mla_g_kva, w_k, w_vt, gk, (cos, sin), heads, qk_dim)
    y_att = _attention(q, k, v, k_c, v_c, 512, 2048, 512)

    x1 = _out_proj(y_hy.reshape(bsz * seq, width), y_att.reshape(bsz * seq, heads * v_dim),
                   w_out[0].astype(BF16), x2, g1, seq)
    hmid = _norm_mod_matmul(x1, norm2_g, sh2, sc2, w_mlp1[0].astype(BF16), seq, BF16, True, 512, 1024)
    out = _matmul_gated_residual(hmid, w_mlp2[0].astype(BF16), x1, g2, seq)
    return out.reshape(bsz, seq, d)
```

```python
import functools
import math

import numpy as np
import jax
import jax.numpy as jnp
from jax import lax
from jax.experimental import pallas as pl
from jax.experimental.pallas import tpu as pltpu

F32 = jnp.float32
BF16 = jnp.bfloat16

NORM_EPS = 1e-6
GRID_W = 64
MLA_ROPE = 64
ROPE_THETA = 10000.0
HY_POS_BANDS = 16
HY_DECAY_TARGET = 1e-2
HY_FAST_DECAY_PCT = 0.3
HY_SLOW_DECAY_PCT = 1.5

LANES = 128
SUBLANES = 8
VMEM_LIMIT = 56 * 1024 * 1024


def _cparams(*sem):
    return pltpu.CompilerParams(dimension_semantics=sem, vmem_limit_bytes=VMEM_LIMIT)


def _pick(dim, pref, align):
    t = min(pref, dim)
    t -= t % align
    while t >= align:
        if dim % t == 0:
            return t
        t -= align
    return dim


def _dot(a, b):
    return jnp.dot(a, b, preferred_element_type=F32)


def _adaln_kernel(c_ref, w_ref, b_ref, o_ref):
    c = c_ref[...]
    s = c * jax.nn.sigmoid(c)
    o_ref[...] = _dot(s.astype(BF16), w_ref[...].astype(BF16)) + b_ref[...]


def _adaln(cc, w, b):
    d, n = w.shape
    tn = _pick(n, 512, LANES)
    return pl.pallas_call(
        _adaln_kernel,
        grid=(n // tn,),
        in_specs=[pl.BlockSpec((8, d), lambda j: (0, 0)),
                  pl.BlockSpec((d, tn), lambda j: (0, j)),
                  pl.BlockSpec((1, tn), lambda j: (0, j))],
        out_specs=pl.BlockSpec((8, tn), lambda j: (0, j)),
        out_shape=jax.ShapeDtypeStruct((8, n), F32),
        compiler_params=_cparams("parallel"),
        name="adaln",
    )(cc, w, b)


def _nmm_kernel(x_ref, g_ref, sh_ref, sc_ref, w_ref, o_ref, h_ref, *, sq_relu):
    @pl.when(pl.program_id(1) == 0)
    def _():
        x = x_ref[...]
        ms = jnp.mean(x * x, axis=-1, keepdims=True)
        y = x * lax.rsqrt(ms + NORM_EPS) * g_ref[...]
        h_ref[...] = (y * (1.0 + sc_ref[0]) + sh_ref[0]).astype(BF16)

    acc = _dot(h_ref[...], w_ref[...])
    if sq_relu:
        acc = jnp.square(jnp.maximum(acc, 0.0))
    o_ref[...] = acc.astype(o_ref.dtype)


def _norm_mod_matmul(x, g, shift, scale, w, rows_per_mod, out_dtype, sq_relu, tm_pref, tn_pref, col0=0):
    m, k = x.shape
    n = w.shape[1] - col0
    tm = _pick(math.gcd(m, rows_per_mod), tm_pref, 16)
    tn = _pick(math.gcd(n, col0) if col0 else n, tn_pref, LANES)
    mod_idx = lambda i, j: ((i * tm) // rows_per_mod, 0, 0)
    return pl.pallas_call(
        functools.partial(_nmm_kernel, sq_relu=sq_relu),
        grid=(m // tm, n // tn),
        in_specs=[pl.BlockSpec((tm, k), lambda i, j: (i, 0)),
                  pl.BlockSpec((1, k), lambda i, j: (0, 0)),
                  pl.BlockSpec((1, 1, k), mod_idx),
                  pl.BlockSpec((1, 1, k), mod_idx),
                  pl.BlockSpec((k, tn), lambda i, j: (0, col0 // tn + j))],
        out_specs=pl.BlockSpec((tm, tn), lambda i, j: (i, j)),
        out_shape=jax.ShapeDtypeStruct((m, n), out_dtype),
        scratch_shapes=[pltpu.VMEM((tm, k), BF16)],
        compiler_params=_cparams("parallel", "arbitrary"),
        name="norm_mod_matmul",
    )(x, g, shift, scale, w)


HALO = 16


def _nmm_conv_kernel(x_ref, xp_ref, xn_ref, g_ref, sh_ref, sc_ref, w_ref, cw_ref, cb_ref, o_ref, h_ref,
                     *, tm, rows_per_seq):
    i = pl.program_id(0)

    def normed(x):
        ms = jnp.mean(x * x, axis=-1, keepdims=True)
        y = x * lax.rsqrt(ms + NORM_EPS) * g_ref[...]
        return y * (1.0 + sc_ref[0]) + sh_ref[0]

    @pl.when(pl.program_id(1) == 0)
    def _():
        first = (i * tm) % rows_per_seq == 0
        last = ((i + 1) * tm) % rows_per_seq == 0
        h_ref[0:HALO, :] = jnp.where(first, 0.0, normed(xp_ref[...])).astype(BF16)
        h_ref[HALO:HALO + tm, :] = normed(x_ref[...]).astype(BF16)
        h_ref[HALO + tm:, :] = jnp.where(last, 0.0, normed(xn_ref[...])).astype(BF16)

    acc = _dot(h_ref[...], w_ref[...])
    rows = acc.shape[0]
    mid = acc[HALO:HALO + tm]
    up = pltpu.roll(acc, 1, 0)[HALO:HALO + tm]
    dn = pltpu.roll(acc, rows - 1, 0)[HALO:HALO + tm]
    o_ref[...] = up * cw_ref[0:1, :] + mid * cw_ref[1:2, :] + dn * cw_ref[2:3, :] + cb_ref[...]


def _norm_mod_matmul_conv(x, g, shift, scale, w, conv_w, conv_b, rows_per_seq, tm_pref, tn_pref):
    m, k = x.shape
    n = w.shape[1]
    tm = _pick(rows_per_seq, tm_pref, HALO)
    tn = _pick(n, tn_pref, LANES)
    r = tm // HALO
    last = m // HALO - 1
    mod_idx = lambda i, j: ((i * tm) // rows_per_seq, 0, 0)
    return pl.pallas_call(
        functools.partial(_nmm_conv_kernel, tm=tm, rows_per_seq=rows_per_seq),
        grid=(m // tm, n // tn),
        in_specs=[pl.BlockSpec((tm, k), lambda i, j: (i, 0)),
                  pl.BlockSpec((HALO, k), lambda i, j: (jnp.maximum(i * r - 1, 0), 0)),
                  pl.BlockSpec((HALO, k), lambda i, j: (jnp.minimum((i + 1) * r, last), 0)),
                  pl.BlockSpec((1, k), lambda i, j: (0, 0)),
                  pl.BlockSpec((1, 1, k), mod_idx),
                  pl.BlockSpec((1, 1, k), mod_idx),
                  pl.BlockSpec((k, tn), lambda i, j: (0, j)),
                  pl.BlockSpec((3, tn), lambda i, j: (0, j)),
                  pl.BlockSpec((1, tn), lambda i, j: (0, j))],
        out_specs=pl.BlockSpec((tm, tn), lambda i, j: (i, j)),
        out_shape=jax.ShapeDtypeStruct((m, n), F32),
        scratch_shapes=[pltpu.VMEM((tm + 2 * HALO, k), BF16)],
        compiler_params=_cparams("parallel", "arbitrary"),
        name="norm_mod_matmul_conv",
    )(x, x, x, g, shift, scale, w, conv_w, conv_b)


def _outproj_kernel(a0_ref, a1_ref, w_ref, res_ref, gate_ref, o_ref, *, k0):
    acc = _dot(a0_ref[...], w_ref[0:k0, :]) + _dot(a1_ref[...], w_ref[k0:, :])
    o_ref[...] = res_ref[...] + gate_ref[0] * acc


def _out_proj(a0, a1, w, res, gate, rows_per_mod):
    m, k0 = a0.shape
    k1 = a1.shape[1]
    n = w.shape[1]
    tm = _pick(math.gcd(m, rows_per_mod), 1024, 16)
    tn = _pick(n, 1024, LANES)
    return pl.pallas_call(
        functools.partial(_outproj_kernel, k0=k0),
        grid=(m // tm, n // tn),
        in_specs=[pl.BlockSpec((tm, k0), lambda i, j: (i, 0)),
                  pl.BlockSpec((tm, k1), lambda i, j: (i, 0)),
                  pl.BlockSpec((k0 + k1, tn), lambda i, j: (0, j)),
                  pl.BlockSpec((tm, tn), lambda i, j: (i, j)),
                  pl.BlockSpec((1, 1, tn), lambda i, j: ((i * tm) // rows_per_mod, 0, j))],
        out_specs=pl.BlockSpec((tm, tn), lambda i, j: (i, j)),
        out_shape=jax.ShapeDtypeStruct((m, n), F32),
        compiler_params=_cparams("parallel", "parallel"),
        name="out_proj",
    )(a0, a1, w, res, gate)


def _mmres_kernel(a_ref, w_ref, res_ref, gate_ref, o_ref, acc_ref):
    kk = pl.program_id(2)

    @pl.when(kk == 0)
    def _():
        acc_ref[...] = jnp.zeros_like(acc_ref)

    acc_ref[...] += _dot(a_ref[...], w_ref[...])

    @pl.when(kk == pl.num_programs(2) - 1)
    def _():
        o_ref[...] = res_ref[...] + gate_ref[0] * acc_ref[...]


def _matmul_gated_residual(a, w, res, gate, rows_per_mod):
    m, k = a.shape
    n = w.shape[1]
    tm = _pick(math.gcd(m, rows_per_mod), 1024, 16)
    tn = _pick(n, 1024, LANES)
    tk = _pick(k, 2048, LANES)
    return pl.pallas_call(
        _mmres_kernel,
        grid=(m // tm, n // tn, k // tk),
        in_specs=[pl.BlockSpec((tm, tk), lambda i, j, kk: (i, kk)),
                  pl.BlockSpec((tk, tn), lambda i, j, kk: (kk, j)),
                  pl.BlockSpec((tm, tn), lambda i, j, kk: (i, j)),
                  pl.BlockSpec((1, 1, tn), lambda i, j, kk: ((i * tm) // rows_per_mod, 0, j))],
        out_specs=pl.BlockSpec((tm, tn), lambda i, j, kk: (i, j)),
        out_shape=jax.ShapeDtypeStruct((m, n), F32),
        scratch_shapes=[pltpu.VMEM((tm, tn), F32)],
        compiler_params=_cparams("parallel", "parallel", "arbitrary"),
        name="matmul_gated_residual",
    )(a, w, res, gate)


def _filt_kernel(feat_ref, t_ref, w1_ref, b1_ref, w2_ref, b2_ref, fr_ref, w3_ref, dl_ref, o_ref, h_ref, *, seq):
    i = pl.program_id(0)
    j = pl.program_id(1)

    @pl.when(j == 0)
    def _():
        fr = fr_ref[...]
        h = jnp.sin(fr * (_dot(feat_ref[...].astype(BF16), w1_ref[...]) + b1_ref[...]))
        h = jnp.sin(fr * (_dot(h.astype(BF16), w2_ref[...]) + b2_ref[...]))
        h_ref[...] = h.astype(BF16)

    f = _dot(h_ref[...], w3_ref[...]) * jnp.exp(-t_ref[...] * dl_ref[...])
    rows = lax.broadcasted_iota(jnp.int32, f.shape, 0) + i * f.shape[0]
    o_ref[0] = jnp.where(rows == seq, 0.0, f)


def _hyena_filters(seq, w1, b1, w2, b2, w3, freq, width):
    lag = jnp.arange(2 * seq)
    src = jnp.where(lag < seq, lag, jnp.clip(2 * seq - lag, 0, seq - 1))
    pos = jnp.arange(seq, dtype=F32)
    t = jnp.linspace(0.0, 1.0, seq, dtype=F32)[:, None]
    bands = jnp.linspace(1e-4, HY_POS_BANDS - 1, HY_POS_BANDS, dtype=F32)
    ang = (2.0 * math.pi / seq) * pos[:, None] * bands[None, :]
    feats = jnp.concatenate([t, jnp.cos(ang), -jnp.sin(ang)], axis=-1)
    pd = feats.shape[1]
    hid = w1.shape[1]
    hp = LANES
    feats = jnp.pad(feats, ((0, 0), (0, hp - pd)))[src]
    t = t[src]
    w1p = jnp.pad(w1, ((0, hp - pd), (0, hp - hid))).astype(BF16)
    w2p = jnp.pad(w2, ((0, hp - hid), (0, hp - hid))).astype(BF16)
    w3p = jnp.pad(w3, ((0, hp - hid), (0, 0))).astype(BF16)
    b1p = jnp.pad(b1, (0, hp - hid))[None, :]
    b2p = jnp.pad(b2, (0, hp - hid))[None, :]
    frp = jnp.pad(freq, (0, hp - hid))[None, :]
    deltas = jnp.abs(jnp.linspace(math.log(HY_DECAY_TARGET) / HY_SLOW_DECAY_PCT,
                                  math.log(HY_DECAY_TARGET) / HY_FAST_DECAY_PCT, width, dtype=F32))
    ncol = w3.shape[1] // 2
    dl = jnp.tile(deltas, ncol // width)[None, :]
    tl = _pick(seq, 512, SUBLANES)
    tn = _pick(ncol, 2048, LANES)
    fwd_tiles = seq // tl
    return pl.pallas_call(
        functools.partial(_filt_kernel, seq=seq),
        grid=(2 * seq // tl, ncol // tn),
        in_specs=[pl.BlockSpec((tl, hp), lambda i, j: (i, 0)),
                  pl.BlockSpec((tl, 1), lambda i, j: (i, 0)),
                  pl.BlockSpec((hp, hp), lambda i, j: (0, 0)),
                  pl.BlockSpec((1, hp), lambda i, j: (0, 0)),
                  pl.BlockSpec((hp, hp), lambda i, j: (0, 0)),
                  pl.BlockSpec((1, hp), lambda i, j: (0, 0)),
                  pl.BlockSpec((1, hp), lambda i, j: (0, 0)),
                  pl.BlockSpec((hp, tn), lambda i, j: (0, (i // fwd_tiles) * (ncol // tn) + j)),
                  pl.BlockSpec((1, tn), lambda i, j: (0, j))],
        out_specs=pl.BlockSpec((1, tl, tn), lambda i, j: (0, i, j)),
        out_shape=jax.ShapeDtypeStruct((1, 2 * seq, ncol), F32),
        scratch_shapes=[pltpu.VMEM((tl, hp), BF16)],
        compiler_params=_cparams("parallel", "arbitrary"),
        name="hyena_filters",
    )(feats, t, w1p, b1p, w2p, b2p, frp, w3p, dl)


K1_GROUP = SUBLANES


def _dft_tables(n1, n2):
    n = n1 * n2
    hl = n1 // 2
    k1 = np.arange(n1)[None, :, None]
    l1 = np.arange(hl)[None, None, :]
    l2 = np.arange(n2)[:, None, None]
    ang = -2.0 * np.pi * ((k1 * (n2 * l1 + l2)) % n) / n
    gr, gi = np.cos(ang), np.sin(ang)
    gm = np.concatenate([np.concatenate([gr, -gi], axis=2), np.concatenate([gi, gr], axis=2)], axis=1)
    a2 = -2.0 * np.pi * ((np.arange(n2)[:, None] * np.arange(n2)[None, :]) % n2) / n2
    fr, fi = np.cos(a2), np.sin(a2)
    f2 = np.block([[fr, -fi], [fi, fr]])
    if2 = np.block([[fr, fi], [-fi, fr]])
    ir, ii = np.transpose(gr, (0, 2, 1)) / n, -np.transpose(gi, (0, 2, 1)) / n
    igm = np.concatenate([np.concatenate([ir, -ii], axis=2), np.concatenate([ii, ir], axis=2)], axis=1)
    angf = -2.0 * np.pi * ((k1 * (n2 * np.arange(n1)[None, None, :] + l2)) % n) / n
    gmr = np.concatenate([np.cos(angf), np.sin(angf)], axis=1)
    to = lambda a: jnp.asarray(a.astype(np.float32)).astype(BF16)
    return to(gm), to(f2), to(if2), to(igm), to(gmr)


def _fft_split(seq):
    n = 2 * seq
    n2 = LANES if n % (LANES * 2 * SUBLANES) == 0 and n // LANES >= 2 * SUBLANES else 2 * SUBLANES
    n1 = n // n2
    assert n1 * n2 == n and n1 % (2 * SUBLANES) == 0 and n2 % SUBLANES == 0, (n1, n2)
    return n1, n2


def _fill_pitched(src_ref, p_ref, *, nb, rows, n2, pitch):
    def body(l1, carry):
        for b in range(nb):
            p_ref[b, pl.ds(l1, n2, stride=pitch), :] = src_ref[b, pl.ds(pl.multiple_of(l1 * n2, n2), n2), :]
        return carry

    lax.fori_loop(0, rows, body, 0, unroll=2)


def _stage1_to_yv(p_ref, gm_ref, yv_ref, step, *, nb, n1, l2c, pitch):
    rows = n1 // nb

    def body(t, carry):
        l2 = step * l2c + t
        base = pl.multiple_of(l2 * pitch, SUBLANES)
        rhs = jnp.concatenate([p_ref[b, pl.ds(base, rows), :] for b in range(nb)], axis=0)
        res = _dot(gm_ref[t], rhs.astype(BF16))
        row0 = pl.multiple_of(l2 * K1_GROUP, K1_GROUP)
        for g in range(n1 // K1_GROUP):
            for ri in range(2):
                lo = ri * n1 + g * K1_GROUP
                yv_ref[g * 2 + ri, pl.ds(row0, K1_GROUP), :] = res[lo:lo + K1_GROUP]
        return carry

    lax.fori_loop(0, l2c, body, 0, unroll=8)


def _yv_pair(yv_ref, g, t, n2):
    rows = lambda tt: pl.ds(tt, n2, stride=K1_GROUP)
    re = jnp.concatenate([yv_ref[g * 2, rows(t), :], yv_ref[g * 2, rows(t + 1), :]], axis=1)
    im = jnp.concatenate([yv_ref[g * 2 + 1, rows(t), :], yv_ref[g * 2 + 1, rows(t + 1), :]], axis=1)
    return jnp.concatenate([re, im], axis=0).astype(BF16)


def _filtspec_kernel(filt_ref, gmr_ref, f2_ref, o_ref, yv_ref, p_ref, *, n1, n2, l2c, g2, s1, pitch):
    s = pl.program_id(1)

    @pl.when(s == 0)
    def _():
        _fill_pitched(filt_ref, p_ref, nb=1, rows=n1, n2=n2, pitch=pitch)

    @pl.when(s < s1)
    def _():
        _stage1_to_yv(p_ref, gmr_ref, yv_ref, s, nb=1, n1=n1, l2c=l2c, pitch=pitch)

    @pl.when(s >= s1)
    def _():
        for gg in range(g2):
            g = (s - s1) * g2 + gg
            for t in range(0, K1_GROUP, 2):
                x = _dot(f2_ref[...], _yv_pair(yv_ref, g, t, n2))
                for d in range(2):
                    o_ref[gg * K1_GROUP + t + d, 0] = x[:n2, d * LANES:(d + 1) * LANES].astype(o_ref.dtype)
                    o_ref[gg * K1_GROUP + t + d, 1] = x[n2:, d * LANES:(d + 1) * LANES].astype(o_ref.dtype)


def _filter_spectrum(filt, gmr, f2, n1, n2):
    c = filt.shape[-1]
    ng = n1 // K1_GROUP
    l2c = _pick(n2, 32, 4)
    g2 = _pick(ng, 4, 1)
    s1, s2 = n2 // l2c, ng // g2
    pitch = n1 + SUBLANES
    return pl.pallas_call(
        functools.partial(_filtspec_kernel, n1=n1, n2=n2, l2c=l2c, g2=g2, s1=s1, pitch=pitch),
        grid=(c // LANES, s1 + s2),
        in_specs=[pl.BlockSpec((1, n1 * n2, LANES), lambda cb, s: (0, 0, cb), pipeline_mode=pl.Buffered(1)),
                  pl.BlockSpec((l2c, 2 * n1, n1), lambda cb, s: (jnp.minimum(s, s1 - 1), 0, 0)),
                  pl.BlockSpec((2 * n2, 2 * n2), lambda cb, s: (0, 0))],
        out_specs=pl.BlockSpec((g2 * K1_GROUP, 2, n2, LANES), lambda cb, s: (jnp.maximum(s - s1, 0), 0, 0, cb)),
        out_shape=jax.ShapeDtypeStruct((n1, 2, n2, c), BF16),
        scratch_shapes=[pltpu.VMEM((2 * ng, n2 * K1_GROUP, LANES), F32), pltpu.VMEM((1, n2 * pitch, LANES), F32)],
        compiler_params=_cparams("parallel", "arbitrary"),
        name="filter_spectrum",
    )(filt, gmr, f2)


def _fftconv_kernel(sig_ref, gm_ref, k_ref, f2_ref, if2_ref, igm_ref, gate_ref, bias_ref, o_ref, yv_ref, p_ref,
                    *, n1, n2, l2c, g2, l3, rows4, s1, s2, s3, pitch):
    s = pl.program_id(1)
    hl = n1 // 2
    ng = n1 // K1_GROUP

    @pl.when(s == 0)
    def _():
        _fill_pitched(sig_ref, p_ref, nb=2, rows=hl, n2=n2, pitch=pitch)

    @pl.when(s < s1)
    def _():
        _stage1_to_yv(p_ref, gm_ref, yv_ref, s, nb=2, n1=n1, l2c=l2c, pitch=pitch)

    @pl.when((s >= s1) & (s < s1 + s2))
    def _():
        for gg in range(g2):
            g = (s - s1) * g2 + gg
            rows = lambda tt: pl.ds(tt, n2, stride=K1_GROUP)
            for t in range(0, K1_GROUP, 2):
                x = _dot(f2_ref[...], _yv_pair(yv_ref, g, t, n2))
                xr, xi = x[:n2], x[n2:]
                kt = gg * K1_GROUP + t
                kr = jnp.concatenate([k_ref[kt, 0], k_ref[kt + 1, 0]], axis=1).astype(F32)
                ki = jnp.concatenate([k_ref[kt, 1], k_ref[kt + 1, 1]], axis=1).astype(F32)
                p = jnp.concatenate([xr * kr - xi * ki, xr * ki + xi * kr], axis=0).astype(BF16)
                yp = _dot(if2_ref[...], p)
                for d in range(2):
                    for ri in range(2):
                        yv_ref[g * 2 + ri, rows(t + d), :] = yp[ri * n2:(ri + 1) * n2, d * LANES:(d + 1) * LANES]

    @pl.when((s >= s1 + s2) & (s < s1 + s2 + s3))
    def _():
        def body(i, carry):
            l2 = (s - s1 - s2) * (l3 * SUBLANES) + i
            row0 = pl.multiple_of(l2 * K1_GROUP, K1_GROUP)
            tiles = [yv_ref[g * 2 + ri, pl.ds(row0, K1_GROUP), :] for ri in range(2) for g in range(ng)]
            res = _dot(igm_ref[i], jnp.concatenate(tiles, axis=0).astype(BF16))
            base = pl.multiple_of(l2 * pitch, SUBLANES)
            for b in range(2):
                p_ref[b, pl.ds(base, hl), :] = res[b * hl:(b + 1) * hl]
            return carry

        lax.fori_loop(0, l3 * SUBLANES, body, 0, unroll=8)

    @pl.when(s >= s1 + s2 + s3)
    def _():
        r = s - s1 - s2 - s3
        for j in range(rows4 // n2):
            l1 = r * (rows4 // n2) + j
            nat = pl.ds(pl.multiple_of(l1 * n2, n2), n2)
            for b in range(2):
                conv = p_ref[b, pl.ds(l1, n2, stride=pitch), :] + bias_ref[...] * sig_ref[b, nat, :]
                o_ref[b, j * n2:(j + 1) * n2, :] = (gate_ref[b, j * n2:(j + 1) * n2, :] * conv).astype(o_ref.dtype)


def _fft_conv_gated(sig, kspec, kcol_block0, gate, gate_col_block0, bias, tables, n1, n2, out_dtype):
    gm, f2, if2, igm = tables
    c = bias.shape[1]
    seq = sig.shape[1]
    ng = n1 // K1_GROUP
    l2c = _pick(n2, 32, 4)
    g2 = _pick(ng, 4, 1)
    l3 = _pick(n2 // SUBLANES, 2, 1)
    rows4 = _pick(seq, 2048, n2)
    s1, s2, s3, s4 = n2 // l2c, ng // g2, n2 // (l3 * SUBLANES), seq // rows4
    pitch = n1 // 2 + SUBLANES
    clamp = lambda v, n: jnp.clip(v, 0, n - 1)
    return pl.pallas_call(
        functools.partial(_fftconv_kernel, n1=n1, n2=n2, l2c=l2c, g2=g2, l3=l3, rows4=rows4, s1=s1, s2=s2, s3=s3,
                          pitch=pitch),
        grid=(c // LANES, s1 + s2 + s3 + s4),
        in_specs=[pl.BlockSpec((2, seq, LANES), lambda cb, s: (0, 0, cb), pipeline_mode=pl.Buffered(1)),
                  pl.BlockSpec((l2c, 2 * n1, n1), lambda cb, s: (clamp(s, s1), 0, 0)),
                  pl.BlockSpec((g2 * K1_GROUP, 2, n2, LANES),
                               lambda cb, s: (clamp(s - s1, s2), 0, 0, kcol_block0 + cb)),
                  pl.BlockSpec((2 * n2, 2 * n2), lambda cb, s: (0, 0)),
                  pl.BlockSpec((2 * n2, 2 * n2), lambda cb, s: (0, 0)),
                  pl.BlockSpec((l3 * SUBLANES, n1, 2 * n1), lambda cb, s: (clamp(s - s1 - s2, s3), 0, 0)),
                  pl.BlockSpec((2, rows4, LANES),
                               lambda cb, s: (0, clamp(s - s1 - s2 - s3, s4), gate_col_block0 + cb)),
                  pl.BlockSpec((1, LANES), lambda cb, s: (0, cb))],
        out_specs=pl.BlockSpec((2, rows4, LANES), lambda cb, s: (0, clamp(s - s1 - s2 - s3, s4), cb)),
        out_shape=jax.ShapeDtypeStruct((2, seq, c), out_dtype),
        scratch_shapes=[pltpu.VMEM((2 * ng, n2 * K1_GROUP, LANES), F32), pltpu.VMEM((2, n2 * pitch, LANES), F32)],
        compiler_params=_cparams("parallel", "arbitrary"),
        name="fft_conv_gated",
    )(sig, gm, kspec, f2, if2, igm, gate, bias)


def _hyena(u, fw1, fb1, fw2, fb2, fw3, freq, hy_bias):
    bsz, seq, _ = u.shape
    assert bsz == 2, "the FFT convolution packs exactly two batch elements into one complex signal"
    width = hy_bias.shape[1]
    n1, n2 = _fft_split(seq)
    gm, f2, if2, igm, gmr = _dft_tables(n1, n2)

    filt = _hyena_filters(seq, fw1, fb1, fw2, fb2, fw3, freq, width)
    kspec = _filter_spectrum(filt, gmr, f2, n1, n2)

    tables = (gm, f2, if2, igm)
    wb = width // LANES
    z = _fft_conv_gated(u, kspec, 0, u, wb, hy_bias[0:1], tables, n1, n2, F32)
    return _fft_conv_gated(z, kspec, wb, u, 2 * wb, hy_bias[1:2], tables, n1, n2, BF16)


HEAD_LANES = 2 * LANES


def _rope_tile(w):
    lead = w.shape[:-1]
    t = jnp.swapaxes(w.reshape(lead + (2, 2, MLA_ROPE // 4)), -3, -2).reshape(lead + (2, MLA_ROPE // 2))
    t = jnp.pad(t, [(0, 0)] * (len(lead) + 1) + [(0, LANES // 2 - MLA_ROPE // 2)])
    return t.reshape(lead + (LANES,))


def _head_layout(w, heads, nope):
    r = w.shape[0]
    w3 = w.reshape(r, heads, nope + MLA_ROPE)
    return jnp.concatenate([w3[..., :nope], _rope_tile(w3[..., nope:])], axis=-1).reshape(r, heads * HEAD_LANES)


def _rope_tables(seq):
    rows = seq // GRID_W
    row = jnp.repeat(jnp.arange(rows, dtype=F32), GRID_W)
    col = jnp.tile(jnp.arange(GRID_W, dtype=F32), rows)
    half = MLA_ROPE // 2
    inv = ROPE_THETA ** (-jnp.arange(0, half, 2, dtype=F32) / half)
    ang = jnp.concatenate([row[:, None] * inv, col[:, None] * inv], axis=-1)
    zero = jnp.zeros_like(ang)
    cos = jnp.concatenate([jnp.cos(ang), zero, jnp.cos(ang), zero], axis=-1)
    sin = jnp.concatenate([-jnp.sin(ang), zero, jnp.sin(ang), zero], axis=-1)
    return cos, sin


def _rms(x, g):
    return x * lax.rsqrt(jnp.mean(x * x, axis=-1, keepdims=True) + NORM_EPS) * g


def _q_kernel(qa_ref, g_ref, w_ref, gn_ref, cos_ref, sin_ref, o_ref, *, heads, qk_dim, out_scale):
    xn = _rms(qa_ref[...], g_ref[...]).astype(BF16)
    q = _dot(xn, w_ref[...])
    gn = gn_ref[...]
    cos, sin = cos_ref[...], sin_ref[...]
    for h in range(heads):
        qh = q[:, h * HEAD_LANES:(h + 1) * HEAD_LANES]
        inv = lax.rsqrt(jnp.sum(qh * qh, axis=-1, keepdims=True) / qk_dim + NORM_EPS)
        qn = qh * inv * gn
        r = qn[:, LANES:]
        r = r * cos + pltpu.roll(r, 64, 1) * sin
        o_ref[0, h, :, 0:LANES] = (qn[:, :LANES] * out_scale).astype(BF16)
        o_ref[0, h, :, LANES:] = (r * out_scale).astype(BF16)


def _mla_queries(proj, col_block, g_qa, w_q, gn, cos, sin, heads, qk_dim, out_scale):
    bsz, seq, _ = proj.shape
    r = g_qa.shape[1]
    tm = _pick(seq, 512, 16)
    return pl.pallas_call(
        functools.partial(_q_kernel, heads=heads, qk_dim=qk_dim, out_scale=out_scale),
        grid=(bsz, seq // tm),
        in_specs=[pl.BlockSpec((None, tm, r), lambda b, i: (b, i, col_block)),
                  pl.BlockSpec((1, r), lambda b, i: (0, 0)),
                  pl.BlockSpec((r, heads * HEAD_LANES), lambda b, i: (0, 0)),
                  pl.BlockSpec((1, HEAD_LANES), lambda b, i: (0, 0)),
                  pl.BlockSpec((tm, LANES), lambda b, i: (i, 0)),
                  pl.BlockSpec((tm, LANES), lambda b, i: (i, 0))],
        out_specs=pl.BlockSpec((1, heads, tm, HEAD_LANES), lambda b, i: (b, 0, i, 0)),
        out_shape=jax.ShapeDtypeStruct((bsz, heads, seq, HEAD_LANES), BF16),
        compiler_params=_cparams("parallel", "parallel"),
        name="mla_queries",
    )(proj, g_qa, w_q, gn, cos, sin)


V_ROWS = LANES + 16


def _kv_kernel(kva_ref, kr_ref, g_ref, wk_ref, wvt_ref, gn_ref, *rest, heads, qk_dim, rope):
    if rope:
        cos_ref, sin_ref, k_ref, v_ref = rest
    else:
        k_ref, v_ref = rest
    xn = _rms(kva_ref[...], g_ref[...]).astype(BF16)
    kk = _dot(xn, wk_ref[...])
    vt = lax.dot_general(wvt_ref[...], xn, (((1,), (1,)), ((), ())), preferred_element_type=F32)
    gn = gn_ref[...]
    kr = kr_ref[...]
    ssr = jnp.sum(kr * kr, axis=-1, keepdims=True)
    krg = kr * gn[:, LANES:]
    if rope:
        krg = krg * cos_ref[...] + pltpu.roll(krg, 64, 1) * sin_ref[...]
    ones_row = (lax.broadcasted_iota(jnp.int32, (V_ROWS - LANES, kr.shape[0]), 0) == 0).astype(BF16)
    for h in range(heads):
        kn = kk[:, h * LANES:(h + 1) * LANES]
        inv = lax.rsqrt((jnp.sum(kn * kn, axis=-1, keepdims=True) + ssr) / qk_dim + NORM_EPS)
        k_ref[0, h, :, 0:LANES] = (kn * inv * gn[:, :LANES]).astype(BF16)
        k_ref[0, h, :, LANES:] = (krg * inv).astype(BF16)
        v_ref[0, h, 0:LANES, :] = vt[h * LANES:(h + 1) * LANES].astype(BF16)
        v_ref[0, h, LANES:, :] = ones_row


def _mla_keys_values(proj, kv_block, kr_block, g_kva, w_k, w_vt, gn, rope, heads, qk_dim):
    bsz, seq, _ = proj.shape
    r = g_kva.shape[1]
    tm = _pick(seq, 512, LANES)
    in_specs = [pl.BlockSpec((None, tm, r), lambda b, i: (b, i, kv_block)),
                pl.BlockSpec((None, tm, LANES), lambda b, i: (b, i, kr_block)),
                pl.BlockSpec((1, r), lambda b, i: (0, 0)),
                pl.BlockSpec((r, heads * LANES), lambda b, i: (0, 0)),
                pl.BlockSpec((heads * LANES, r), lambda b, i: (0, 0)),
                pl.BlockSpec((1, HEAD_LANES), lambda b, i: (0, 0))]
    args = [proj, proj, g_kva, w_k, w_vt, gn]
    if rope is not None:
        in_specs += [pl.BlockSpec((tm, LANES), lambda b, i: (i, 0))] * 2
        args += list(rope)
    return pl.pallas_call(
        functools.partial(_kv_kernel, heads=heads, qk_dim=qk_dim, rope=rope is not None),
        grid=(bsz, seq // tm),
        in_specs=in_specs,
        out_specs=[pl.BlockSpec((1, heads, tm, HEAD_LANES), lambda b, i: (b, 0, i, 0)),
                   pl.BlockSpec((1, heads, V_ROWS, tm), lambda b, i: (b, 0, 0, i))],
        out_shape=[jax.ShapeDtypeStruct((bsz, heads, seq, HEAD_LANES), BF16),
                   jax.ShapeDtypeStruct((bsz, heads, V_ROWS, seq), BF16)],
        compiler_params=_cparams("parallel", "parallel"),
        name="mla_keys_values",
    )(*args)


ATTN_HEADS_PER_STEP = 2
ATTN_LOOKAHEAD = 2


def _attn_kernel(q_ref, k_ref, v_ref, kc_ref, vc_ref, o_ref, m_ref, acc_ref, *, hps, kchunk):
    kj = pl.program_id(3)

    def scores(g, k):
        return lax.dot_general(k, q_ref[0, g], (((1,), (1,)), ((), ())), preferred_element_type=F32)

    def update(g, st, vt):
        m_old = m_ref[g]
        m_new = jnp.maximum(m_old, jnp.max(st, axis=0, keepdims=True))
        p = jnp.exp2(st - m_new)
        acc_ref[g] = jnp.exp2(m_old - m_new) * acc_ref[g] + _dot(vt, p.astype(BF16))
        m_ref[g] = m_new

    def sweep(units):
        ahead = [scores(g, k()) for g, k, _ in units[:ATTN_LOOKAHEAD]]
        for i, (g, _, vt) in enumerate(units):
            st = ahead.pop(0)
            if i + ATTN_LOOKAHEAD < len(units):
                nxt = units[i + ATTN_LOOKAHEAD]
                ahead.append(scores(nxt[0], nxt[1]()))
            update(g, st, vt())

    @pl.when(kj == 0)
    def _():
        m_ref[...] = jnp.full_like(m_ref, -jnp.inf)
        acc_ref[...] = jnp.zeros_like(acc_ref)
        sweep([(g, functools.partial(lambda g: kc_ref[0, g], g), functools.partial(lambda g: vc_ref[0, g], g))
               for g in range(hps)])

    tk = k_ref.shape[2]
    sweep([(g,
            functools.partial(lambda g, c: k_ref[0, g, c * kchunk:(c + 1) * kchunk, :], g, c),
            functools.partial(lambda g, c: v_ref[0, g, :, c * kchunk:(c + 1) * kchunk], g, c))
           for c in range(tk // kchunk) for g in range(hps)])

    @pl.when(kj == pl.num_programs(3) - 1)
    def _():
        for g in range(hps):
            acc = acc_ref[g]
            out_t = acc[:LANES] / acc[LANES:LANES + 1]
            o_ref[0, :, g * LANES:(g + 1) * LANES] = out_t.T.astype(o_ref.dtype)


def _attention(q, k, vt, kc, vct, tq_pref, tk_pref, kchunk_pref):
    bsz, heads, seq, _ = q.shape
    lc = kc.shape[2]
    hps = ATTN_HEADS_PER_STEP if heads % ATTN_HEADS_PER_STEP == 0 else 1
    tq = _pick(seq, tq_pref, LANES)
    tk = _pick(seq, tk_pref, LANES)
    kchunk = _pick(tk, kchunk_pref, LANES)
    return pl.pallas_call(
        functools.partial(_attn_kernel, hps=hps, kchunk=kchunk),
        grid=(bsz, heads // hps, seq // tq, seq // tk),
        in_specs=[pl.BlockSpec((1, hps, tq, HEAD_LANES), lambda b, h, i, j: (b, h, i, 0)),
                  pl.BlockSpec((1, hps, tk, HEAD_LANES), lambda b, h, i, j: (b, h, j, 0)),
                  pl.BlockSpec((1, hps, V_ROWS, tk), lambda b, h, i, j: (b, h, 0, j)),
                  pl.BlockSpec((1, hps, lc, HEAD_LANES), lambda b, h, i, j: (b, h, 0, 0)),
                  pl.BlockSpec((1, hps, V_ROWS, lc), lambda b, h, i, j: (b, h, 0, 0))],
        out_specs=pl.BlockSpec((1, tq, hps * LANES), lambda b, h, i, j: (b, i, h)),
        out_shape=jax.ShapeDtypeStruct((bsz, seq, heads * LANES), BF16),
        scratch_shapes=[pltpu.VMEM((hps, 1, tq), F32), pltpu.VMEM((hps, V_ROWS, tq), F32)],
        compiler_params=_cparams("parallel", "parallel", "parallel", "arbitrary"),
        name="attention",
    )(q, k, vt, kc, vct)


def kernel(x, c, ctx, c_ctx, norm1_g, norm2_g, w_ada, b_ada, w_in, hy_conv_w, hy_conv_b, hy_filt_w1, hy_filt_b1, hy_filt_w2, hy_filt_b2, hy_filt_w3, hy_freq, hy_bias, mla_g_qa, mla_w_qb, mla_g_kva, mla_w_kvb, mla_q_norm_g, mla_k_norm_g, w_out, w_mlp1, w_mlp2):
    assert w_ada.shape[0] == 1, "single-layer block"
    bsz, seq, d = x.shape
    lc = ctx.shape[1]
    hyc = hy_conv_b.shape[1]
    width = hy_bias.shape[2]
    q_lora = mla_g_qa.shape[1]
    kv_lora = mla_g_kva.shape[1]
    qk_dim = mla_q_norm_g.shape[1]
    nope = qk_dim - MLA_ROPE
    heads = mla_w_qb.shape[2] // qk_dim
    v_dim = mla_w_kvb.shape[2] // heads - nope
    assert nope == LANES and v_dim == LANES and seq % GRID_W == 0
    q0, kv0, kr0 = hyc, hyc + q_lora, hyc + q_lora + kv_lora
    assert q0 % q_lora == 0 and kv0 % kv_lora == 0 and kr0 % LANES == 0

    cc = jnp.zeros((8, d), F32).at[:bsz].set(c).at[bsz].set(c_ctx)
    mod = _adaln(cc, w_ada[0], b_ada)
    chunk = lambda i: mod[:bsz, i * d:(i + 1) * d][:, None, :]
    sh1, sc1, g1, sh2, sc2, g2 = [chunk(i) for i in range(6)]
    csh1 = mod[bsz:bsz + 1, 0:d][:, None, :]
    csc1 = mod[bsz:bsz + 1, d:2 * d][:, None, :]

    w_in0 = w_in[0]
    np_cols = kr0 + LANES
    np_cols += (-np_cols) % 1024
    w_pad = jnp.concatenate([w_in0[:, :kr0].astype(BF16), _rope_tile(w_in0[:, kr0:]).astype(BF16),
                             jnp.zeros((d, np_cols - kr0 - LANES), BF16)], axis=1)
    w_q = _head_layout(mla_w_qb[0], heads, nope).astype(BF16)
    w_kv = mla_w_kvb[0].reshape(kv_lora, heads, nope + v_dim).astype(BF16)
    w_k = w_kv[..., :nope].reshape(kv_lora, heads * nope)
    w_vt = w_kv[..., nope:].reshape(kv_lora, heads * v_dim).T
    gq = _head_layout(mla_q_norm_g, 1, nope)
    gk = _head_layout(mla_k_norm_g, 1, nope)
    cos, sin = _rope_tables(seq)

    ctx_cols = kv0 - kv0 % 1024
    proj_c = _norm_mod_matmul(ctx.reshape(bsz * lc, d), norm1_g, csh1, csc1, w_pad, bsz * lc, F32, False,
                              512, 1024, col0=ctx_cols).reshape(bsz, lc, np_cols - ctx_cols)
    k_c, v_c = _mla_keys_values(proj_c, (kv0 - ctx_cols) // kv_lora, (kr0 - ctx_cols) // LANES,
                                mla_g_kva, w_k, w_vt, gk, None, heads, qk_dim)

    x2 = x.reshape(bsz * seq, d)
    taps = jnp.zeros((3, np_cols), F32).at[1].set(1.0).at[:, :hyc].set(hy_conv_w[0])
    tap_bias = jnp.zeros((1, np_cols), F32).at[0, :hyc].set(hy_conv_b[0])
    proj = _norm_mod_matmul_conv(x2, norm1_g, sh1, sc1, w_pad, taps, tap_bias, seq, 512, 1024)
    proj = proj.reshape(bsz, seq, np_cols)
    y_hy = _hyena(proj, hy_filt_w1[0], hy_filt_b1[0], hy_filt_w2[0], hy_filt_b2[0], hy_filt_w3[0], hy_freq[0],
                  hy_bias[0])
    q = _mla_queries(proj, q0 // q_lora, mla_g_qa, w_q, gq, cos, sin, heads, qk_dim, qk_dim ** -0.5 * math.log2(math.e))
    k, v = _mla_keys_values(proj, kv0 // kv_lora, kr0 // LANES, mla_g_kva, w_k, w_vt, gk, (cos, sin), heads, qk_dim)
    y_att = _attention(q, k, v, k_c, v_c, 1024, 1024, 512)

    x1 = _out_proj(y_hy.reshape(bsz * seq, width), y_att.reshape(bsz * seq, heads * v_dim),
                   w_out[0].astype(BF16), x2, g1, seq)
    hmid = _norm_mod_matmul(x1, norm2_g, sh2, sc2, w_mlp1[0].astype(BF16), seq, BF16, True, 512, 1024)
    out = _matmul_gated_residual(hmid, w_mlp2[0].astype(BF16), x1, g2, seq)
    return out.reshape(bsz, seq, d)
```

```python
import functools
import math

import numpy as np
import jax
import jax.numpy as jnp
from jax import lax
from jax.experimental import pallas as pl
from jax.experimental.pallas import tpu as pltpu

F32 = jnp.float32
BF16 = jnp.bfloat16

NORM_EPS = 1e-6
GRID_W = 64
MLA_ROPE = 64
ROPE_THETA = 10000.0
HY_POS_BANDS = 16
HY_DECAY_TARGET = 1e-2
HY_FAST_DECAY_PCT = 0.3
HY_SLOW_DECAY_PCT = 1.5

LANES = 128
SUBLANES = 8
VMEM_LIMIT = 56 * 1024 * 1024


def _cparams(*sem):
    return pltpu.CompilerParams(dimension_semantics=sem, vmem_limit_bytes=VMEM_LIMIT)


def _pick(dim, pref, align):
    t = min(pref, dim)
    t -= t % align
    while t >= align:
        if dim % t == 0:
            return t
        t -= align
    return dim


def _dot(a, b):
    return jnp.dot(a, b, preferred_element_type=F32)


def _adaln_kernel(c_ref, w_ref, b_ref, o_ref):
    c = c_ref[...]
    s = c * jax.nn.sigmoid(c)
    o_ref[...] = _dot(s.astype(BF16), w_ref[...].astype(BF16)) + b_ref[...]


def _adaln(cc, w, b):
    d, n = w.shape
    tn = _pick(n, 512, LANES)
    return pl.pallas_call(
        _adaln_kernel,
        grid=(n // tn,),
        in_specs=[pl.BlockSpec((8, d), lambda j: (0, 0)),
                  pl.BlockSpec((d, tn), lambda j: (0, j)),
                  pl.BlockSpec((1, tn), lambda j: (0, j))],
        out_specs=pl.BlockSpec((8, tn), lambda j: (0, j)),
        out_shape=jax.ShapeDtypeStruct((8, n), F32),
        compiler_params=_cparams("parallel"),
        name="adaln",
    )(cc, w, b)


def _nmm_kernel(x_ref, g_ref, sh_ref, sc_ref, w_ref, o_ref, h_ref, *, sq_relu):
    @pl.when(pl.program_id(1) == 0)
    def _():
        x = x_ref[...]
        ms = jnp.mean(x * x, axis=-1, keepdims=True)
        y = x * lax.rsqrt(ms + NORM_EPS) * g_ref[...]
        h_ref[...] = (y * (1.0 + sc_ref[0]) + sh_ref[0]).astype(BF16)

    acc = _dot(h_ref[...], w_ref[...])
    if sq_relu:
        acc = jnp.square(jnp.maximum(acc, 0.0))
    o_ref[...] = acc.astype(o_ref.dtype)


def _norm_mod_matmul(x, g, shift, scale, w, rows_per_mod, out_dtype, sq_relu, tm_pref, tn_pref, col0=0):
    m, k = x.shape
    n = w.shape[1] - col0
    tm = _pick(math.gcd(m, rows_per_mod), tm_pref, 16)
    tn = _pick(math.gcd(n, col0) if col0 else n, tn_pref, LANES)
    mod_idx = lambda i, j: ((i * tm) // rows_per_mod, 0, 0)
    return pl.pallas_call(
        functools.partial(_nmm_kernel, sq_relu=sq_relu),
        grid=(m // tm, n // tn),
        in_specs=[pl.BlockSpec((tm, k), lambda i, j: (i, 0)),
                  pl.BlockSpec((1, k), lambda i, j: (0, 0)),
                  pl.BlockSpec((1, 1, k), mod_idx),
                  pl.BlockSpec((1, 1, k), mod_idx),
                  pl.BlockSpec((k, tn), lambda i, j: (0, col0 // tn + j))],
        out_specs=pl.BlockSpec((tm, tn), lambda i, j: (i, j)),
        out_shape=jax.ShapeDtypeStruct((m, n), out_dtype),
        scratch_shapes=[pltpu.VMEM((tm, k), BF16)],
        compiler_params=_cparams("parallel", "arbitrary"),
        name="norm_mod_matmul",
    )(x, g, shift, scale, w)


HALO = 16


def _nmm_conv_kernel(x_ref, xp_ref, xn_ref, g_ref, sh_ref, sc_ref, w_ref, cw_ref, cb_ref, o_ref, h_ref,
                     *, tm, rows_per_seq):
    i = pl.program_id(0)

    def normed(x):
        ms = jnp.mean(x * x, axis=-1, keepdims=True)
        y = x * lax.rsqrt(ms + NORM_EPS) * g_ref[...]
        return y * (1.0 + sc_ref[0]) + sh_ref[0]

    @pl.when(pl.program_id(1) == 0)
    def _():
        first = (i * tm) % rows_per_seq == 0
        last = ((i + 1) * tm) % rows_per_seq == 0
        h_ref[0:HALO, :] = jnp.where(first, 0.0, normed(xp_ref[...])).astype(BF16)
        h_ref[HALO:HALO + tm, :] = normed(x_ref[...]).astype(BF16)
        h_ref[HALO + tm:, :] = jnp.where(last, 0.0, normed(xn_ref[...])).astype(BF16)

    acc = _dot(h_ref[...], w_ref[...])
    rows = acc.shape[0]
    mid = acc[HALO:HALO + tm]
    up = pltpu.roll(acc, 1, 0)[HALO:HALO + tm]
    dn = pltpu.roll(acc, rows - 1, 0)[HALO:HALO + tm]
    o_ref[...] = up * cw_ref[0:1, :] + mid * cw_ref[1:2, :] + dn * cw_ref[2:3, :] + cb_ref[...]


def _norm_mod_matmul_conv(x, g, shift, scale, w, conv_w, conv_b, rows_per_seq, tm_pref, tn_pref):
    m, k = x.shape
    n = w.shape[1]
    tm = _pick(rows_per_seq, tm_pref, HALO)
    tn = _pick(n, tn_pref, LANES)
    r = tm // HALO
    last = m // HALO - 1
    mod_idx = lambda i, j: ((i * tm) // rows_per_seq, 0, 0)
    return pl.pallas_call(
        functools.partial(_nmm_conv_kernel, tm=tm, rows_per_seq=rows_per_seq),
        grid=(m // tm, n // tn),
        in_specs=[pl.BlockSpec((tm, k), lambda i, j: (i, 0)),
                  pl.BlockSpec((HALO, k), lambda i, j: (jnp.maximum(i * r - 1, 0), 0)),
                  pl.BlockSpec((HALO, k), lambda i, j: (jnp.minimum((i + 1) * r, last), 0)),
                  pl.BlockSpec((1, k), lambda i, j: (0, 0)),
                  pl.BlockSpec((1, 1, k), mod_idx),
                  pl.BlockSpec((1, 1, k), mod_idx),
                  pl.BlockSpec((k, tn), lambda i, j: (0, j)),
                  pl.BlockSpec((3, tn), lambda i, j: (0, j)),
                  pl.BlockSpec((1, tn), lambda i, j: (0, j))],
        out_specs=pl.BlockSpec((tm, tn), lambda i, j: (i, j)),
        out_shape=jax.ShapeDtypeStruct((m, n), F32),
        scratch_shapes=[pltpu.VMEM((tm + 2 * HALO, k), BF16)],
        compiler_params=_cparams("parallel", "arbitrary"),
        name="norm_mod_matmul_conv",
    )(x, x, x, g, shift, scale, w, conv_w, conv_b)


def _outproj_kernel(a0_ref, a1_ref, w_ref, res_ref, gate_ref, o_ref, *, k0):
    acc = _dot(a0_ref[...], w_ref[0:k0, :]) + _dot(a1_ref[...], w_ref[k0:, :])
    o_ref[...] = res_ref[...] + gate_ref[0] * acc


def _out_proj(a0, a1, w, res, gate, rows_per_mod):
    m, k0 = a0.shape
    k1 = a1.shape[1]
    n = w.shape[1]
    tm = _pick(math.gcd(m, rows_per_mod), 1024, 16)
    tn = _pick(n, 1024, LANES)
    return pl.pallas_call(
        functools.partial(_outproj_kernel, k0=k0),
        grid=(m // tm, n // tn),
        in_specs=[pl.BlockSpec((tm, k0), lambda i, j: (i, 0)),
                  pl.BlockSpec((tm, k1), lambda i, j: (i, 0)),
                  pl.BlockSpec((k0 + k1, tn), lambda i, j: (0, j)),
                  pl.BlockSpec((tm, tn), lambda i, j: (i, j)),
                  pl.BlockSpec((1, 1, tn), lambda i, j: ((i * tm) // rows_per_mod, 0, j))],
        out_specs=pl.BlockSpec((tm, tn), lambda i, j: (i, j)),
        out_shape=jax.ShapeDtypeStruct((m, n), F32),
        compiler_params=_cparams("parallel", "parallel"),
        name="out_proj",
    )(a0, a1, w, res, gate)


def _mmres_kernel(a_ref, w_ref, res_ref, gate_ref, o_ref, acc_ref):
    kk = pl.program_id(2)

    @pl.when(kk == 0)
    def _():
        acc_ref[...] = jnp.zeros_like(acc_ref)

    acc_ref[...] += _dot(a_ref[...], w_ref[...])

    @pl.when(kk == pl.num_programs(2) - 1)
    def _():
        o_ref[...] = res_ref[...] + gate_ref[0] * acc_ref[...]


def _matmul_gated_residual(a, w, res, gate, rows_per_mod):
    m, k = a.shape
    n = w.shape[1]
    tm = _pick(math.gcd(m, rows_per_mod), 1024, 16)
    tn = _pick(n, 1024, LANES)
    tk = _pick(k, 2048, LANES)
    return pl.pallas_call(
        _mmres_kernel,
        grid=(m // tm, n // tn, k // tk),
        in_specs=[pl.BlockSpec((tm, tk), lambda i, j, kk: (i, kk)),
                  pl.BlockSpec((tk, tn), lambda i, j, kk: (kk, j)),
                  pl.BlockSpec((tm, tn), lambda i, j, kk: (i, j)),
                  pl.BlockSpec((1, 1, tn), lambda i, j, kk: ((i * tm) // rows_per_mod, 0, j))],
        out_specs=pl.BlockSpec((tm, tn), lambda i, j, kk: (i, j)),
        out_shape=jax.ShapeDtypeStruct((m, n), F32),
        scratch_shapes=[pltpu.VMEM((tm, tn), F32)],
        compiler_params=_cparams("parallel", "parallel", "arbitrary"),
        name="matmul_gated_residual",
    )(a, w, res, gate)


def _filt_kernel(feat_ref, t_ref, w1_ref, b1_ref, w2_ref, b2_ref, fr_ref, w3_ref, dl_ref, o_ref, h_ref, *, seq):
    i = pl.program_id(0)
    j = pl.program_id(1)

    @pl.when(j == 0)
    def _():
        fr = fr_ref[...]
        h = jnp.sin(fr * (_dot(feat_ref[...].astype(BF16), w1_ref[...]) + b1_ref[...]))
        h = jnp.sin(fr * (_dot(h.astype(BF16), w2_ref[...]) + b2_ref[...]))
        h_ref[...] = h.astype(BF16)

    f = _dot(h_ref[...], w3_ref[...]) * jnp.exp(-t_ref[...] * dl_ref[...])
    rows = lax.broadcasted_iota(jnp.int32, f.shape, 0) + i * f.shape[0]
    o_ref[0] = jnp.where(rows == seq, 0.0, f)


def _hyena_filters(seq, w1, b1, w2, b2, w3, freq, width):
    two_sided = lambda a: jnp.concatenate([a, a[:1], jnp.flip(a[1:], axis=0)], axis=0)
    pos = jnp.arange(seq, dtype=F32)
    t = jnp.linspace(0.0, 1.0, seq, dtype=F32)[:, None]
    bands = jnp.linspace(1e-4, HY_POS_BANDS - 1, HY_POS_BANDS, dtype=F32)
    ang = (2.0 * math.pi / seq) * pos[:, None] * bands[None, :]
    feats = jnp.concatenate([t, jnp.cos(ang), -jnp.sin(ang)], axis=-1)
    pd = feats.shape[1]
    hid = w1.shape[1]
    hp = LANES
    feats = two_sided(jnp.pad(feats, ((0, 0), (0, hp - pd))))
    t = two_sided(t)
    w1p = jnp.pad(w1, ((0, hp - pd), (0, hp - hid))).astype(BF16)
    w2p = jnp.pad(w2, ((0, hp - hid), (0, hp - hid))).astype(BF16)
    w3p = jnp.pad(w3, ((0, hp - hid), (0, 0))).astype(BF16)
    b1p = jnp.pad(b1, (0, hp - hid))[None, :]
    b2p = jnp.pad(b2, (0, hp - hid))[None, :]
    frp = jnp.pad(freq, (0, hp - hid))[None, :]
    deltas = jnp.abs(jnp.linspace(math.log(HY_DECAY_TARGET) / HY_SLOW_DECAY_PCT,
                                  math.log(HY_DECAY_TARGET) / HY_FAST_DECAY_PCT, width, dtype=F32))
    ncol = w3.shape[1] // 2
    dl = jnp.tile(deltas, ncol // width)[None, :]
    tl = _pick(seq, 512, SUBLANES)
    tn = _pick(ncol, 2048, LANES)
    fwd_tiles = seq // tl
    return pl.pallas_call(
        functools.partial(_filt_kernel, seq=seq),
        grid=(2 * seq // tl, ncol // tn),
        in_specs=[pl.BlockSpec((tl, hp), lambda i, j: (i, 0)),
                  pl.BlockSpec((tl, 1), lambda i, j: (i, 0)),
                  pl.BlockSpec((hp, hp), lambda i, j: (0, 0)),
                  pl.BlockSpec((1, hp), lambda i, j: (0, 0)),
                  pl.BlockSpec((hp, hp), lambda i, j: (0, 0)),
                  pl.BlockSpec((1, hp), lambda i, j: (0, 0)),
                  pl.BlockSpec((1, hp), lambda i, j: (0, 0)),
                  pl.BlockSpec((hp, tn), lambda i, j: (0, (i // fwd_tiles) * (ncol // tn) + j)),
                  pl.BlockSpec((1, tn), lambda i, j: (0, j))],
        out_specs=pl.BlockSpec((1, tl, tn), lambda i, j: (0, i, j)),
        out_shape=jax.ShapeDtypeStruct((1, 2 * seq, ncol), F32),
        scratch_shapes=[pltpu.VMEM((tl, hp), BF16)],
        compiler_params=_cparams("parallel", "arbitrary"),
        name="hyena_filters",
    )(feats, t, w1p, b1p, w2p, b2p, frp, w3p, dl)


K1_GROUP = SUBLANES


def _dft_tables(n1, n2):
    n = n1 * n2
    hl = n1 // 2
    k1 = np.arange(n1)[None, :, None]
    l1 = np.arange(hl)[None, None, :]
    l2 = np.arange(n2)[:, None, None]
    ang = -2.0 * np.pi * ((k1 * (n2 * l1 + l2)) % n) / n
    gr, gi = np.cos(ang), np.sin(ang)
    gm = np.concatenate([np.concatenate([gr, -gi], axis=2), np.concatenate([gi, gr], axis=2)], axis=1)
    a2 = -2.0 * np.pi * ((np.arange(n2)[:, None] * np.arange(n2)[None, :]) % n2) / n2
    fr, fi = np.cos(a2), np.sin(a2)
    f2 = np.block([[fr, -fi], [fi, fr]])
    if2 = np.block([[fr, fi], [-fi, fr]])
    ir, ii = np.transpose(gr, (0, 2, 1)) / n, -np.transpose(gi, (0, 2, 1)) / n
    igm = np.concatenate([np.concatenate([ir, -ii], axis=2), np.concatenate([ii, ir], axis=2)], axis=1)
    angf = -2.0 * np.pi * ((k1 * (n2 * np.arange(n1)[None, None, :] + l2)) % n) / n
    gmr = np.concatenate([np.cos(angf), np.sin(angf)], axis=1)
    to = lambda a: jnp.asarray(a.astype(np.float32)).astype(BF16)
    return to(gm), to(f2), to(if2), to(igm), to(gmr)


def _fft_split(seq):
    n = 2 * seq
    n2 = LANES if n % (LANES * 2 * SUBLANES) == 0 and n // LANES >= 2 * SUBLANES else 2 * SUBLANES
    n1 = n // n2
    assert n1 * n2 == n and n1 % (2 * SUBLANES) == 0 and n2 % SUBLANES == 0, (n1, n2)
    return n1, n2


def _fill_pitched(src_ref, p_ref, *, nb, rows, n2, pitch):
    def body(l1, carry):
        for b in range(nb):
            p_ref[b, pl.ds(l1, n2, stride=pitch), :] = src_ref[b, pl.ds(pl.multiple_of(l1 * n2, n2), n2), :]
        return carry

    lax.fori_loop(0, rows, body, 0, unroll=2)


def _stage1_to_yv(p_ref, gm_ref, yv_ref, step, *, nb, n1, l2c, pitch):
    rows = n1 // nb

    def body(t, carry):
        l2 = step * l2c + t
        base = pl.multiple_of(l2 * pitch, SUBLANES)
        rhs = jnp.concatenate([p_ref[b, pl.ds(base, rows), :] for b in range(nb)], axis=0)
        res = _dot(gm_ref[t], rhs.astype(BF16))
        row0 = pl.multiple_of(l2 * K1_GROUP, K1_GROUP)
        for g in range(n1 // K1_GROUP):
            for ri in range(2):
                lo = ri * n1 + g * K1_GROUP
                yv_ref[g * 2 + ri, pl.ds(row0, K1_GROUP), :] = res[lo:lo + K1_GROUP]
        return carry

    lax.fori_loop(0, l2c, body, 0, unroll=8)


def _yv_pair(yv_ref, g, t, n2):
    rows = lambda tt: pl.ds(tt, n2, stride=K1_GROUP)
    re = jnp.concatenate([yv_ref[g * 2, rows(t), :], yv_ref[g * 2, rows(t + 1), :]], axis=1)
    im = jnp.concatenate([yv_ref[g * 2 + 1, rows(t), :], yv_ref[g * 2 + 1, rows(t + 1), :]], axis=1)
    return jnp.concatenate([re, im], axis=0).astype(BF16)


def _filtspec_kernel(filt_ref, gmr_ref, f2_ref, o_ref, yv_ref, p_ref, *, n1, n2, l2c, g2, s1, pitch):
    s = pl.program_id(1)

    @pl.when(s == 0)
    def _():
        _fill_pitched(filt_ref, p_ref, nb=1, rows=n1, n2=n2, pitch=pitch)

    @pl.when(s < s1)
    def _():
        _stage1_to_yv(p_ref, gmr_ref, yv_ref, s, nb=1, n1=n1, l2c=l2c, pitch=pitch)

    @pl.when(s >= s1)
    def _():
        for gg in range(g2):
            g = (s - s1) * g2 + gg
            for t in range(0, K1_GROUP, 2):
                x = _dot(f2_ref[...], _yv_pair(yv_ref, g, t, n2))
                for d in range(2):
                    o_ref[gg * K1_GROUP + t + d, 0] = x[:n2, d * LANES:(d + 1) * LANES].astype(o_ref.dtype)
                    o_ref[gg * K1_GROUP + t + d, 1] = x[n2:, d * LANES:(d + 1) * LANES].astype(o_ref.dtype)


def _filter_spectrum(filt, gmr, f2, n1, n2):
    c = filt.shape[-1]
    ng = n1 // K1_GROUP
    l2c = _pick(n2, 32, 4)
    g2 = _pick(ng, 4, 1)
    s1, s2 = n2 // l2c, ng // g2
    pitch = n1 + SUBLANES
    return pl.pallas_call(
        functools.partial(_filtspec_kernel, n1=n1, n2=n2, l2c=l2c, g2=g2, s1=s1, pitch=pitch),
        grid=(c // LANES, s1 + s2),
        in_specs=[pl.BlockSpec((1, n1 * n2, LANES), lambda cb, s: (0, 0, cb), pipeline_mode=pl.Buffered(1)),
                  pl.BlockSpec((l2c, 2 * n1, n1), lambda cb, s: (jnp.minimum(s, s1 - 1), 0, 0)),
                  pl.BlockSpec((2 * n2, 2 * n2), lambda cb, s: (0, 0))],
        out_specs=pl.BlockSpec((g2 * K1_GROUP, 2, n2, LANES), lambda cb, s: (jnp.maximum(s - s1, 0), 0, 0, cb)),
        out_shape=jax.ShapeDtypeStruct((n1, 2, n2, c), BF16),
        scratch_shapes=[pltpu.VMEM((2 * ng, n2 * K1_GROUP, LANES), F32), pltpu.VMEM((1, n2 * pitch, LANES), F32)],
        compiler_params=_cparams("parallel", "arbitrary"),
        name="filter_spectrum",
    )(filt, gmr, f2)


def _fftconv_kernel(sig_ref, gm_ref, k_ref, f2_ref, if2_ref, igm_ref, gate_ref, bias_ref, o_ref, yv_ref, p_ref,
                    *, n1, n2, l2c, g2, l3, rows4, s1, s2, s3, pitch):
    s = pl.program_id(1)
    hl = n1 // 2
    ng = n1 // K1_GROUP

    @pl.when(s == 0)
    def _():
        _fill_pitched(sig_ref, p_ref, nb=2, rows=hl, n2=n2, pitch=pitch)

    @pl.when(s < s1)
    def _():
        _stage1_to_yv(p_ref, gm_ref, yv_ref, s, nb=2, n1=n1, l2c=l2c, pitch=pitch)

    @pl.when((s >= s1) & (s < s1 + s2))
    def _():
        for gg in range(g2):
            g = (s - s1) * g2 + gg
            rows = lambda tt: pl.ds(tt, n2, stride=K1_GROUP)
            for t in range(0, K1_GROUP, 2):
                x = _dot(f2_ref[...], _yv_pair(yv_ref, g, t, n2))
                xr, xi = x[:n2], x[n2:]
                kt = gg * K1_GROUP + t
                kr = jnp.concatenate([k_ref[kt, 0], k_ref[kt + 1, 0]], axis=1).astype(F32)
                ki = jnp.concatenate([k_ref[kt, 1], k_ref[kt + 1, 1]], axis=1).astype(F32)
                p = jnp.concatenate([xr * kr - xi * ki, xr * ki + xi * kr], axis=0).astype(BF16)
                yp = _dot(if2_ref[...], p)
                for d in range(2):
                    for ri in range(2):
                        yv_ref[g * 2 + ri, rows(t + d), :] = yp[ri * n2:(ri + 1) * n2, d * LANES:(d + 1) * LANES]

    @pl.when((s >= s1 + s2) & (s < s1 + s2 + s3))
    def _():
        def body(i, carry):
            l2 = (s - s1 - s2) * (l3 * SUBLANES) + i
            row0 = pl.multiple_of(l2 * K1_GROUP, K1_GROUP)
            tiles = [yv_ref[g * 2 + ri, pl.ds(row0, K1_GROUP), :] for ri in range(2) for g in range(ng)]
            res = _dot(igm_ref[i], jnp.concatenate(tiles, axis=0).astype(BF16))
            base = pl.multiple_of(l2 * pitch, SUBLANES)
            for b in range(2):
                p_ref[b, pl.ds(base, hl), :] = res[b * hl:(b + 1) * hl]
            return carry

        lax.fori_loop(0, l3 * SUBLANES, body, 0, unroll=8)

    @pl.when(s >= s1 + s2 + s3)
    def _():
        r = s - s1 - s2 - s3
        for j in range(rows4 // n2):
            l1 = r * (rows4 // n2) + j
            nat = pl.ds(pl.multiple_of(l1 * n2, n2), n2)
            for b in range(2):
                conv = p_ref[b, pl.ds(l1, n2, stride=pitch), :] + bias_ref[...] * sig_ref[b, nat, :]
                o_ref[b, j * n2:(j + 1) * n2, :] = (gate_ref[b, j * n2:(j + 1) * n2, :] * conv).astype(o_ref.dtype)


def _fft_conv_gated(sig, kspec, kcol_block0, gate, gate_col_block0, bias, tables, n1, n2, out_dtype):
    gm, f2, if2, igm = tables
    c = bias.shape[1]
    seq = sig.shape[1]
    ng = n1 // K1_GROUP
    l2c = _pick(n2, 32, 4)
    g2 = _pick(ng, 4, 1)
    l3 = _pick(n2 // SUBLANES, 4, 1)
    rows4 = _pick(seq, 2048, n2)
    s1, s2, s3, s4 = n2 // l2c, ng // g2, n2 // (l3 * SUBLANES), seq // rows4
    pitch = n1 // 2 + SUBLANES
    clamp = lambda v, n: jnp.clip(v, 0, n - 1)
    return pl.pallas_call(
        functools.partial(_fftconv_kernel, n1=n1, n2=n2, l2c=l2c, g2=g2, l3=l3, rows4=rows4, s1=s1, s2=s2, s3=s3,
                          pitch=pitch),
        grid=(c // LANES, s1 + s2 + s3 + s4),
        in_specs=[pl.BlockSpec((2, seq, LANES), lambda cb, s: (0, 0, cb), pipeline_mode=pl.Buffered(1)),
                  pl.BlockSpec((l2c, 2 * n1, n1), lambda cb, s: (clamp(s, s1), 0, 0)),
                  pl.BlockSpec((g2 * K1_GROUP, 2, n2, LANES),
                               lambda cb, s: (clamp(s - s1, s2), 0, 0, kcol_block0 + cb)),
                  pl.BlockSpec((2 * n2, 2 * n2), lambda cb, s: (0, 0)),
                  pl.BlockSpec((2 * n2, 2 * n2), lambda cb, s: (0, 0)),
                  pl.BlockSpec((l3 * SUBLANES, n1, 2 * n1), lambda cb, s: (clamp(s - s1 - s2, s3), 0, 0)),
                  pl.BlockSpec((2, rows4, LANES),
                               lambda cb, s: (0, clamp(s - s1 - s2 - s3, s4), gate_col_block0 + cb)),
                  pl.BlockSpec((1, LANES), lambda cb, s: (0, cb))],
        out_specs=pl.BlockSpec((2, rows4, LANES), lambda cb, s: (0, clamp(s - s1 - s2 - s3, s4), cb)),
        out_shape=jax.ShapeDtypeStruct((2, seq, c), out_dtype),
        scratch_shapes=[pltpu.VMEM((2 * ng, n2 * K1_GROUP, LANES), F32), pltpu.VMEM((2, n2 * pitch, LANES), F32)],
        compiler_params=_cparams("parallel", "arbitrary"),
        name="fft_conv_gated",
    )(sig, gm, kspec, f2, if2, igm, gate, bias)


def _hyena(u, fw1, fb1, fw2, fb2, fw3, freq, hy_bias):
    bsz, seq, _ = u.shape
    assert bsz == 2, "the FFT convolution packs exactly two batch elements into one complex signal"
    width = hy_bias.shape[1]
    n1, n2 = _fft_split(seq)
    gm, f2, if2, igm, gmr = _dft_tables(n1, n2)

    filt = _hyena_filters(seq, fw1, fb1, fw2, fb2, fw3, freq, width)
    kspec = _filter_spectrum(filt, gmr, f2, n1, n2)

    tables = (gm, f2, if2, igm)
    wb = width // LANES
    z = _fft_conv_gated(u, kspec, 0, u, wb, hy_bias[0:1], tables, n1, n2, F32)
    return _fft_conv_gated(z, kspec, wb, u, 2 * wb, hy_bias[1:2], tables, n1, n2, BF16)


HEAD_LANES = 2 * LANES


ROPE_NF = MLA_ROPE // 4


def _head_layout(w, heads, nope):
    r = w.shape[0]
    w3 = jnp.pad(w.reshape(r, heads, nope + MLA_ROPE), ((0, 0), (0, 0), (0, HEAD_LANES - nope - MLA_ROPE)))
    return w3.reshape(r, heads * HEAD_LANES)


def _rope_tables(seq):
    rows = seq // GRID_W
    row = jnp.repeat(jnp.arange(rows, dtype=F32), GRID_W)
    col = jnp.tile(jnp.arange(GRID_W, dtype=F32), rows)
    half = MLA_ROPE // 2
    inv = ROPE_THETA ** (-jnp.arange(0, half, 2, dtype=F32) / half)
    zero = jnp.zeros((seq, ROPE_NF), F32)
    cos, sin_lo, sin_hi = [], [], []
    for pos in (row, col):
        ang = pos[:, None] * inv
        cos += [jnp.cos(ang), jnp.cos(ang)]
        sin_lo += [-jnp.sin(ang), zero]
        sin_hi += [zero, jnp.sin(ang)]
    pad = [jnp.zeros((seq, LANES - MLA_ROPE), F32)]
    return tuple(jnp.concatenate(t + pad, axis=-1) for t in (cos, sin_lo, sin_hi))


def _rope(t, cos, sin_lo, sin_hi):
    return t * cos + pltpu.roll(t, LANES - ROPE_NF, 1) * sin_lo + pltpu.roll(t, ROPE_NF, 1) * sin_hi


def _rms(x, g):
    return x * lax.rsqrt(jnp.mean(x * x, axis=-1, keepdims=True) + NORM_EPS) * g


def _q_kernel(qa_ref, g_ref, w_ref, gn_ref, cos_ref, slo_ref, shi_ref, o_ref, *, heads, qk_dim, out_scale):
    xn = _rms(qa_ref[...], g_ref[...]).astype(BF16)
    q = _dot(xn, w_ref[...])
    gn = gn_ref[...]
    cos, slo, shi = cos_ref[...], slo_ref[...], shi_ref[...]
    for h in range(heads):
        qh = q[:, h * HEAD_LANES:(h + 1) * HEAD_LANES]
        inv = lax.rsqrt(jnp.sum(qh * qh, axis=-1, keepdims=True) / qk_dim + NORM_EPS)
        qn = qh * inv * gn
        o_ref[0, h, :, 0:LANES] = (qn[:, :LANES] * out_scale).astype(BF16)
        o_ref[0, h, :, LANES:] = (_rope(qn[:, LANES:], cos, slo, shi) * out_scale).astype(BF16)


def _mla_queries(proj, col_block, g_qa, w_q, gn, rope, heads, qk_dim, out_scale):
    bsz, seq, _ = proj.shape
    r = g_qa.shape[1]
    tm = _pick(seq, 512, 16)
    return pl.pallas_call(
        functools.partial(_q_kernel, heads=heads, qk_dim=qk_dim, out_scale=out_scale),
        grid=(bsz, seq // tm),
        in_specs=[pl.BlockSpec((None, tm, r), lambda b, i: (b, i, col_block)),
                  pl.BlockSpec((1, r), lambda b, i: (0, 0)),
                  pl.BlockSpec((r, heads * HEAD_LANES), lambda b, i: (0, 0)),
                  pl.BlockSpec((1, HEAD_LANES), lambda b, i: (0, 0))]
                 + [pl.BlockSpec((tm, LANES), lambda b, i: (i, 0))] * 3,
        out_specs=pl.BlockSpec((1, heads, tm, HEAD_LANES), lambda b, i: (b, 0, i, 0)),
        out_shape=jax.ShapeDtypeStruct((bsz, heads, seq, HEAD_LANES), BF16),
        compiler_params=_cparams("parallel", "parallel"),
        name="mla_queries",
    )(proj, g_qa, w_q, gn, *rope)


V_ROWS = LANES + 16


def _kv_kernel(kva_ref, kr_ref, g_ref, wk_ref, wvt_ref, gn_ref, *rest, heads, qk_dim, rope):
    if rope:
        cos_ref, slo_ref, shi_ref, k_ref, v_ref = rest
    else:
        k_ref, v_ref = rest
    xn = _rms(kva_ref[...], g_ref[...]).astype(BF16)
    kk = _dot(xn, wk_ref[...])
    vt = lax.dot_general(wvt_ref[...], xn, (((1,), (1,)), ((), ())), preferred_element_type=F32)
    gn = gn_ref[...]
    kr = kr_ref[...]
    ssr = jnp.sum(kr * kr, axis=-1, keepdims=True)
    krg = kr * gn[:, LANES:]
    if rope:
        krg = _rope(krg, cos_ref[...], slo_ref[...], shi_ref[...])
    ones_row = (lax.broadcasted_iota(jnp.int32, (V_ROWS - LANES, kr.shape[0]), 0) == 0).astype(BF16)
    for h in range(heads):
        kn = kk[:, h * LANES:(h + 1) * LANES]
        inv = lax.rsqrt((jnp.sum(kn * kn, axis=-1, keepdims=True) + ssr) / qk_dim + NORM_EPS)
        k_ref[0, h, :, 0:LANES] = (kn * inv * gn[:, :LANES]).astype(BF16)
        k_ref[0, h, :, LANES:] = (krg * inv).astype(BF16)
        v_ref[0, h, 0:LANES, :] = vt[h * LANES:(h + 1) * LANES].astype(BF16)
        v_ref[0, h, LANES:, :] = ones_row


def _mla_keys_values(proj, kv_block, kr_block, g_kva, w_k, w_vt, gn, rope, heads, qk_dim):
    bsz, seq, _ = proj.shape
    r = g_kva.shape[1]
    tm = _pick(seq, 512, LANES)
    in_specs = [pl.BlockSpec((None, tm, r), lambda b, i: (b, i, kv_block)),
                pl.BlockSpec((None, tm, LANES), lambda b, i: (b, i, kr_block)),
                pl.BlockSpec((1, r), lambda b, i: (0, 0)),
                pl.BlockSpec((r, heads * LANES), lambda b, i: (0, 0)),
                pl.BlockSpec((heads * LANES, r), lambda b, i: (0, 0)),
                pl.BlockSpec((1, HEAD_LANES), lambda b, i: (0, 0))]
    args = [proj, proj, g_kva, w_k, w_vt, gn]
    if rope is not None:
        in_specs += [pl.BlockSpec((tm, LANES), lambda b, i: (i, 0))] * 3
        args += list(rope)
    return pl.pallas_call(
        functools.partial(_kv_kernel, heads=heads, qk_dim=qk_dim, rope=rope is not None),
        grid=(bsz, seq // tm),
        in_specs=in_specs,
        out_specs=[pl.BlockSpec((1, heads, tm, HEAD_LANES), lambda b, i: (b, 0, i, 0)),
                   pl.BlockSpec((1, heads, V_ROWS, tm), lambda b, i: (b, 0, 0, i))],
        out_shape=[jax.ShapeDtypeStruct((bsz, heads, seq, HEAD_LANES), BF16),
                   jax.ShapeDtypeStruct((bsz, heads, V_ROWS, seq), BF16)],
        compiler_params=_cparams("parallel", "parallel"),
        name="mla_keys_values",
    )(*args)


ATTN_HEADS_PER_STEP = 2
ATTN_LOOKAHEAD = 2


def _attn_kernel(q_ref, k_ref, v_ref, kc_ref, vc_ref, o_ref, m_ref, acc_ref, *, hps, kchunk):
    kj = pl.program_id(3)

    def scores(g, k):
        return lax.dot_general(k, q_ref[0, g], (((1,), (1,)), ((), ())), preferred_element_type=F32)

    def update(g, st, vt):
        m_old = m_ref[g]
        m_new = jnp.maximum(m_old, jnp.max(st, axis=0, keepdims=True))
        p = jnp.exp2(st - m_new)
        acc_ref[g] = jnp.exp2(m_old - m_new) * acc_ref[g] + _dot(vt, p.astype(BF16))
        m_ref[g] = m_new

    def sweep(units):
        ahead = [scores(g, k()) for g, k, _ in units[:ATTN_LOOKAHEAD]]
        for i, (g, _, vt) in enumerate(units):
            st = ahead.pop(0)
            if i + ATTN_LOOKAHEAD < len(units):
                nxt = units[i + ATTN_LOOKAHEAD]
                ahead.append(scores(nxt[0], nxt[1]()))
            update(g, st, vt())

    @pl.when(kj == 0)
    def _():
        m_ref[...] = jnp.full_like(m_ref, -jnp.inf)
        acc_ref[...] = jnp.zeros_like(acc_ref)
        sweep([(g, functools.partial(lambda g: kc_ref[0, g], g), functools.partial(lambda g: vc_ref[0, g], g))
               for g in range(hps)])

    tk = k_ref.shape[2]
    sweep([(g,
            functools.partial(lambda g, c: k_ref[0, g, c * kchunk:(c + 1) * kchunk, :], g, c),
            functools.partial(lambda g, c: v_ref[0, g, :, c * kchunk:(c + 1) * kchunk], g, c))
           for c in range(tk // kchunk) for g in range(hps)])

    @pl.when(kj == pl.num_programs(3) - 1)
    def _():
        for g in range(hps):
            acc = acc_ref[g]
            out_t = acc[:LANES] / acc[LANES:LANES + 1]
            o_ref[0, :, g * LANES:(g + 1) * LANES] = out_t.T.astype(o_ref.dtype)


def _attention(q, k, vt, kc, vct, tq_pref, tk_pref, kchunk_pref):
    bsz, heads, seq, _ = q.shape
    lc = kc.shape[2]
    hps = ATTN_HEADS_PER_STEP if heads % ATTN_HEADS_PER_STEP == 0 else 1
    tq = _pick(seq, tq_pref, LANES)
    tk = _pick(seq, tk_pref, LANES)
    kchunk = _pick(tk, kchunk_pref, LANES)
    return pl.pallas_call(
        functools.partial(_attn_kernel, hps=hps, kchunk=kchunk),
        grid=(bsz, heads // hps, seq // tq, seq // tk),
        in_specs=[pl.BlockSpec((1, hps, tq, HEAD_LANES), lambda b, h, i, j: (b, h, i, 0)),
                  pl.BlockSpec((1, hps, tk, HEAD_LANES), lambda b, h, i, j: (b, h, j, 0)),
                  pl.BlockSpec((1, hps, V_ROWS, tk), lambda b, h, i, j: (b, h, 0, j)),
                  pl.BlockSpec((1, hps, lc, HEAD_LANES), lambda b, h, i, j: (b, h, 0, 0)),
                  pl.BlockSpec((1, hps, V_ROWS, lc), lambda b, h, i, j: (b, h, 0, 0))],
        out_specs=pl.BlockSpec((1, tq, hps * LANES), lambda b, h, i, j: (b, i, h)),
        out_shape=jax.ShapeDtypeStruct((bsz, seq, heads * LANES), BF16),
        scratch_shapes=[pltpu.VMEM((hps, 1, tq), F32), pltpu.VMEM((hps, V_ROWS, tq), F32)],
        compiler_params=_cparams("parallel", "parallel", "parallel", "arbitrary"),
        name="attention",
    )(q, k, vt, kc, vct)


def kernel(x, c, ctx, c_ctx, norm1_g, norm2_g, w_ada, b_ada, w_in, hy_conv_w, hy_conv_b, hy_filt_w1, hy_filt_b1, hy_filt_w2, hy_filt_b2, hy_filt_w3, hy_freq, hy_bias, mla_g_qa, mla_w_qb, mla_g_kva, mla_w_kvb, mla_q_norm_g, mla_k_norm_g, w_out, w_mlp1, w_mlp2):
    assert w_ada.shape[0] == 1, "single-layer block"
    bsz, seq, d = x.shape
    lc = ctx.shape[1]
    hyc = hy_conv_b.shape[1]
    width = hy_bias.shape[2]
    q_lora = mla_g_qa.shape[1]
    kv_lora = mla_g_kva.shape[1]
    qk_dim = mla_q_norm_g.shape[1]
    nope = qk_dim - MLA_ROPE
    heads = mla_w_qb.shape[2] // qk_dim
    v_dim = mla_w_kvb.shape[2] // heads - nope
    assert nope == LANES and v_dim == LANES and seq % GRID_W == 0
    q0, kv0, kr0 = hyc, hyc + q_lora, hyc + q_lora + kv_lora
    assert q0 % q_lora == 0 and kv0 % kv_lora == 0 and kr0 % LANES == 0

    cc = jnp.zeros((8, d), F32).at[:bsz].set(c).at[bsz].set(c_ctx)
    mod = _adaln(cc, w_ada[0], b_ada)
    chunk = lambda i: mod[:bsz, i * d:(i + 1) * d][:, None, :]
    sh1, sc1, g1, sh2, sc2, g2 = [chunk(i) for i in range(6)]
    csh1 = mod[bsz:bsz + 1, 0:d][:, None, :]
    csc1 = mod[bsz:bsz + 1, d:2 * d][:, None, :]

    w_in0 = w_in[0]
    np_cols = kr0 + LANES
    np_cols += (-np_cols) % 1024
    w_pad = jnp.pad(w_in0, ((0, 0), (0, np_cols - w_in0.shape[1]))).astype(BF16)
    w_q = _head_layout(mla_w_qb[0], heads, nope).astype(BF16)
    w_kv = mla_w_kvb[0].reshape(kv_lora, heads, nope + v_dim).astype(BF16)
    w_k = w_kv[..., :nope].reshape(kv_lora, heads * nope)
    w_vt = w_kv[..., nope:].reshape(kv_lora, heads * v_dim).T
    gq = _head_layout(mla_q_norm_g, 1, nope)
    gk = _head_layout(mla_k_norm_g, 1, nope)
    rope = _rope_tables(seq)

    ctx_cols = kv0 - kv0 % 1024
    proj_c = _norm_mod_matmul(ctx.reshape(bsz * lc, d), norm1_g, csh1, csc1, w_pad, bsz * lc, F32, False,
                              512, 1024, col0=ctx_cols).reshape(bsz, lc, np_cols - ctx_cols)
    k_c, v_c = _mla_keys_values(proj_c, (kv0 - ctx_cols) // kv_lora, (kr0 - ctx_cols) // LANES,
                                mla_g_kva, w_k, w_vt, gk, None, heads, qk_dim)

    x2 = x.reshape(bsz * seq, d)
    taps = jnp.zeros((3, np_cols), F32).at[1].set(1.0).at[:, :hyc].set(hy_conv_w[0])
    tap_bias = jnp.zeros((1, np_cols), F32).at[0, :hyc].set(hy_conv_b[0])
    proj = _norm_mod_matmul_conv(x2, norm1_g, sh1, sc1, w_pad, taps, tap_bias, seq, 512, 1024)
    proj = proj.reshape(bsz, seq, np_cols)
    y_hy = _hyena(proj, hy_filt_w1[0], hy_filt_b1[0], hy_filt_w2[0], hy_filt_b2[0], hy_filt_w3[0], hy_freq[0],
                  hy_bias[0])
    q = _mla_queries(proj, q0 // q_lora, mla_g_qa, w_q, gq, rope, heads, qk_dim, qk_dim ** -0.5 * math.log2(math.e))
    k, v = _mla_keys_values(proj, kv0 // kv_lora, kr0 // LANES, mla_g_kva, w_k, w_vt, gk, rope, heads, qk_dim)
    y_att = _attention(q, k, v, k_c, v_c, 1024, 1024, 512)

    x1 = _out_proj(y_hy.reshape(bsz * seq, width), y_att.reshape(bsz * seq, heads * v_dim),
                   w_out[0].astype(BF16), x2, g1, seq)
    hmid = _norm_mod_matmul(x1, norm2_g, sh2, sc2, w_mlp1[0].astype(BF16), seq, BF16, True, 512, 1024)
    out = _matmul_gated_residual(hmid, w_mlp2[0].astype(BF16), x1, g2, seq)
    return out.reshape(bsz, seq, d)
```

```python
import functools
import math

import numpy as np
import jax
import jax.numpy as jnp
from jax import lax
from jax.experimental import pallas as pl
from jax.experimental.pallas import tpu as pltpu

F32 = jnp.float32
BF16 = jnp.bfloat16

NORM_EPS = 1e-6
GRID_W = 64
MLA_ROPE = 64
ROPE_THETA = 10000.0
HY_POS_BANDS = 16
HY_DECAY_TARGET = 1e-2
HY_FAST_DECAY_PCT = 0.3
HY_SLOW_DECAY_PCT = 1.5

LANES = 128
SUBLANES = 8
VMEM_LIMIT = 56 * 1024 * 1024


def _cparams(*sem):
    return pltpu.CompilerParams(dimension_semantics=sem, vmem_limit_bytes=VMEM_LIMIT)


def _pick(dim, pref, align):
    t = min(pref, dim)
    t -= t % align
    while t >= align:
        if dim % t == 0:
            return t
        t -= align
    return dim


def _dot(a, b):
    return jnp.dot(a, b, preferred_element_type=F32)


def _adaln_kernel(c_ref, w_ref, b_ref, o_ref):
    c = c_ref[...]
    s = c * jax.nn.sigmoid(c)
    o_ref[...] = _dot(s.astype(BF16), w_ref[...].astype(BF16)) + b_ref[...]


def _adaln(cc, w, b):
    d, n = w.shape
    tn = _pick(n, 512, LANES)
    return pl.pallas_call(
        _adaln_kernel,
        grid=(n // tn,),
        in_specs=[pl.BlockSpec((8, d), lambda j: (0, 0)),
                  pl.BlockSpec((d, tn), lambda j: (0, j)),
                  pl.BlockSpec((1, tn), lambda j: (0, j))],
        out_specs=pl.BlockSpec((8, tn), lambda j: (0, j)),
        out_shape=jax.ShapeDtypeStruct((8, n), F32),
        compiler_params=_cparams("parallel"),
        name="adaln",
    )(cc, w, b)


def _nmm_kernel(x_ref, g_ref, sh_ref, sc_ref, w_ref, o_ref, h_ref, *, sq_relu):
    @pl.when(pl.program_id(1) == 0)
    def _():
        x = x_ref[...]
        ms = jnp.mean(x * x, axis=-1, keepdims=True)
        y = x * lax.rsqrt(ms + NORM_EPS) * g_ref[...]
        h_ref[...] = (y * (1.0 + sc_ref[0]) + sh_ref[0]).astype(BF16)

    acc = _dot(h_ref[...], w_ref[...])
    if sq_relu:
        acc = jnp.square(jnp.maximum(acc, 0.0))
    o_ref[...] = acc.astype(o_ref.dtype)


def _norm_mod_matmul(x, g, shift, scale, w, rows_per_mod, out_dtype, sq_relu, tm_pref, tn_pref, col0=0):
    m, k = x.shape
    n = w.shape[1] - col0
    tm = _pick(math.gcd(m, rows_per_mod), tm_pref, 16)
    tn = _pick(math.gcd(n, col0) if col0 else n, tn_pref, LANES)
    mod_idx = lambda i, j: ((i * tm) // rows_per_mod, 0, 0)
    return pl.pallas_call(
        functools.partial(_nmm_kernel, sq_relu=sq_relu),
        grid=(m // tm, n // tn),
        in_specs=[pl.BlockSpec((tm, k), lambda i, j: (i, 0)),
                  pl.BlockSpec((1, k), lambda i, j: (0, 0)),
                  pl.BlockSpec((1, 1, k), mod_idx),
                  pl.BlockSpec((1, 1, k), mod_idx),
                  pl.BlockSpec((k, tn), lambda i, j: (0, col0 // tn + j))],
        out_specs=pl.BlockSpec((tm, tn), lambda i, j: (i, j)),
        out_shape=jax.ShapeDtypeStruct((m, n), out_dtype),
        scratch_shapes=[pltpu.VMEM((tm, k), BF16)],
        compiler_params=_cparams("parallel", "arbitrary"),
        name="norm_mod_matmul",
    )(x, g, shift, scale, w)


HALO = 16


def _nmm_conv_kernel(x_ref, xp_ref, xn_ref, g_ref, sh_ref, sc_ref, w_ref, cw_ref, cb_ref, o_ref, h_ref,
                     *, tm, rows_per_seq):
    i = pl.program_id(0)

    def normed(x):
        ms = jnp.mean(x * x, axis=-1, keepdims=True)
        y = x * lax.rsqrt(ms + NORM_EPS) * g_ref[...]
        return y * (1.0 + sc_ref[0]) + sh_ref[0]

    @pl.when(pl.program_id(1) == 0)
    def _():
        first = (i * tm) % rows_per_seq == 0
        last = ((i + 1) * tm) % rows_per_seq == 0
        h_ref[0:HALO, :] = jnp.where(first, 0.0, normed(xp_ref[...])).astype(BF16)
        h_ref[HALO:HALO + tm, :] = normed(x_ref[...]).astype(BF16)
        h_ref[HALO + tm:, :] = jnp.where(last, 0.0, normed(xn_ref[...])).astype(BF16)

    acc = _dot(h_ref[...], w_ref[...])
    rows = acc.shape[0]
    mid = acc[HALO:HALO + tm]
    up = pltpu.roll(acc, 1, 0)[HALO:HALO + tm]
    dn = pltpu.roll(acc, rows - 1, 0)[HALO:HALO + tm]
    o_ref[...] = up * cw_ref[0:1, :] + mid * cw_ref[1:2, :] + dn * cw_ref[2:3, :] + cb_ref[...]


def _norm_mod_matmul_conv(x, g, shift, scale, w, conv_w, conv_b, rows_per_seq, tm_pref, tn_pref):
    m, k = x.shape
    n = w.shape[1]
    tm = _pick(rows_per_seq, tm_pref, HALO)
    tn = _pick(n, tn_pref, LANES)
    r = tm // HALO
    last = m // HALO - 1
    mod_idx = lambda i, j: ((i * tm) // rows_per_seq, 0, 0)
    return pl.pallas_call(
        functools.partial(_nmm_conv_kernel, tm=tm, rows_per_seq=rows_per_seq),
        grid=(m // tm, n // tn),
        in_specs=[pl.BlockSpec((tm, k), lambda i, j: (i, 0)),
                  pl.BlockSpec((HALO, k), lambda i, j: (jnp.maximum(i * r - 1, 0), 0)),
                  pl.BlockSpec((HALO, k), lambda i, j: (jnp.minimum((i + 1) * r, last), 0)),
                  pl.BlockSpec((1, k), lambda i, j: (0, 0)),
                  pl.BlockSpec((1, 1, k), mod_idx),
                  pl.BlockSpec((1, 1, k), mod_idx),
                  pl.BlockSpec((k, tn), lambda i, j: (0, j)),
                  pl.BlockSpec((3, tn), lambda i, j: (0, j)),
                  pl.BlockSpec((1, tn), lambda i, j: (0, j))],
        out_specs=pl.BlockSpec((tm, tn), lambda i, j: (i, j)),
        out_shape=jax.ShapeDtypeStruct((m, n), F32),
        scratch_shapes=[pltpu.VMEM((tm + 2 * HALO, k), BF16)],
        compiler_params=_cparams("parallel", "arbitrary"),
        name="norm_mod_matmul_conv",
    )(x, x, x, g, shift, scale, w, conv_w, conv_b)


def _outproj_kernel(a0_ref, a1_ref, w_ref, res_ref, gate_ref, o_ref, *, k0):
    acc = _dot(a0_ref[...], w_ref[0:k0, :]) + _dot(a1_ref[...], w_ref[k0:, :])
    o_ref[...] = res_ref[...] + gate_ref[0] * acc


def _out_proj(a0, a1, w, res, gate, rows_per_mod):
    m, k0 = a0.shape
    k1 = a1.shape[1]
    n = w.shape[1]
    tm = _pick(math.gcd(m, rows_per_mod), 1024, 16)
    tn = _pick(n, 1024, LANES)
    return pl.pallas_call(
        functools.partial(_outproj_kernel, k0=k0),
        grid=(m // tm, n // tn),
        in_specs=[pl.BlockSpec((tm, k0), lambda i, j: (i, 0)),
                  pl.BlockSpec((tm, k1), lambda i, j: (i, 0)),
                  pl.BlockSpec((k0 + k1, tn), lambda i, j: (0, j)),
                  pl.BlockSpec((tm, tn), lambda i, j: (i, j)),
                  pl.BlockSpec((1, 1, tn), lambda i, j: ((i * tm) // rows_per_mod, 0, j))],
        out_specs=pl.BlockSpec((tm, tn), lambda i, j: (i, j)),
        out_shape=jax.ShapeDtypeStruct((m, n), F32),
        compiler_params=_cparams("parallel", "parallel"),
        name="out_proj",
    )(a0, a1, w, res, gate)


def _mmres_kernel(a_ref, w_ref, res_ref, gate_ref, o_ref, acc_ref):
    kk = pl.program_id(2)

    @pl.when(kk == 0)
    def _():
        acc_ref[...] = jnp.zeros_like(acc_ref)

    acc_ref[...] += _dot(a_ref[...], w_ref[...])

    @pl.when(kk == pl.num_programs(2) - 1)
    def _():
        o_ref[...] = res_ref[...] + gate_ref[0] * acc_ref[...]


def _matmul_gated_residual(a, w, res, gate, rows_per_mod):
    m, k = a.shape
    n = w.shape[1]
    tm = _pick(math.gcd(m, rows_per_mod), 1024, 16)
    tn = _pick(n, 1024, LANES)
    tk = _pick(k, 2048, LANES)
    return pl.pallas_call(
        _mmres_kernel,
        grid=(m // tm, n // tn, k // tk),
        in_specs=[pl.BlockSpec((tm, tk), lambda i, j, kk: (i, kk)),
                  pl.BlockSpec((tk, tn), lambda i, j, kk: (kk, j)),
                  pl.BlockSpec((tm, tn), lambda i, j, kk: (i, j)),
                  pl.BlockSpec((1, 1, tn), lambda i, j, kk: ((i * tm) // rows_per_mod, 0, j))],
        out_specs=pl.BlockSpec((tm, tn), lambda i, j, kk: (i, j)),
        out_shape=jax.ShapeDtypeStruct((m, n), F32),
        scratch_shapes=[pltpu.VMEM((tm, tn), F32)],
        compiler_params=_cparams("parallel", "parallel", "arbitrary"),
        name="matmul_gated_residual",
    )(a, w, res, gate)


def _filt_kernel(feat_ref, t_ref, w1_ref, b1_ref, w2_ref, b2_ref, fr_ref, w3_ref, dl_ref, o_ref, h_ref, *, seq):
    i = pl.program_id(0)
    j = pl.program_id(1)

    @pl.when(j == 0)
    def _():
        fr = fr_ref[...]
        h = jnp.sin(fr * (_dot(feat_ref[...].astype(BF16), w1_ref[...]) + b1_ref[...]))
        h = jnp.sin(fr * (_dot(h.astype(BF16), w2_ref[...]) + b2_ref[...]))
        h_ref[...] = h.astype(BF16)

    f = _dot(h_ref[...], w3_ref[...]) * jnp.exp(-t_ref[...] * dl_ref[...])
    rows = lax.broadcasted_iota(jnp.int32, f.shape, 0) + i * f.shape[0]
    o_ref[0] = jnp.where(rows == seq, 0.0, f)


def _hyena_filters(seq, w1, b1, w2, b2, w3, freq, width):
    two_sided = lambda a: np.concatenate([a, a[:1], a[:0:-1]], axis=0)
    pos = np.arange(seq, dtype=np.float32)
    t = np.linspace(0.0, 1.0, seq, dtype=np.float32)[:, None]
    bands = np.linspace(1e-4, HY_POS_BANDS - 1, HY_POS_BANDS, dtype=np.float32)
    ang = np.float32(2.0 * math.pi / seq) * pos[:, None] * bands[None, :]
    feats = np.concatenate([t, np.cos(ang), -np.sin(ang)], axis=-1).astype(np.float32)
    pd = feats.shape[1]
    hid = w1.shape[1]
    hp = LANES
    feats = jnp.asarray(two_sided(np.pad(feats, ((0, 0), (0, hp - pd)))))
    t = jnp.asarray(two_sided(t))
    w1p = jnp.pad(w1, ((0, hp - pd), (0, hp - hid))).astype(BF16)
    w2p = jnp.pad(w2, ((0, hp - hid), (0, hp - hid))).astype(BF16)
    w3p = jnp.pad(w3, ((0, hp - hid), (0, 0))).astype(BF16)
    b1p = jnp.pad(b1, (0, hp - hid))[None, :]
    b2p = jnp.pad(b2, (0, hp - hid))[None, :]
    frp = jnp.pad(freq, (0, hp - hid))[None, :]
    deltas = np.abs(np.linspace(math.log(HY_DECAY_TARGET) / HY_SLOW_DECAY_PCT,
                                math.log(HY_DECAY_TARGET) / HY_FAST_DECAY_PCT, width, dtype=np.float32))
    ncol = w3.shape[1] // 2
    dl = jnp.asarray(np.tile(deltas, ncol // width)[None, :])
    tl = _pick(seq, 512, SUBLANES)
    tn = _pick(ncol, 2048, LANES)
    fwd_tiles = seq // tl
    return pl.pallas_call(
        functools.partial(_filt_kernel, seq=seq),
        grid=(2 * seq // tl, ncol // tn),
        in_specs=[pl.BlockSpec((tl, hp), lambda i, j: (i, 0)),
                  pl.BlockSpec((tl, 1), lambda i, j: (i, 0)),
                  pl.BlockSpec((hp, hp), lambda i, j: (0, 0)),
                  pl.BlockSpec((1, hp), lambda i, j: (0, 0)),
                  pl.BlockSpec((hp, hp), lambda i, j: (0, 0)),
                  pl.BlockSpec((1, hp), lambda i, j: (0, 0)),
                  pl.BlockSpec((1, hp), lambda i, j: (0, 0)),
                  pl.BlockSpec((hp, tn), lambda i, j: (0, (i // fwd_tiles) * (ncol // tn) + j)),
                  pl.BlockSpec((1, tn), lambda i, j: (0, j))],
        out_specs=pl.BlockSpec((1, tl, tn), lambda i, j: (0, i, j)),
        out_shape=jax.ShapeDtypeStruct((1, 2 * seq, ncol), F32),
        scratch_shapes=[pltpu.VMEM((tl, hp), BF16)],
        compiler_params=_cparams("parallel", "arbitrary"),
        name="hyena_filters",
    )(feats, t, w1p, b1p, w2p, b2p, frp, w3p, dl)


K1_GROUP = SUBLANES


def _dft_tables(n1, n2):
    n = n1 * n2
    hl = n1 // 2
    k1 = np.arange(n1)[None, :, None]
    l1 = np.arange(hl)[None, None, :]
    l2 = np.arange(n2)[:, None, None]
    ang = -2.0 * np.pi * ((k1 * (n2 * l1 + l2)) % n) / n
    gr, gi = np.cos(ang), np.sin(ang)
    gm = np.concatenate([np.concatenate([gr, -gi], axis=2), np.concatenate([gi, gr], axis=2)], axis=1)
    a2 = -2.0 * np.pi * ((np.arange(n2)[:, None] * np.arange(n2)[None, :]) % n2) / n2
    fr, fi = np.cos(a2), np.sin(a2)
    f2 = np.block([[fr, -fi], [fi, fr]])
    if2 = np.block([[fr, fi], [-fi, fr]])
    ir, ii = np.transpose(gr, (0, 2, 1)) / n, -np.transpose(gi, (0, 2, 1)) / n
    igm = np.concatenate([np.concatenate([ir, -ii], axis=2), np.concatenate([ii, ir], axis=2)], axis=1)
    angf = -2.0 * np.pi * ((k1 * (n2 * np.arange(n1)[None, None, :] + l2)) % n) / n
    gmr = np.concatenate([np.cos(angf), np.sin(angf)], axis=1)
    to = lambda a: jnp.asarray(a.astype(np.float32)).astype(BF16)
    return to(gm), to(f2), to(if2), to(igm), to(gmr)


def _fft_split(seq):
    n = 2 * seq
    n2 = LANES if n % (LANES * 2 * SUBLANES) == 0 and n // LANES >= 2 * SUBLANES else 2 * SUBLANES
    n1 = n // n2
    assert n1 * n2 == n and n1 % (2 * SUBLANES) == 0 and n2 % SUBLANES == 0, (n1, n2)
    return n1, n2


def _fill_pitched(src_ref, p_ref, *, nb, rows, n2, pitch):
    def body(l1, carry):
        for b in range(nb):
            p_ref[b, pl.ds(l1, n2, stride=pitch), :] = src_ref[b, pl.ds(pl.multiple_of(l1 * n2, n2), n2), :]
        return carry

    lax.fori_loop(0, rows, body, 0, unroll=2)


def _stage1_to_yv(p_ref, gm_ref, yv_ref, step, *, nb, n1, l2c, pitch):
    rows = n1 // nb

    def body(t, carry):
        l2 = step * l2c + t
        base = pl.multiple_of(l2 * pitch, SUBLANES)
        rhs = jnp.concatenate([p_ref[b, pl.ds(base, rows), :] for b in range(nb)], axis=0)
        res = _dot(gm_ref[t], rhs.astype(BF16))
        row0 = pl.multiple_of(l2 * K1_GROUP, K1_GROUP)
        for g in range(n1 // K1_GROUP):
            for ri in range(2):
                lo = ri * n1 + g * K1_GROUP
                yv_ref[g * 2 + ri, pl.ds(row0, K1_GROUP), :] = res[lo:lo + K1_GROUP]
        return carry

    lax.fori_loop(0, l2c, body, 0, unroll=8)


def _yv_pair(yv_ref, g, t, n2):
    rows = lambda tt: pl.ds(tt, n2, stride=K1_GROUP)
    re = jnp.concatenate([yv_ref[g * 2, rows(t), :], yv_ref[g * 2, rows(t + 1), :]], axis=1)
    im = jnp.concatenate([yv_ref[g * 2 + 1, rows(t), :], yv_ref[g * 2 + 1, rows(t + 1), :]], axis=1)
    return jnp.concatenate([re, im], axis=0).astype(BF16)


def _filtspec_kernel(filt_ref, gmr_ref, f2_ref, o_ref, yv_ref, p_ref, *, n1, n2, l2c, g2, s1, pitch):
    s = pl.program_id(1)

    @pl.when(s == 0)
    def _():
        _fill_pitched(filt_ref, p_ref, nb=1, rows=n1, n2=n2, pitch=pitch)

    @pl.when(s < s1)
    def _():
        _stage1_to_yv(p_ref, gmr_ref, yv_ref, s, nb=1, n1=n1, l2c=l2c, pitch=pitch)

    @pl.when(s >= s1)
    def _():
        for gg in range(g2):
            g = (s - s1) * g2 + gg
            for t in range(0, K1_GROUP, 2):
                x = _dot(f2_ref[...], _yv_pair(yv_ref, g, t, n2))
                for d in range(2):
                    o_ref[gg * K1_GROUP + t + d, 0] = x[:n2, d * LANES:(d + 1) * LANES].astype(o_ref.dtype)
                    o_ref[gg * K1_GROUP + t + d, 1] = x[n2:, d * LANES:(d + 1) * LANES].astype(o_ref.dtype)


def _filter_spectrum(filt, gmr, f2, n1, n2):
    c = filt.shape[-1]
    ng = n1 // K1_GROUP
    l2c = _pick(n2, 32, 4)
    g2 = _pick(ng, 4, 1)
    s1, s2 = n2 // l2c, ng // g2
    pitch = n1 + SUBLANES
    return pl.pallas_call(
        functools.partial(_filtspec_kernel, n1=n1, n2=n2, l2c=l2c, g2=g2, s1=s1, pitch=pitch),
        grid=(c // LANES, s1 + s2),
        in_specs=[pl.BlockSpec((1, n1 * n2, LANES), lambda cb, s: (0, 0, cb), pipeline_mode=pl.Buffered(1)),
                  pl.BlockSpec((l2c, 2 * n1, n1), lambda cb, s: (jnp.minimum(s, s1 - 1), 0, 0)),
                  pl.BlockSpec((2 * n2, 2 * n2), lambda cb, s: (0, 0))],
        out_specs=pl.BlockSpec((g2 * K1_GROUP, 2, n2, LANES), lambda cb, s: (jnp.maximum(s - s1, 0), 0, 0, cb)),
        out_shape=jax.ShapeDtypeStruct((n1, 2, n2, c), BF16),
        scratch_shapes=[pltpu.VMEM((2 * ng, n2 * K1_GROUP, LANES), F32), pltpu.VMEM((1, n2 * pitch, LANES), F32)],
        compiler_params=_cparams("parallel", "arbitrary"),
        name="filter_spectrum",
    )(filt, gmr, f2)


def _fftconv_kernel(sig_ref, gm_ref, k_ref, f2_ref, if2_ref, igm_ref, gate_ref, bias_ref, o_ref, yv_ref, p_ref,
                    *, n1, n2, l2c, g2, l3, rows4, s1, s2, s3, pitch):
    s = pl.program_id(1)
    hl = n1 // 2
    ng = n1 // K1_GROUP

    @pl.when(s == 0)
    def _():
        _fill_pitched(sig_ref, p_ref, nb=2, rows=hl, n2=n2, pitch=pitch)

    @pl.when(s < s1)
    def _():
        _stage1_to_yv(p_ref, gm_ref, yv_ref, s, nb=2, n1=n1, l2c=l2c, pitch=pitch)

    @pl.when((s >= s1) & (s < s1 + s2))
    def _():
        for gg in range(g2):
            g = (s - s1) * g2 + gg
            rows = lambda tt: pl.ds(tt, n2, stride=K1_GROUP)
            for t in range(0, K1_GROUP, 2):
                x = _dot(f2_ref[...], _yv_pair(yv_ref, g, t, n2))
                xr, xi = x[:n2], x[n2:]
                kt = gg * K1_GROUP + t
                kr = jnp.concatenate([k_ref[kt, 0], k_ref[kt + 1, 0]], axis=1).astype(F32)
                ki = jnp.concatenate([k_ref[kt, 1], k_ref[kt + 1, 1]], axis=1).astype(F32)
                p = jnp.concatenate([xr * kr - xi * ki, xr * ki + xi * kr], axis=0).astype(BF16)
                yp = _dot(if2_ref[...], p)
                for d in range(2):
                    for ri in range(2):
                        yv_ref[g * 2 + ri, rows(t + d), :] = yp[ri * n2:(ri + 1) * n2, d * LANES:(d + 1) * LANES]

    @pl.when((s >= s1 + s2) & (s < s1 + s2 + s3))
    def _():
        def body(i, carry):
            l2 = (s - s1 - s2) * (l3 * SUBLANES) + i
            row0 = pl.multiple_of(l2 * K1_GROUP, K1_GROUP)
            tiles = [yv_ref[g * 2 + ri, pl.ds(row0, K1_GROUP), :] for ri in range(2) for g in range(ng)]
            res = _dot(igm_ref[i], jnp.concatenate(tiles, axis=0).astype(BF16))
            base = pl.multiple_of(l2 * pitch, SUBLANES)
            for b in range(2):
                p_ref[b, pl.ds(base, hl), :] = res[b * hl:(b + 1) * hl]
            return carry

        lax.fori_loop(0, l3 * SUBLANES, body, 0, unroll=8)

    @pl.when(s >= s1 + s2 + s3)
    def _():
        r = s - s1 - s2 - s3
        for j in range(rows4 // n2):
            l1 = r * (rows4 // n2) + j
            nat = pl.ds(pl.multiple_of(l1 * n2, n2), n2)
            for b in range(2):
                conv = p_ref[b, pl.ds(l1, n2, stride=pitch), :] + bias_ref[...] * sig_ref[b, nat, :]
                o_ref[b, j * n2:(j + 1) * n2, :] = (gate_ref[b, j * n2:(j + 1) * n2, :] * conv).astype(o_ref.dtype)


def _fft_conv_gated(sig, kspec, kcol_block0, gate, gate_col_block0, bias, tables, n1, n2, out_dtype):
    gm, f2, if2, igm = tables
    c = bias.shape[1]
    seq = sig.shape[1]
    ng = n1 // K1_GROUP
    l2c = _pick(n2, 32, 4)
    g2 = _pick(ng, 4, 1)
    l3 = _pick(n2 // SUBLANES, 4, 1)
    rows4 = _pick(seq, 2048, n2)
    s1, s2, s3, s4 = n2 // l2c, ng // g2, n2 // (l3 * SUBLANES), seq // rows4
    pitch = n1 // 2 + SUBLANES
    clamp = lambda v, n: jnp.clip(v, 0, n - 1)
    return pl.pallas_call(
        functools.partial(_fftconv_kernel, n1=n1, n2=n2, l2c=l2c, g2=g2, l3=l3, rows4=rows4, s1=s1, s2=s2, s3=s3,
                          pitch=pitch),
        grid=(c // LANES, s1 + s2 + s3 + s4),
        in_specs=[pl.BlockSpec((2, seq, LANES), lambda cb, s: (0, 0, cb), pipeline_mode=pl.Buffered(1)),
                  pl.BlockSpec((l2c, 2 * n1, n1), lambda cb, s: (clamp(s, s1), 0, 0)),
                  pl.BlockSpec((g2 * K1_GROUP, 2, n2, LANES),
                               lambda cb, s: (clamp(s - s1, s2), 0, 0, kcol_block0 + cb)),
                  pl.BlockSpec((2 * n2, 2 * n2), lambda cb, s: (0, 0)),
                  pl.BlockSpec((2 * n2, 2 * n2), lambda cb, s: (0, 0)),
                  pl.BlockSpec((l3 * SUBLANES, n1, 2 * n1), lambda cb, s: (clamp(s - s1 - s2, s3), 0, 0)),
                  pl.BlockSpec((2, rows4, LANES),
                               lambda cb, s: (0, clamp(s - s1 - s2 - s3, s4), gate_col_block0 + cb)),
                  pl.BlockSpec((1, LANES), lambda cb, s: (0, cb))],
        out_specs=pl.BlockSpec((2, rows4, LANES), lambda cb, s: (0, clamp(s - s1 - s2 - s3, s4), cb)),
        out_shape=jax.ShapeDtypeStruct((2, seq, c), out_dtype),
        scratch_shapes=[pltpu.VMEM((2 * ng, n2 * K1_GROUP, LANES), F32), pltpu.VMEM((2, n2 * pitch, LANES), F32)],
        compiler_params=_cparams("parallel", "arbitrary"),
        name="fft_conv_gated",
    )(sig, gm, kspec, f2, if2, igm, gate, bias)


def _hyena(u, fw1, fb1, fw2, fb2, fw3, freq, hy_bias):
    bsz, seq, _ = u.shape
    assert bsz == 2, "the FFT convolution packs exactly two batch elements into one complex signal"
    width = hy_bias.shape[1]
    n1, n2 = _fft_split(seq)
    gm, f2, if2, igm, gmr = _dft_tables(n1, n2)

    filt = _hyena_filters(seq, fw1, fb1, fw2, fb2, fw3, freq, width)
    kspec = _filter_spectrum(filt, gmr, f2, n1, n2)

    tables = (gm, f2, if2, igm)
    wb = width // LANES
    z = _fft_conv_gated(u, kspec, 0, u, wb, hy_bias[0:1], tables, n1, n2, F32)
    return _fft_conv_gated(z, kspec, wb, u, 2 * wb, hy_bias[1:2], tables, n1, n2, BF16)


HEAD_LANES = 2 * LANES


ROPE_NF = MLA_ROPE // 4


def _head_layout(w, heads, nope):
    r = w.shape[0]
    w3 = jnp.pad(w.reshape(r, heads, nope + MLA_ROPE), ((0, 0), (0, 0), (0, HEAD_LANES - nope - MLA_ROPE)))
    return w3.reshape(r, heads * HEAD_LANES)


def _rope_tables(seq):
    rows = seq // GRID_W
    row = np.repeat(np.arange(rows, dtype=np.float32), GRID_W)
    col = np.tile(np.arange(GRID_W, dtype=np.float32), rows)
    half = MLA_ROPE // 2
    inv = (np.float32(ROPE_THETA) ** (-np.arange(0, half, 2, dtype=np.float32) / np.float32(half))).astype(np.float32)
    zero = np.zeros((seq, ROPE_NF), np.float32)
    cos, sin_lo, sin_hi = [], [], []
    for pos in (row, col):
        ang = pos[:, None] * inv
        cos += [np.cos(ang), np.cos(ang)]
        sin_lo += [-np.sin(ang), zero]
        sin_hi += [zero, np.sin(ang)]
    pad = [np.zeros((seq, LANES - MLA_ROPE), np.float32)]
    return tuple(jnp.asarray(np.concatenate(t + pad, axis=-1).astype(np.float32)) for t in (cos, sin_lo, sin_hi))


def _rope(t, cos, sin_lo, sin_hi):
    return t * cos + pltpu.roll(t, LANES - ROPE_NF, 1) * sin_lo + pltpu.roll(t, ROPE_NF, 1) * sin_hi


def _rms(x, g):
    return x * lax.rsqrt(jnp.mean(x * x, axis=-1, keepdims=True) + NORM_EPS) * g


def _q_kernel(qa_ref, g_ref, w_ref, gn_ref, cos_ref, slo_ref, shi_ref, o_ref, *, heads, qk_dim, out_scale):
    xn = _rms(qa_ref[...], g_ref[...]).astype(BF16)
    q = _dot(xn, w_ref[...])
    gn = gn_ref[...]
    cos, slo, shi = cos_ref[...], slo_ref[...], shi_ref[...]
    for h in range(heads):
        qh = q[:, h * HEAD_LANES:(h + 1) * HEAD_LANES]
        inv = lax.rsqrt(jnp.sum(qh * qh, axis=-1, keepdims=True) / qk_dim + NORM_EPS)
        qn = qh * inv * gn
        o_ref[0, h, :, 0:LANES] = (qn[:, :LANES] * out_scale).astype(BF16)
        o_ref[0, h, :, LANES:] = (_rope(qn[:, LANES:], cos, slo, shi) * out_scale).astype(BF16)


def _mla_queries(proj, col_block, g_qa, w_q, gn, rope, heads, qk_dim, out_scale):
    bsz, seq, _ = proj.shape
    r = g_qa.shape[1]
    tm = _pick(seq, 512, 16)
    return pl.pallas_call(
        functools.partial(_q_kernel, heads=heads, qk_dim=qk_dim, out_scale=out_scale),
        grid=(bsz, seq // tm),
        in_specs=[pl.BlockSpec((None, tm, r), lambda b, i: (b, i, col_block)),
                  pl.BlockSpec((1, r), lambda b, i: (0, 0)),
                  pl.BlockSpec((r, heads * HEAD_LANES), lambda b, i: (0, 0)),
                  pl.BlockSpec((1, HEAD_LANES), lambda b, i: (0, 0))]
                 + [pl.BlockSpec((tm, LANES), lambda b, i: (i, 0))] * 3,
        out_specs=pl.BlockSpec((1, heads, tm, HEAD_LANES), lambda b, i: (b, 0, i, 0)),
        out_shape=jax.ShapeDtypeStruct((bsz, heads, seq, HEAD_LANES), BF16),
        compiler_params=_cparams("parallel", "parallel"),
        name="mla_queries",
    )(proj, g_qa, w_q, gn, *rope)


V_ROWS = LANES + 16


def _kv_kernel(kva_ref, kr_ref, g_ref, wk_ref, wvt_ref, gn_ref, *rest, heads, qk_dim, rope):
    if rope:
        cos_ref, slo_ref, shi_ref, k_ref, v_ref = rest
    else:
        k_ref, v_ref = rest
    xn = _rms(kva_ref[...], g_ref[...]).astype(BF16)
    kk = _dot(xn, wk_ref[...])
    vt = lax.dot_general(wvt_ref[...], xn, (((1,), (1,)), ((), ())), preferred_element_type=F32)
    gn = gn_ref[...]
    kr = kr_ref[...]
    ssr = jnp.sum(kr * kr, axis=-1, keepdims=True)
    krg = kr * gn[:, LANES:]
    if rope:
        krg = _rope(krg, cos_ref[...], slo_ref[...], shi_ref[...])
    ones_row = (lax.broadcasted_iota(jnp.int32, (V_ROWS - LANES, kr.shape[0]), 0) == 0).astype(BF16)
    for h in range(heads):
        kn = kk[:, h * LANES:(h + 1) * LANES]
        inv = lax.rsqrt((jnp.sum(kn * kn, axis=-1, keepdims=True) + ssr) / qk_dim + NORM_EPS)
        k_ref[0, h, :, 0:LANES] = (kn * inv * gn[:, :LANES]).astype(BF16)
        k_ref[0, h, :, LANES:] = (krg * inv).astype(BF16)
        v_ref[0, h, 0:LANES, :] = vt[h * LANES:(h + 1) * LANES].astype(BF16)
        v_ref[0, h, LANES:, :] = ones_row


def _mla_keys_values(proj, kv_block, kr_block, g_kva, w_k, w_vt, gn, rope, heads, qk_dim):
    bsz, seq, _ = proj.shape
    r = g_kva.shape[1]
    tm = _pick(seq, 512, LANES)
    in_specs = [pl.BlockSpec((None, tm, r), lambda b, i: (b, i, kv_block)),
                pl.BlockSpec((None, tm, LANES), lambda b, i: (b, i, kr_block)),
                pl.BlockSpec((1, r), lambda b, i: (0, 0)),
                pl.BlockSpec((r, heads * LANES), lambda b, i: (0, 0)),
                pl.BlockSpec((heads * LANES, r), lambda b, i: (0, 0)),
                pl.BlockSpec((1, HEAD_LANES), lambda b, i: (0, 0))]
    args = [proj, proj, g_kva, w_k, w_vt, gn]
    if rope is not None:
        in_specs += [pl.BlockSpec((tm, LANES), lambda b, i: (i, 0))] * 3
        args += list(rope)
    return pl.pallas_call(
        functools.partial(_kv_kernel, heads=heads, qk_dim=qk_dim, rope=rope is not None),
        grid=(bsz, seq // tm),
        in_specs=in_specs,
        out_specs=[pl.BlockSpec((1, heads, tm, HEAD_LANES), lambda b, i: (b, 0, i, 0)),
                   pl.BlockSpec((1, heads, V_ROWS, tm), lambda b, i: (b, 0, 0, i))],
        out_shape=[jax.ShapeDtypeStruct((bsz, heads, seq, HEAD_LANES), BF16),
                   jax.ShapeDtypeStruct((bsz, heads, V_ROWS, seq), BF16)],
        compiler_params=_cparams("parallel", "parallel"),
        name="mla_keys_values",
    )(*args)


ATTN_HEADS_PER_STEP = 2
ATTN_LOOKAHEAD = 2


def _attn_kernel(q_ref, k_ref, v_ref, kc_ref, vc_ref, o_ref, m_ref, acc_ref, *, hps, kchunk):
    kj = pl.program_id(3)

    def scores(g, k):
        return lax.dot_general(k, q_ref[0, g], (((1,), (1,)), ((), ())), preferred_element_type=F32)

    def update(g, st, vt):
        m_old = m_ref[g]
        m_new = jnp.maximum(m_old, jnp.max(st, axis=0, keepdims=True))
        p = jnp.exp2(st - m_new)
        acc_ref[g] = jnp.exp2(m_old - m_new) * acc_ref[g] + _dot(vt, p.astype(BF16))
        m_ref[g] = m_new

    def sweep(units):
        ahead = [scores(g, k()) for g, k, _ in units[:ATTN_LOOKAHEAD]]
        for i, (g, _, vt) in enumerate(units):
            st = ahead.pop(0)
            if i + ATTN_LOOKAHEAD < len(units):
                nxt = units[i + ATTN_LOOKAHEAD]
                ahead.append(scores(nxt[0], nxt[1]()))
            update(g, st, vt())

    @pl.when(kj == 0)
    def _():
        m_ref[...] = jnp.full_like(m_ref, -jnp.inf)
        acc_ref[...] = jnp.zeros_like(acc_ref)
        sweep([(g, functools.partial(lambda g: kc_ref[0, g], g), functools.partial(lambda g: vc_ref[0, g], g))
               for g in range(hps)])

    tk = k_ref.shape[2]
    sweep([(g,
            functools.partial(lambda g, c: k_ref[0, g, c * kchunk:(c + 1) * kchunk, :], g, c),
            functools.partial(lambda g, c: v_ref[0, g, :, c * kchunk:(c + 1) * kchunk], g, c))
           for c in range(tk // kchunk) for g in range(hps)])

    @pl.when(kj == pl.num_programs(3) - 1)
    def _():
        for g in range(hps):
            acc = acc_ref[g]
            out_t = acc[:LANES] / acc[LANES:LANES + 1]
            o_ref[0, :, g * LANES:(g + 1) * LANES] = out_t.T.astype(o_ref.dtype)


def _attention(q, k, vt, kc, vct, tq_pref, tk_pref, kchunk_pref):
    bsz, heads, seq, _ = q.shape
    lc = kc.shape[2]
    hps = ATTN_HEADS_PER_STEP if heads % ATTN_HEADS_PER_STEP == 0 else 1
    tq = _pick(seq, tq_pref, LANES)
    tk = _pick(seq, tk_pref, LANES)
    kchunk = _pick(tk, kchunk_pref, LANES)
    return pl.pallas_call(
        functools.partial(_attn_kernel, hps=hps, kchunk=kchunk),
        grid=(bsz, heads // hps, seq // tq, seq // tk),
        in_specs=[pl.BlockSpec((1, hps, tq, HEAD_LANES), lambda b, h, i, j: (b, h, i, 0)),
                  pl.BlockSpec((1, hps, tk, HEAD_LANES), lambda b, h, i, j: (b, h, j, 0)),
                  pl.BlockSpec((1, hps, V_ROWS, tk), lambda b, h, i, j: (b, h, 0, j)),
                  pl.BlockSpec((1, hps, lc, HEAD_LANES), lambda b, h, i, j: (b, h, 0, 0)),
                  pl.BlockSpec((1, hps, V_ROWS, lc), lambda b, h, i, j: (b, h, 0, 0))],
        out_specs=pl.BlockSpec((1, tq, hps * LANES), lambda b, h, i, j: (b, i, h)),
        out_shape=jax.ShapeDtypeStruct((bsz, seq, heads * LANES), BF16),
        scratch_shapes=[pltpu.VMEM((hps, 1, tq), F32), pltpu.VMEM((hps, V_ROWS, tq), F32)],
        compiler_params=_cparams("parallel", "parallel", "parallel", "arbitrary"),
        name="attention",
    )(q, k, vt, kc, vct)


def kernel(x, c, ctx, c_ctx, norm1_g, norm2_g, w_ada, b_ada, w_in, hy_conv_w, hy_conv_b, hy_filt_w1, hy_filt_b1, hy_filt_w2, hy_filt_b2, hy_filt_w3, hy_freq, hy_bias, mla_g_qa, mla_w_qb, mla_g_kva, mla_w_kvb, mla_q_norm_g, mla_k_norm_g, w_out, w_mlp1, w_mlp2):
    assert w_ada.shape[0] == 1, "single-layer block"
    bsz, seq, d = x.shape
    lc = ctx.shape[1]
    hyc = hy_conv_b.shape[1]
    width = hy_bias.shape[2]
    q_lora = mla_g_qa.shape[1]
    kv_lora = mla_g_kva.shape[1]
    qk_dim = mla_q_norm_g.shape[1]
    nope = qk_dim - MLA_ROPE
    heads = mla_w_qb.shape[2] // qk_dim
    v_dim = mla_w_kvb.shape[2] // heads - nope
    assert nope == LANES and v_dim == LANES and seq % GRID_W == 0
    q0, kv0, kr0 = hyc, hyc + q_lora, hyc + q_lora + kv_lora
    assert q0 % q_lora == 0 and kv0 % kv_lora == 0 and kr0 % LANES == 0

    cc = jnp.zeros((8, d), F32).at[:bsz].set(c).at[bsz].set(c_ctx)
    mod = _adaln(cc, w_ada[0], b_ada)
    chunk = lambda i: mod[:bsz, i * d:(i + 1) * d][:, None, :]
    sh1, sc1, g1, sh2, sc2, g2 = [chunk(i) for i in range(6)]
    csh1 = mod[bsz:bsz + 1, 0:d][:, None, :]
    csc1 = mod[bsz:bsz + 1, d:2 * d][:, None, :]

    w_in0 = w_in[0]
    np_cols = kr0 + LANES
    np_cols += (-np_cols) % 1024
    w_pad = jnp.pad(w_in0, ((0, 0), (0, np_cols - w_in0.shape[1]))).astype(BF16)
    w_q = _head_layout(mla_w_qb[0], heads, nope).astype(BF16)
    w_kv = mla_w_kvb[0].reshape(kv_lora, heads, nope + v_dim).astype(BF16)
    w_k = w_kv[..., :nope].reshape(kv_lora, heads * nope)
    w_vt = w_kv[..., nope:].reshape(kv_lora, heads * v_dim).T
    gq = _head_layout(mla_q_norm_g, 1, nope)
    gk = _head_layout(mla_k_norm_g, 1, nope)
    rope = _rope_tables(seq)

    ctx_cols = kv0 - kv0 % 1024
    proj_c = _norm_mod_matmul(ctx.reshape(bsz * lc, d), norm1_g, csh1, csc1, w_pad, bsz * lc, F32, False,
                              512, 1024, col0=ctx_cols).reshape(bsz, lc, np_cols - ctx_cols)
    k_c, v_c = _mla_keys_values(proj_c, (kv0 - ctx_cols) // kv_lora, (kr0 - ctx_cols) // LANES,
                                mla_g_kva, w_k, w_vt, gk, None, heads, qk_dim)

    x2 = x.reshape(bsz * seq, d)
    taps = jnp.zeros((3, np_cols), F32).at[1].set(1.0).at[:, :hyc].set(hy_conv_w[0])
    tap_bias = jnp.zeros((1, np_cols), F32).at[0, :hyc].set(hy_conv_b[0])
    proj = _norm_mod_matmul_conv(x2, norm1_g, sh1, sc1, w_pad, taps, tap_bias, seq, 512, 1024)
    proj = proj.reshape(bsz, seq, np_cols)
    y_hy = _hyena(proj, hy_filt_w1[0], hy_filt_b1[0], hy_filt_w2[0], hy_filt_b2[0], hy_filt_w3[0], hy_freq[0],
                  hy_bias[0])
    q = _mla_queries(proj, q0 // q_lora, mla_g_qa, w_q, gq, rope, heads, qk_dim, qk_dim ** -0.5 * math.log2(math.e))
    k, v = _mla_keys_values(proj, kv0 // kv_lora, kr0 // LANES, mla_g_kva, w_k, w_vt, gk, rope, heads, qk_dim)
    y_att = _attention(q, k, v, k_c, v_c, 1024, 2048, 512)

    x1 = _out_proj(y_hy.reshape(bsz * seq, width), y_att.reshape(bsz * seq, heads * v_dim),
                   w_out[0].astype(BF16), x2, g1, seq)
    hmid = _norm_mod_matmul(x1, norm2_g, sh2, sc2, w_mlp1[0].astype(BF16), seq, BF16, True, 512, 1024)
    out = _matmul_gated_residual(hmid, w_mlp2[0].astype(BF16), x1, g2, seq)
    return out.reshape(bsz, seq, d)
```

```python
import functools
import math

import numpy as np
import jax
import jax.numpy as jnp
from jax import lax
from jax.experimental import pallas as pl
from jax.experimental.pallas import tpu as pltpu

F32 = jnp.float32
BF16 = jnp.bfloat16

NORM_EPS = 1e-6
GRID_W = 64
MLA_ROPE = 64
ROPE_THETA = 10000.0
HY_POS_BANDS = 16
HY_DECAY_TARGET = 1e-2
HY_FAST_DECAY_PCT = 0.3
HY_SLOW_DECAY_PCT = 1.5

LANES = 128
SUBLANES = 8
VMEM_LIMIT = 56 * 1024 * 1024


def _cparams(*sem):
    return pltpu.CompilerParams(dimension_semantics=sem, vmem_limit_bytes=VMEM_LIMIT)


def _pick(dim, pref, align):
    t = min(pref, dim)
    t -= t % align
    while t >= align:
        if dim % t == 0:
            return t
        t -= align
    return dim


def _dot(a, b):
    return jnp.dot(a, b, preferred_element_type=F32)


def _adaln_kernel(c_ref, w_ref, b_ref, o_ref):
    c = c_ref[...]
    s = c * jax.nn.sigmoid(c)
    o_ref[...] = _dot(s.astype(BF16), w_ref[...].astype(BF16)) + b_ref[...]


def _adaln(cc, w, b):
    d, n = w.shape
    tn = _pick(n, 512, LANES)
    return pl.pallas_call(
        _adaln_kernel,
        grid=(n // tn,),
        in_specs=[pl.BlockSpec((8, d), lambda j: (0, 0)),
                  pl.BlockSpec((d, tn), lambda j: (0, j)),
                  pl.BlockSpec((1, tn), lambda j: (0, j))],
        out_specs=pl.BlockSpec((8, tn), lambda j: (0, j)),
        out_shape=jax.ShapeDtypeStruct((8, n), F32),
        compiler_params=_cparams("parallel"),
        name="adaln",
    )(cc, w, b)


def _nmm_kernel(x_ref, g_ref, sh_ref, sc_ref, w_ref, o_ref, h_ref, *, sq_relu):
    @pl.when(pl.program_id(1) == 0)
    def _():
        x = x_ref[...]
        ms = jnp.mean(x * x, axis=-1, keepdims=True)
        y = x * lax.rsqrt(ms + NORM_EPS) * g_ref[...]
        h_ref[...] = (y * (1.0 + sc_ref[0]) + sh_ref[0]).astype(BF16)

    acc = _dot(h_ref[...], w_ref[...])
    if sq_relu:
        acc = jnp.square(jnp.maximum(acc, 0.0))
    o_ref[...] = acc.astype(o_ref.dtype)


def _norm_mod_matmul(x, g, shift, scale, w, rows_per_mod, out_dtype, sq_relu, tm_pref, tn_pref, col0=0):
    m, k = x.shape
    n = w.shape[1] - col0
    tm = _pick(math.gcd(m, rows_per_mod), tm_pref, 16)
    tn = _pick(math.gcd(n, col0) if col0 else n, tn_pref, LANES)
    mod_idx = lambda i, j: ((i * tm) // rows_per_mod, 0, 0)
    return pl.pallas_call(
        functools.partial(_nmm_kernel, sq_relu=sq_relu),
        grid=(m // tm, n // tn),
        in_specs=[pl.BlockSpec((tm, k), lambda i, j: (i, 0)),
                  pl.BlockSpec((1, k), lambda i, j: (0, 0)),
                  pl.BlockSpec((1, 1, k), mod_idx),
                  pl.BlockSpec((1, 1, k), mod_idx),
                  pl.BlockSpec((k, tn), lambda i, j: (0, col0 // tn + j))],
        out_specs=pl.BlockSpec((tm, tn), lambda i, j: (i, j)),
        out_shape=jax.ShapeDtypeStruct((m, n), out_dtype),
        scratch_shapes=[pltpu.VMEM((tm, k), BF16)],
        compiler_params=_cparams("parallel", "arbitrary"),
        name="norm_mod_matmul",
    )(x, g, shift, scale, w)


HALO = 16


def _nmm_conv_kernel(x_ref, xp_ref, xn_ref, g_ref, sh_ref, sc_ref, w_ref, cw_ref, cb_ref, o_ref, h_ref,
                     *, tm, rows_per_seq):
    i = pl.program_id(0)

    def normed(x):
        ms = jnp.mean(x * x, axis=-1, keepdims=True)
        y = x * lax.rsqrt(ms + NORM_EPS) * g_ref[...]
        return y * (1.0 + sc_ref[0]) + sh_ref[0]

    @pl.when(pl.program_id(1) == 0)
    def _():
        first = (i * tm) % rows_per_seq == 0
        last = ((i + 1) * tm) % rows_per_seq == 0
        h_ref[0:HALO, :] = jnp.where(first, 0.0, normed(xp_ref[...])).astype(BF16)
        h_ref[HALO:HALO + tm, :] = normed(x_ref[...]).astype(BF16)
        h_ref[HALO + tm:, :] = jnp.where(last, 0.0, normed(xn_ref[...])).astype(BF16)

    acc = _dot(h_ref[...], w_ref[...])
    rows = acc.shape[0]
    mid = acc[HALO:HALO + tm]
    up = pltpu.roll(acc, 1, 0)[HALO:HALO + tm]
    dn = pltpu.roll(acc, rows - 1, 0)[HALO:HALO + tm]
    o_ref[...] = up * cw_ref[0:1, :] + mid * cw_ref[1:2, :] + dn * cw_ref[2:3, :] + cb_ref[...]


def _norm_mod_matmul_conv(x, g, shift, scale, w, conv_w, conv_b, rows_per_seq, tm_pref, tn_pref):
    m, k = x.shape
    n = w.shape[1]
    tm = _pick(rows_per_seq, tm_pref, HALO)
    tn = _pick(n, tn_pref, LANES)
    r = tm // HALO
    last = m // HALO - 1
    mod_idx = lambda i, j: ((i * tm) // rows_per_seq, 0, 0)
    return pl.pallas_call(
        functools.partial(_nmm_conv_kernel, tm=tm, rows_per_seq=rows_per_seq),
        grid=(m // tm, n // tn),
        in_specs=[pl.BlockSpec((tm, k), lambda i, j: (i, 0)),
                  pl.BlockSpec((HALO, k), lambda i, j: (jnp.maximum(i * r - 1, 0), 0)),
                  pl.BlockSpec((HALO, k), lambda i, j: (jnp.minimum((i + 1) * r, last), 0)),
                  pl.BlockSpec((1, k), lambda i, j: (0, 0)),
                  pl.BlockSpec((1, 1, k), mod_idx),
                  pl.BlockSpec((1, 1, k), mod_idx),
                  pl.BlockSpec((k, tn), lambda i, j: (0, j)),
                  pl.BlockSpec((3, tn), lambda i, j: (0, j)),
                  pl.BlockSpec((1, tn), lambda i, j: (0, j))],
        out_specs=pl.BlockSpec((tm, tn), lambda i, j: (i, j)),
        out_shape=jax.ShapeDtypeStruct((m, n), F32),
        scratch_shapes=[pltpu.VMEM((tm + 2 * HALO, k), BF16)],
        compiler_params=_cparams("parallel", "arbitrary"),
        name="norm_mod_matmul_conv",
    )(x, x, x, g, shift, scale, w, conv_w, conv_b)


def _outproj_kernel(a0_ref, a1_ref, w_ref, res_ref, gate_ref, o_ref, *, k0):
    acc = _dot(a0_ref[...], w_ref[0:k0, :]) + _dot(a1_ref[...], w_ref[k0:, :])
    o_ref[...] = res_ref[...] + gate_ref[0] * acc


def _out_proj(a0, a1, w, res, gate, rows_per_mod):
    m, k0 = a0.shape
    k1 = a1.shape[1]
    n = w.shape[1]
    tm = _pick(math.gcd(m, rows_per_mod), 1024, 16)
    tn = _pick(n, 1024, LANES)
    return pl.pallas_call(
        functools.partial(_outproj_kernel, k0=k0),
        grid=(m // tm, n // tn),
        in_specs=[pl.BlockSpec((tm, k0), lambda i, j: (i, 0)),
                  pl.BlockSpec((tm, k1), lambda i, j: (i, 0)),
                  pl.BlockSpec((k0 + k1, tn), lambda i, j: (0, j)),
                  pl.BlockSpec((tm, tn), lambda i, j: (i, j)),
                  pl.BlockSpec((1, 1, tn), lambda i, j: ((i * tm) // rows_per_mod, 0, j))],
        out_specs=pl.BlockSpec((tm, tn), lambda i, j: (i, j)),
        out_shape=jax.ShapeDtypeStruct((m, n), F32),
        compiler_params=_cparams("parallel", "parallel"),
        name="out_proj",
    )(a0, a1, w, res, gate)


def _mmres_kernel(a_ref, w_ref, res_ref, gate_ref, o_ref, acc_ref):
    kk = pl.program_id(2)

    @pl.when(kk == 0)
    def _():
        acc_ref[...] = jnp.zeros_like(acc_ref)

    acc_ref[...] += _dot(a_ref[...], w_ref[...])

    @pl.when(kk == pl.num_programs(2) - 1)
    def _():
        o_ref[...] = res_ref[...] + gate_ref[0] * acc_ref[...]


def _matmul_gated_residual(a, w, res, gate, rows_per_mod):
    m, k = a.shape
    n = w.shape[1]
    tm = _pick(math.gcd(m, rows_per_mod), 1024, 16)
    tn = _pick(n, 1024, LANES)
    tk = _pick(k, 2048, LANES)
    return pl.pallas_call(
        _mmres_kernel,
        grid=(m // tm, n // tn, k // tk),
        in_specs=[pl.BlockSpec((tm, tk), lambda i, j, kk: (i, kk)),
                  pl.BlockSpec((tk, tn), lambda i, j, kk: (kk, j)),
                  pl.BlockSpec((tm, tn), lambda i, j, kk: (i, j)),
                  pl.BlockSpec((1, 1, tn), lambda i, j, kk: ((i * tm) // rows_per_mod, 0, j))],
        out_specs=pl.BlockSpec((tm, tn), lambda i, j, kk: (i, j)),
        out_shape=jax.ShapeDtypeStruct((m, n), F32),
        scratch_shapes=[pltpu.VMEM((tm, tn), F32)],
        compiler_params=_cparams("parallel", "parallel", "arbitrary"),
        name="matmul_gated_residual",
    )(a, w, res, gate)


def _filt_kernel(feat_ref, t_ref, w1_ref, b1_ref, w2_ref, b2_ref, fr_ref, w3_ref, dl_ref, o_ref, h_ref, *, seq):
    i = pl.program_id(0)
    j = pl.program_id(1)

    @pl.when(j == 0)
    def _():
        fr = fr_ref[...]
        h = jnp.sin(fr * (_dot(feat_ref[...].astype(BF16), w1_ref[...]) + b1_ref[...]))
        h = jnp.sin(fr * (_dot(h.astype(BF16), w2_ref[...]) + b2_ref[...]))
        h_ref[...] = h.astype(BF16)

    f = _dot(h_ref[...], w3_ref[...]) * jnp.exp(-t_ref[...] * dl_ref[...])
    rows = lax.broadcasted_iota(jnp.int32, f.shape, 0) + i * f.shape[0]
    o_ref[0] = jnp.where(rows == seq, 0.0, f).astype(o_ref.dtype)


def _hyena_filters(seq, w1, b1, w2, b2, w3, freq, width):
    two_sided = lambda a: np.concatenate([a, a[:1], a[:0:-1]], axis=0)
    pos = np.arange(seq, dtype=np.float32)
    t = np.linspace(0.0, 1.0, seq, dtype=np.float32)[:, None]
    bands = np.linspace(1e-4, HY_POS_BANDS - 1, HY_POS_BANDS, dtype=np.float32)
    ang = np.float32(2.0 * math.pi / seq) * pos[:, None] * bands[None, :]
    feats = np.concatenate([t, np.cos(ang), -np.sin(ang)], axis=-1).astype(np.float32)
    pd = feats.shape[1]
    hid = w1.shape[1]
    hp = LANES
    feats = jnp.asarray(two_sided(np.pad(feats, ((0, 0), (0, hp - pd)))))
    t = jnp.asarray(two_sided(t))
    w1p = jnp.pad(w1, ((0, hp - pd), (0, hp - hid))).astype(BF16)
    w2p = jnp.pad(w2, ((0, hp - hid), (0, hp - hid))).astype(BF16)
    w3p = jnp.pad(w3, ((0, hp - hid), (0, 0))).astype(BF16)
    b1p = jnp.pad(b1, (0, hp - hid))[None, :]
    b2p = jnp.pad(b2, (0, hp - hid))[None, :]
    frp = jnp.pad(freq, (0, hp - hid))[None, :]
    deltas = np.abs(np.linspace(math.log(HY_DECAY_TARGET) / HY_SLOW_DECAY_PCT,
                                math.log(HY_DECAY_TARGET) / HY_FAST_DECAY_PCT, width, dtype=np.float32))
    ncol = w3.shape[1] // 2
    dl = jnp.asarray(np.tile(deltas, ncol // width)[None, :])
    tl = _pick(seq, 512, SUBLANES)
    tn = _pick(ncol, 2048, LANES)
    fwd_tiles = seq // tl
    return pl.pallas_call(
        functools.partial(_filt_kernel, seq=seq),
        grid=(2 * seq // tl, ncol // tn),
        in_specs=[pl.BlockSpec((tl, hp), lambda i, j: (i, 0)),
                  pl.BlockSpec((tl, 1), lambda i, j: (i, 0)),
                  pl.BlockSpec((hp, hp), lambda i, j: (0, 0)),
                  pl.BlockSpec((1, hp), lambda i, j: (0, 0)),
                  pl.BlockSpec((hp, hp), lambda i, j: (0, 0)),
                  pl.BlockSpec((1, hp), lambda i, j: (0, 0)),
                  pl.BlockSpec((1, hp), lambda i, j: (0, 0)),
                  pl.BlockSpec((hp, tn), lambda i, j: (0, (i // fwd_tiles) * (ncol // tn) + j)),
                  pl.BlockSpec((1, tn), lambda i, j: (0, j))],
        out_specs=pl.BlockSpec((1, tl, tn), lambda i, j: (0, i, j)),
        out_shape=jax.ShapeDtypeStruct((1, 2 * seq, ncol), BF16),
        scratch_shapes=[pltpu.VMEM((tl, hp), BF16)],
        compiler_params=_cparams("parallel", "arbitrary"),
        name="hyena_filters",
    )(feats, t, w1p, b1p, w2p, b2p, frp, w3p, dl)


K1_GROUP = SUBLANES


def _dft_tables(n1, n2):
    n = n1 * n2
    hl = n1 // 2
    k1 = np.arange(n1)[None, :, None]
    l1 = np.arange(hl)[None, None, :]
    l2 = np.arange(n2)[:, None, None]
    ang = -2.0 * np.pi * ((k1 * (n2 * l1 + l2)) % n) / n
    gr, gi = np.cos(ang), np.sin(ang)
    gm = np.concatenate([np.concatenate([gr, -gi], axis=2), np.concatenate([gi, gr], axis=2)], axis=1)
    a2 = -2.0 * np.pi * ((np.arange(n2)[:, None] * np.arange(n2)[None, :]) % n2) / n2
    fr, fi = np.cos(a2), np.sin(a2)
    f2 = np.block([[fr, -fi], [fi, fr]])
    if2 = np.block([[fr, fi], [-fi, fr]])
    ir, ii = np.transpose(gr, (0, 2, 1)) / n, -np.transpose(gi, (0, 2, 1)) / n
    igm = np.concatenate([np.concatenate([ir, -ii], axis=2), np.concatenate([ii, ir], axis=2)], axis=1)
    angf = -2.0 * np.pi * ((k1 * (n2 * np.arange(n1)[None, None, :] + l2)) % n) / n
    gmr = np.concatenate([np.cos(angf), np.sin(angf)], axis=1)
    to = lambda a: jnp.asarray(a.astype(np.float32)).astype(BF16)
    return to(gm), to(f2), to(if2), to(igm), to(gmr)


def _fft_split(seq):
    n = 2 * seq
    n2 = LANES if n % (LANES * 2 * SUBLANES) == 0 and n // LANES >= 2 * SUBLANES else 2 * SUBLANES
    n1 = n // n2
    assert n1 * n2 == n and n1 % (2 * SUBLANES) == 0 and n2 % SUBLANES == 0, (n1, n2)
    return n1, n2


def _fill_pitched(src_ref, p_ref, *, nb, rows, n2, pitch):
    def body(l1, carry):
        for b in range(nb):
            rows = src_ref[b, pl.ds(pl.multiple_of(l1 * n2, n2), n2), :]
            p_ref[b, pl.ds(l1, n2, stride=pitch), :] = rows.astype(F32)
        return carry

    lax.fori_loop(0, rows, body, 0, unroll=2)


def _stage1_to_yv(p_ref, gm_ref, yv_ref, step, *, nb, n1, l2c, pitch):
    rows = n1 // nb

    def body(t, carry):
        l2 = step * l2c + t
        base = pl.multiple_of(l2 * pitch, SUBLANES)
        rhs = jnp.concatenate([p_ref[b, pl.ds(base, rows), :] for b in range(nb)], axis=0)
        res = _dot(gm_ref[t], rhs.astype(BF16))
        row0 = pl.multiple_of(l2 * K1_GROUP, K1_GROUP)
        for g in range(n1 // K1_GROUP):
            for ri in range(2):
                lo = ri * n1 + g * K1_GROUP
                yv_ref[g * 2 + ri, pl.ds(row0, K1_GROUP), :] = res[lo:lo + K1_GROUP]
        return carry

    lax.fori_loop(0, l2c, body, 0, unroll=8)


def _yv_pair(yv_ref, g, t, n2):
    rows = lambda tt: pl.ds(tt, n2, stride=K1_GROUP)
    re = jnp.concatenate([yv_ref[g * 2, rows(t), :], yv_ref[g * 2, rows(t + 1), :]], axis=1)
    im = jnp.concatenate([yv_ref[g * 2 + 1, rows(t), :], yv_ref[g * 2 + 1, rows(t + 1), :]], axis=1)
    return jnp.concatenate([re, im], axis=0).astype(BF16)


def _filtspec_kernel(filt_ref, gmr_ref, f2_ref, o_ref, yv_ref, p_ref, *, n1, n2, l2c, g2, s1, pitch):
    s = pl.program_id(1)

    @pl.when(s == 0)
    def _():
        _fill_pitched(filt_ref, p_ref, nb=1, rows=n1, n2=n2, pitch=pitch)

    @pl.when(s < s1)
    def _():
        _stage1_to_yv(p_ref, gmr_ref, yv_ref, s, nb=1, n1=n1, l2c=l2c, pitch=pitch)

    @pl.when(s >= s1)
    def _():
        for gg in range(g2):
            g = (s - s1) * g2 + gg
            for t in range(0, K1_GROUP, 2):
                x = _dot(f2_ref[...], _yv_pair(yv_ref, g, t, n2))
                for d in range(2):
                    o_ref[gg * K1_GROUP + t + d, 0] = x[:n2, d * LANES:(d + 1) * LANES].astype(o_ref.dtype)
                    o_ref[gg * K1_GROUP + t + d, 1] = x[n2:, d * LANES:(d + 1) * LANES].astype(o_ref.dtype)


def _filter_spectrum(filt, gmr, f2, n1, n2):
    c = filt.shape[-1]
    ng = n1 // K1_GROUP
    l2c = _pick(n2, 32, 4)
    g2 = _pick(ng, 4, 1)
    s1, s2 = n2 // l2c, ng // g2
    pitch = n1 + SUBLANES
    return pl.pallas_call(
        functools.partial(_filtspec_kernel, n1=n1, n2=n2, l2c=l2c, g2=g2, s1=s1, pitch=pitch),
        grid=(c // LANES, s1 + s2),
        in_specs=[pl.BlockSpec((1, n1 * n2, LANES), lambda cb, s: (0, 0, cb), pipeline_mode=pl.Buffered(1)),
                  pl.BlockSpec((l2c, 2 * n1, n1), lambda cb, s: (jnp.minimum(s, s1 - 1), 0, 0)),
                  pl.BlockSpec((2 * n2, 2 * n2), lambda cb, s: (0, 0))],
        out_specs=pl.BlockSpec((g2 * K1_GROUP, 2, n2, LANES), lambda cb, s: (jnp.maximum(s - s1, 0), 0, 0, cb)),
        out_shape=jax.ShapeDtypeStruct((n1, 2, n2, c), BF16),
        scratch_shapes=[pltpu.VMEM((2 * ng, n2 * K1_GROUP, LANES), F32), pltpu.VMEM((1, n2 * pitch, LANES), F32)],
        compiler_params=_cparams("parallel", "arbitrary"),
        name="filter_spectrum",
    )(filt, gmr, f2)


def _fftconv_kernel(sig_ref, gm_ref, k_ref, f2_ref, if2_ref, igm_ref, gate_ref, bias_ref, o_ref, yv_ref, p_ref,
                    *, n1, n2, l2c, g2, l3, rows4, s1, s2, s3, pitch):
    s = pl.program_id(1)
    hl = n1 // 2
    ng = n1 // K1_GROUP

    @pl.when(s == 0)
    def _():
        _fill_pitched(sig_ref, p_ref, nb=2, rows=hl, n2=n2, pitch=pitch)

    @pl.when(s < s1)
    def _():
        _stage1_to_yv(p_ref, gm_ref, yv_ref, s, nb=2, n1=n1, l2c=l2c, pitch=pitch)

    @pl.when((s >= s1) & (s < s1 + s2))
    def _():
        for gg in range(g2):
            g = (s - s1) * g2 + gg
            rows = lambda tt: pl.ds(tt, n2, stride=K1_GROUP)
            for t in range(0, K1_GROUP, 2):
                x = _dot(f2_ref[...], _yv_pair(yv_ref, g, t, n2))
                xr, xi = x[:n2], x[n2:]
                kt = gg * K1_GROUP + t
                kr = jnp.concatenate([k_ref[kt, 0], k_ref[kt + 1, 0]], axis=1).astype(F32)
                ki = jnp.concatenate([k_ref[kt, 1], k_ref[kt + 1, 1]], axis=1).astype(F32)
                p = jnp.concatenate([xr * kr - xi * ki, xr * ki + xi * kr], axis=0).astype(BF16)
                yp = _dot(if2_ref[...], p)
                for d in range(2):
                    for ri in range(2):
                        yv_ref[g * 2 + ri, rows(t + d), :] = yp[ri * n2:(ri + 1) * n2, d * LANES:(d + 1) * LANES]

    @pl.when((s >= s1 + s2) & (s < s1 + s2 + s3))
    def _():
        def body(i, carry):
            l2 = (s - s1 - s2) * (l3 * SUBLANES) + i
            row0 = pl.multiple_of(l2 * K1_GROUP, K1_GROUP)
            tiles = [yv_ref[g * 2 + ri, pl.ds(row0, K1_GROUP), :] for ri in range(2) for g in range(ng)]
            res = _dot(igm_ref[i], jnp.concatenate(tiles, axis=0).astype(BF16))
            base = pl.multiple_of(l2 * pitch, SUBLANES)
            for b in range(2):
                p_ref[b, pl.ds(base, hl), :] = res[b * hl:(b + 1) * hl]
            return carry

        lax.fori_loop(0, l3 * SUBLANES, body, 0, unroll=8)

    @pl.when(s >= s1 + s2 + s3)
    def _():
        r = s - s1 - s2 - s3
        for j in range(rows4 // n2):
            l1 = r * (rows4 // n2) + j
            nat = pl.ds(pl.multiple_of(l1 * n2, n2), n2)
            for b in range(2):
                conv = p_ref[b, pl.ds(l1, n2, stride=pitch), :] + bias_ref[...] * sig_ref[b, nat, :]
                o_ref[b, j * n2:(j + 1) * n2, :] = (gate_ref[b, j * n2:(j + 1) * n2, :] * conv).astype(o_ref.dtype)


def _fft_conv_gated(sig, kspec, kcol_block0, gate, gate_col_block0, bias, tables, n1, n2, out_dtype):
    gm, f2, if2, igm = tables
    c = bias.shape[1]
    seq = sig.shape[1]
    ng = n1 // K1_GROUP
    l2c = _pick(n2, 32, 4)
    g2 = _pick(ng, 4, 1)
    l3 = _pick(n2 // SUBLANES, 4, 1)
    rows4 = _pick(seq, 2048, n2)
    s1, s2, s3, s4 = n2 // l2c, ng // g2, n2 // (l3 * SUBLANES), seq // rows4
    pitch = n1 // 2 + SUBLANES
    clamp = lambda v, n: jnp.clip(v, 0, n - 1)
    return pl.pallas_call(
        functools.partial(_fftconv_kernel, n1=n1, n2=n2, l2c=l2c, g2=g2, l3=l3, rows4=rows4, s1=s1, s2=s2, s3=s3,
                          pitch=pitch),
        grid=(c // LANES, s1 + s2 + s3 + s4),
        in_specs=[pl.BlockSpec((2, seq, LANES), lambda cb, s: (0, 0, cb), pipeline_mode=pl.Buffered(1)),
                  pl.BlockSpec((l2c, 2 * n1, n1), lambda cb, s: (clamp(s, s1), 0, 0)),
                  pl.BlockSpec((g2 * K1_GROUP, 2, n2, LANES),
                               lambda cb, s: (clamp(s - s1, s2), 0, 0, kcol_block0 + cb)),
                  pl.BlockSpec((2 * n2, 2 * n2), lambda cb, s: (0, 0)),
                  pl.BlockSpec((2 * n2, 2 * n2), lambda cb, s: (0, 0)),
                  pl.BlockSpec((l3 * SUBLANES, n1, 2 * n1), lambda cb, s: (clamp(s - s1 - s2, s3), 0, 0)),
                  pl.BlockSpec((2, rows4, LANES),
                               lambda cb, s: (0, clamp(s - s1 - s2 - s3, s4), gate_col_block0 + cb)),
                  pl.BlockSpec((1, LANES), lambda cb, s: (0, cb))],
        out_specs=pl.BlockSpec((2, rows4, LANES), lambda cb, s: (0, clamp(s - s1 - s2 - s3, s4), cb)),
        out_shape=jax.ShapeDtypeStruct((2, seq, c), out_dtype),
        scratch_shapes=[pltpu.VMEM((2 * ng, n2 * K1_GROUP, LANES), F32), pltpu.VMEM((2, n2 * pitch, LANES), F32)],
        compiler_params=_cparams("parallel", "arbitrary"),
        name="fft_conv_gated",
    )(sig, gm, kspec, f2, if2, igm, gate, bias)


def _hyena(u, fw1, fb1, fw2, fb2, fw3, freq, hy_bias):
    bsz, seq, _ = u.shape
    assert bsz == 2, "the FFT convolution packs exactly two batch elements into one complex signal"
    width = hy_bias.shape[1]
    n1, n2 = _fft_split(seq)
    gm, f2, if2, igm, gmr = _dft_tables(n1, n2)

    filt = _hyena_filters(seq, fw1, fb1, fw2, fb2, fw3, freq, width)
    kspec = _filter_spectrum(filt, gmr, f2, n1, n2)

    tables = (gm, f2, if2, igm)
    wb = width // LANES
    z = _fft_conv_gated(u, kspec, 0, u, wb, hy_bias[0:1], tables, n1, n2, F32)
    return _fft_conv_gated(z, kspec, wb, u, 2 * wb, hy_bias[1:2], tables, n1, n2, BF16)


HEAD_LANES = 2 * LANES


ROPE_NF = MLA_ROPE // 4


def _head_layout(w, heads, nope):
    r = w.shape[0]
    w3 = jnp.pad(w.reshape(r, heads, nope + MLA_ROPE), ((0, 0), (0, 0), (0, HEAD_LANES - nope - MLA_ROPE)))
    return w3.reshape(r, heads * HEAD_LANES)


def _rope_tables(seq):
    rows = seq // GRID_W
    row = np.repeat(np.arange(rows, dtype=np.float32), GRID_W)
    col = np.tile(np.arange(GRID_W, dtype=np.float32), rows)
    half = MLA_ROPE // 2
    inv = (np.float32(ROPE_THETA) ** (-np.arange(0, half, 2, dtype=np.float32) / np.float32(half))).astype(np.float32)
    zero = np.zeros((seq, ROPE_NF), np.float32)
    cos, sin_lo, sin_hi = [], [], []
    for pos in (row, col):
        ang = pos[:, None] * inv
        cos += [np.cos(ang), np.cos(ang)]
        sin_lo += [-np.sin(ang), zero]
        sin_hi += [zero, np.sin(ang)]
    pad = [np.zeros((seq, LANES - MLA_ROPE), np.float32)]
    return tuple(jnp.asarray(np.concatenate(t + pad, axis=-1).astype(np.float32)) for t in (cos, sin_lo, sin_hi))


def _rope(t, cos, sin_lo, sin_hi):
    return t * cos + pltpu.roll(t, LANES - ROPE_NF, 1) * sin_lo + pltpu.roll(t, ROPE_NF, 1) * sin_hi


def _rms(x, g):
    return x * lax.rsqrt(jnp.mean(x * x, axis=-1, keepdims=True) + NORM_EPS) * g


def _q_kernel(qa_ref, g_ref, w_ref, gn_ref, cos_ref, slo_ref, shi_ref, o_ref, *, heads, qk_dim, out_scale):
    xn = _rms(qa_ref[...], g_ref[...]).astype(BF16)
    q = _dot(xn, w_ref[...])
    gn = gn_ref[...]
    cos, slo, shi = cos_ref[...], slo_ref[...], shi_ref[...]
    for h in range(heads):
        qh = q[:, h * HEAD_LANES:(h + 1) * HEAD_LANES]
        inv = lax.rsqrt(jnp.sum(qh * qh, axis=-1, keepdims=True) / qk_dim + NORM_EPS)
        qn = qh * inv * gn
        o_ref[0, h, :, 0:LANES] = (qn[:, :LANES] * out_scale).astype(BF16)
        o_ref[0, h, :, LANES:] = (_rope(qn[:, LANES:], cos, slo, shi) * out_scale).astype(BF16)


def _mla_queries(proj, col_block, g_qa, w_q, gn, rope, heads, qk_dim, out_scale):
    bsz, seq, _ = proj.shape
    r = g_qa.shape[1]
    tm = _pick(seq, 512, 16)
    return pl.pallas_call(
        functools.partial(_q_kernel, heads=heads, qk_dim=qk_dim, out_scale=out_scale),
        grid=(bsz, seq // tm),
        in_specs=[pl.BlockSpec((None, tm, r), lambda b, i: (b, i, col_block)),
                  pl.BlockSpec((1, r), lambda b, i: (0, 0)),
                  pl.BlockSpec((r, heads * HEAD_LANES), lambda b, i: (0, 0)),
                  pl.BlockSpec((1, HEAD_LANES), lambda b, i: (0, 0))]
                 + [pl.BlockSpec((tm, LANES), lambda b, i: (i, 0))] * 3,
        out_specs=pl.BlockSpec((1, heads, tm, HEAD_LANES), lambda b, i: (b, 0, i, 0)),
        out_shape=jax.ShapeDtypeStruct((bsz, heads, seq, HEAD_LANES), BF16),
        compiler_params=_cparams("parallel", "parallel"),
        name="mla_queries",
    )(proj, g_qa, w_q, gn, *rope)


V_ROWS = LANES + 16


def _kv_kernel(kva_ref, kr_ref, g_ref, wk_ref, wvt_ref, gn_ref, *rest, heads, qk_dim, rope):
    if rope:
        cos_ref, slo_ref, shi_ref, k_ref, v_ref = rest
    else:
        k_ref, v_ref = rest
    xn = _rms(kva_ref[...], g_ref[...]).astype(BF16)
    kk = _dot(xn, wk_ref[...])
    vt = lax.dot_general(wvt_ref[...], xn, (((1,), (1,)), ((), ())), preferred_element_type=F32)
    gn = gn_ref[...]
    kr = kr_ref[...]
    ssr = jnp.sum(kr * kr, axis=-1, keepdims=True)
    krg = kr * gn[:, LANES:]
    if rope:
        krg = _rope(krg, cos_ref[...], slo_ref[...], shi_ref[...])
    ones_row = (lax.broadcasted_iota(jnp.int32, (V_ROWS - LANES, kr.shape[0]), 0) == 0).astype(BF16)
    for h in range(heads):
        kn = kk[:, h * LANES:(h + 1) * LANES]
        inv = lax.rsqrt((jnp.sum(kn * kn, axis=-1, keepdims=True) + ssr) / qk_dim + NORM_EPS)
        k_ref[0, h, :, 0:LANES] = (kn * inv * gn[:, :LANES]).astype(BF16)
        k_ref[0, h, :, LANES:] = (krg * inv).astype(BF16)
        v_ref[0, h, 0:LANES, :] = vt[h * LANES:(h + 1) * LANES].astype(BF16)
        v_ref[0, h, LANES:, :] = ones_row


def _mla_keys_values(proj, kv_block, kr_block, g_kva, w_k, w_vt, gn, rope, heads, qk_dim):
    bsz, seq, _ = proj.shape
    r = g_kva.shape[1]
    tm = _pick(seq, 512, LANES)
    in_specs = [pl.BlockSpec((None, tm, r), lambda b, i: (b, i, kv_block)),
                pl.BlockSpec((None, tm, LANES), lambda b, i: (b, i, kr_block)),
                pl.BlockSpec((1, r), lambda b, i: (0, 0)),
                pl.BlockSpec((r, heads * LANES), lambda b, i: (0, 0)),
                pl.BlockSpec((heads * LANES, r), lambda b, i: (0, 0)),
                pl.BlockSpec((1, HEAD_LANES), lambda b, i: (0, 0))]
    args = [proj, proj, g_kva, w_k, w_vt, gn]
    if rope is not None:
        in_specs += [pl.BlockSpec((tm, LANES), lambda b, i: (i, 0))] * 3
        args += list(rope)
    return pl.pallas_call(
        functools.partial(_kv_kernel, heads=heads, qk_dim=qk_dim, rope=rope is not None),
        grid=(bsz, seq // tm),
        in_specs=in_specs,
        out_specs=[pl.BlockSpec((1, heads, tm, HEAD_LANES), lambda b, i: (b, 0, i, 0)),
                   pl.BlockSpec((1, heads, V_ROWS, tm), lambda b, i: (b, 0, 0, i))],
        out_shape=[jax.ShapeDtypeStruct((bsz, heads, seq, HEAD_LANES), BF16),
                   jax.ShapeDtypeStruct((bsz, heads, V_ROWS, seq), BF16)],
        compiler_params=_cparams("parallel", "parallel"),
        name="mla_keys_values",
    )(*args)


ATTN_HEADS_PER_STEP = 2
ATTN_LOOKAHEAD = 2


def _attn_kernel(q_ref, k_ref, v_ref, kc_ref, vc_ref, o_ref, m_ref, acc_ref, *, hps, kchunk):
    kj = pl.program_id(3)

    def scores(g, k):
        return lax.dot_general(k, q_ref[0, g], (((1,), (1,)), ((), ())), preferred_element_type=F32)

    def update(g, st, vt):
        m_old = m_ref[g]
        m_new = jnp.maximum(m_old, jnp.max(st, axis=0, keepdims=True))
        p = jnp.exp2(st - m_new)
        acc_ref[g] = jnp.exp2(m_old - m_new) * acc_ref[g] + _dot(vt, p.astype(BF16))
        m_ref[g] = m_new

    def sweep(units):
        ahead = [scores(g, k()) for g, k, _ in units[:ATTN_LOOKAHEAD]]
        for i, (g, _, vt) in enumerate(units):
            st = ahead.pop(0)
            if i + ATTN_LOOKAHEAD < len(units):
                nxt = units[i + ATTN_LOOKAHEAD]
                ahead.append(scores(nxt[0], nxt[1]()))
            update(g, st, vt())

    @pl.when(kj == 0)
    def _():
        m_ref[...] = jnp.full_like(m_ref, -jnp.inf)
        acc_ref[...] = jnp.zeros_like(acc_ref)
        sweep([(g, functools.partial(lambda g: kc_ref[0, g], g), functools.partial(lambda g: vc_ref[0, g], g))
               for g in range(hps)])

    tk = k_ref.shape[2]
    sweep([(g,
            functools.partial(lambda g, c: k_ref[0, g, c * kchunk:(c + 1) * kchunk, :], g, c),
            functools.partial(lambda g, c: v_ref[0, g, :, c * kchunk:(c + 1) * kchunk], g, c))
           for c in range(tk // kchunk) for g in range(hps)])

    @pl.when(kj == pl.num_programs(3) - 1)
    def _():
        for g in range(hps):
            acc = acc_ref[g]
            out_t = acc[:LANES] / acc[LANES:LANES + 1]
            o_ref[0, :, g * LANES:(g + 1) * LANES] = out_t.T.astype(o_ref.dtype)


def _attention(q, k, vt, kc, vct, tq_pref, tk_pref, kchunk_pref):
    bsz, heads, seq, _ = q.shape
    lc = kc.shape[2]
    hps = ATTN_HEADS_PER_STEP if heads % ATTN_HEADS_PER_STEP == 0 else 1
    tq = _pick(seq, tq_pref, LANES)
    tk = _pick(seq, tk_pref, LANES)
    kchunk = _pick(tk, kchunk_pref, LANES)
    return pl.pallas_call(
        functools.partial(_attn_kernel, hps=hps, kchunk=kchunk),
        grid=(bsz, heads // hps, seq // tq, seq // tk),
        in_specs=[pl.BlockSpec((1, hps, tq, HEAD_LANES), lambda b, h, i, j: (b, h, i, 0)),
                  pl.BlockSpec((1, hps, tk, HEAD_LANES), lambda b, h, i, j: (b, h, j, 0)),
                  pl.BlockSpec((1, hps, V_ROWS, tk), lambda b, h, i, j: (b, h, 0, j)),
                  pl.BlockSpec((1, hps, lc, HEAD_LANES), lambda b, h, i, j: (b, h, 0, 0)),
                  pl.BlockSpec((1, hps, V_ROWS, lc), lambda b, h, i, j: (b, h, 0, 0))],
        out_specs=pl.BlockSpec((1, tq, hps * LANES), lambda b, h, i, j: (b, i, h)),
        out_shape=jax.ShapeDtypeStruct((bsz, seq, heads * LANES), BF16),
        scratch_shapes=[pltpu.VMEM((hps, 1, tq), F32), pltpu.VMEM((hps, V_ROWS, tq), F32)],
        compiler_params=_cparams("parallel", "parallel", "parallel", "arbitrary"),
        name="attention",
    )(q, k, vt, kc, vct)


def kernel(x, c, ctx, c_ctx, norm1_g, norm2_g, w_ada, b_ada, w_in, hy_conv_w, hy_conv_b, hy_filt_w1, hy_filt_b1, hy_filt_w2, hy_filt_b2, hy_filt_w3, hy_freq, hy_bias, mla_g_qa, mla_w_qb, mla_g_kva, mla_w_kvb, mla_q_norm_g, mla_k_norm_g, w_out, w_mlp1, w_mlp2):
    assert w_ada.shape[0] == 1, "single-layer block"
    bsz, seq, d = x.shape
    lc = ctx.shape[1]
    hyc = hy_conv_b.shape[1]
    width = hy_bias.shape[2]
    q_lora = mla_g_qa.shape[1]
    kv_lora = mla_g_kva.shape[1]
    qk_dim = mla_q_norm_g.shape[1]
    nope = qk_dim - MLA_ROPE
    heads = mla_w_qb.shape[2] // qk_dim
    v_dim = mla_w_kvb.shape[2] // heads - nope
    assert nope == LANES and v_dim == LANES and seq % GRID_W == 0
    q0, kv0, kr0 = hyc, hyc + q_lora, hyc + q_lora + kv_lora
    assert q0 % q_lora == 0 and kv0 % kv_lora == 0 and kr0 % LANES == 0

    cc = jnp.zeros((8, d), F32).at[:bsz].set(c).at[bsz].set(c_ctx)
    mod = _adaln(cc, w_ada[0], b_ada)
    chunk = lambda i: mod[:bsz, i * d:(i + 1) * d][:, None, :]
    sh1, sc1, g1, sh2, sc2, g2 = [chunk(i) for i in range(6)]
    csh1 = mod[bsz:bsz + 1, 0:d][:, None, :]
    csc1 = mod[bsz:bsz + 1, d:2 * d][:, None, :]

    w_in0 = w_in[0]
    np_cols = kr0 + LANES
    np_cols += (-np_cols) % 1024
    w_pad = jnp.pad(w_in0, ((0, 0), (0, np_cols - w_in0.shape[1]))).astype(BF16)
    w_q = _head_layout(mla_w_qb[0], heads, nope).astype(BF16)
    w_kv = mla_w_kvb[0].reshape(kv_lora, heads, nope + v_dim).astype(BF16)
    w_k = w_kv[..., :nope].reshape(kv_lora, heads * nope)
    w_vt = w_kv[..., nope:].reshape(kv_lora, heads * v_dim).T
    gq = _head_layout(mla_q_norm_g, 1, nope)
    gk = _head_layout(mla_k_norm_g, 1, nope)
    rope = _rope_tables(seq)

    ctx_cols = kv0 - kv0 % 1024
    proj_c = _norm_mod_matmul(ctx.reshape(bsz * lc, d), norm1_g, csh1, csc1, w_pad, bsz * lc, F32, False,
                              512, 1024, col0=ctx_cols).reshape(bsz, lc, np_cols - ctx_cols)
    k_c, v_c = _mla_keys_values(proj_c, (kv0 - ctx_cols) // kv_lora, (kr0 - ctx_cols) // LANES,
                                mla_g_kva, w_k, w_vt, gk, None, heads, qk_dim)

    x2 = x.reshape(bsz * seq, d)
    taps = jnp.zeros((3, np_cols), F32).at[1].set(1.0).at[:, :hyc].set(hy_conv_w[0])
    tap_bias = jnp.zeros((1, np_cols), F32).at[0, :hyc].set(hy_conv_b[0])
    proj = _norm_mod_matmul_conv(x2, norm1_g, sh1, sc1, w_pad, taps, tap_bias, seq, 512, 1024)
    proj = proj.reshape(bsz, seq, np_cols)
    y_hy = _hyena(proj, hy_filt_w1[0], hy_filt_b1[0], hy_filt_w2[0], hy_filt_b2[0], hy_filt_w3[0], hy_freq[0],
                  hy_bias[0])
    q = _mla_queries(proj, q0 // q_lora, mla_g_qa, w_q, gq, rope, heads, qk_dim, qk_dim ** -0.5 * math.log2(math.e))
    k, v = _mla_keys_values(proj, kv0 // kv_lora, kr0 // LANES, mla_g_kva, w_k, w_vt, gk, rope, heads, qk_dim)
    y_att = _attention(q, k, v, k_c, v_c, 2048, 2048, 512)

    x1 = _out_proj(y_hy.reshape(bsz * seq, width), y_att.reshape(bsz * seq, heads * v_dim),
                   w_out[0].astype(BF16), x2, g1, seq)
    hmid = _norm_mod_matmul(x1, norm2_g, sh2, sc2, w_mlp1[0].astype(BF16), seq, BF16, True, 512, 1024)
    out = _matmul_gated_residual(hmid, w_mlp2[0].astype(BF16), x1, g2, seq)
    return out.reshape(bsz, seq, d)
```

```python
import functools
import math

import numpy as np
import jax
import jax.numpy as jnp
from jax import lax
from jax.experimental import pallas as pl
from jax.experimental.pallas import tpu as pltpu

F32 = jnp.float32
BF16 = jnp.bfloat16

NORM_EPS = 1e-6
GRID_W = 64
MLA_ROPE = 64
ROPE_THETA = 10000.0
HY_POS_BANDS = 16
HY_DECAY_TARGET = 1e-2
HY_FAST_DECAY_PCT = 0.3
HY_SLOW_DECAY_PCT = 1.5

LANES = 128
SUBLANES = 8
VMEM_LIMIT = 56 * 1024 * 1024


def _cparams(*sem):
    return pltpu.CompilerParams(dimension_semantics=sem, vmem_limit_bytes=VMEM_LIMIT)


def _pick(dim, pref, align):
    t = min(pref, dim)
    t -= t % align
    while t >= align:
        if dim % t == 0:
            return t
        t -= align
    return dim


def _dot(a, b):
    return jnp.dot(a, b, preferred_element_type=F32)


def _adaln_kernel(c_ref, w_ref, b_ref, o_ref):
    c = c_ref[...]
    s = c * jax.nn.sigmoid(c)
    o_ref[...] = _dot(s.astype(BF16), w_ref[...].astype(BF16)) + b_ref[...]


def _adaln(cc, w, b):
    d, n = w.shape
    tn = _pick(n, 512, LANES)
    return pl.pallas_call(
        _adaln_kernel,
        grid=(n // tn,),
        in_specs=[pl.BlockSpec((8, d), lambda j: (0, 0)),
                  pl.BlockSpec((d, tn), lambda j: (0, j)),
                  pl.BlockSpec((1, tn), lambda j: (0, j))],
        out_specs=pl.BlockSpec((8, tn), lambda j: (0, j)),
        out_shape=jax.ShapeDtypeStruct((8, n), F32),
        compiler_params=_cparams("parallel"),
        name="adaln",
    )(cc, w, b)


NORM_ROWS = 16


def _norm_mod_rows(x_ref, gm_ref, sh_ref, h_ref, dst0, nrows, keep=None):
    def body(c, carry):
        r = pl.multiple_of(c * NORM_ROWS, NORM_ROWS)
        x = x_ref[pl.ds(r, NORM_ROWS), :]
        inv = lax.rsqrt(jnp.mean(x * x, axis=-1, keepdims=True) + NORM_EPS)
        y = x * inv * gm_ref[0] + sh_ref[0]
        if keep is not None:
            y = jnp.where(keep, y, 0.0)
        h_ref[pl.ds(dst0 + r, NORM_ROWS), :] = y.astype(BF16)
        return carry

    lax.fori_loop(0, nrows // NORM_ROWS, body, 0, unroll=2 if nrows > NORM_ROWS else 1)


def _nmm_kernel(x_ref, gm_ref, sh_ref, w_ref, o_ref, h_ref, *, sq_relu):
    @pl.when(pl.program_id(1) == 0)
    def _():
        _norm_mod_rows(x_ref, gm_ref, sh_ref, h_ref, 0, x_ref.shape[0])

    acc = _dot(h_ref[...], w_ref[...])
    if sq_relu:
        acc = jnp.square(jnp.maximum(acc, 0.0))
    o_ref[...] = acc.astype(o_ref.dtype)


def _norm_mod_matmul(x, g, shift, scale, w, rows_per_mod, out_dtype, sq_relu, tm_pref, tn_pref, col0=0):
    m, k = x.shape
    n = w.shape[1] - col0
    tm = _pick(math.gcd(m, rows_per_mod), tm_pref, 16)
    tn = _pick(math.gcd(n, col0) if col0 else n, tn_pref, LANES)
    mod_idx = lambda i, j: ((i * tm) // rows_per_mod, 0, 0)
    return pl.pallas_call(
        functools.partial(_nmm_kernel, sq_relu=sq_relu),
        grid=(m // tm, n // tn),
        in_specs=[pl.BlockSpec((tm, k), lambda i, j: (i, 0)),
                  pl.BlockSpec((1, 1, k), mod_idx),
                  pl.BlockSpec((1, 1, k), mod_idx),
                  pl.BlockSpec((k, tn), lambda i, j: (0, col0 // tn + j))],
        out_specs=pl.BlockSpec((tm, tn), lambda i, j: (i, j)),
        out_shape=jax.ShapeDtypeStruct((m, n), out_dtype),
        scratch_shapes=[pltpu.VMEM((tm, k), BF16)],
        compiler_params=_cparams("parallel", "arbitrary"),
        name="norm_mod_matmul",
    )(x, g * (1.0 + scale), shift, w)


HALO = 16


def _nmm_conv_kernel(x_ref, xp_ref, xn_ref, gm_ref, sh_ref, w_ref, cw_ref, cb_ref, o_ref, h_ref,
                     *, tm, rows_per_seq):
    i = pl.program_id(0)

    @pl.when(pl.program_id(1) == 0)
    def _():
        not_first = (i * tm) % rows_per_seq != 0
        not_last = ((i + 1) * tm) % rows_per_seq != 0
        _norm_mod_rows(xp_ref, gm_ref, sh_ref, h_ref, 0, HALO, keep=not_first)
        _norm_mod_rows(x_ref, gm_ref, sh_ref, h_ref, HALO, tm)
        _norm_mod_rows(xn_ref, gm_ref, sh_ref, h_ref, HALO + tm, HALO, keep=not_last)

    acc = _dot(h_ref[...], w_ref[...])
    rows = acc.shape[0]
    mid = acc[HALO:HALO + tm]
    up = pltpu.roll(acc, 1, 0)[HALO:HALO + tm]
    dn = pltpu.roll(acc, rows - 1, 0)[HALO:HALO + tm]
    o_ref[...] = up * cw_ref[0:1, :] + mid * cw_ref[1:2, :] + dn * cw_ref[2:3, :] + cb_ref[...]


def _norm_mod_matmul_conv(x, g, shift, scale, w, conv_w, conv_b, rows_per_seq, tm_pref, tn_pref):
    m, k = x.shape
    n = w.shape[1]
    tm = _pick(rows_per_seq, tm_pref, HALO)
    tn = _pick(n, tn_pref, LANES)
    r = tm // HALO
    last = m // HALO - 1
    mod_idx = lambda i, j: ((i * tm) // rows_per_seq, 0, 0)
    return pl.pallas_call(
        functools.partial(_nmm_conv_kernel, tm=tm, rows_per_seq=rows_per_seq),
        grid=(m // tm, n // tn),
        in_specs=[pl.BlockSpec((tm, k), lambda i, j: (i, 0)),
                  pl.BlockSpec((HALO, k), lambda i, j: (jnp.maximum(i * r - 1, 0), 0)),
                  pl.BlockSpec((HALO, k), lambda i, j: (jnp.minimum((i + 1) * r, last), 0)),
                  pl.BlockSpec((1, 1, k), mod_idx),
                  pl.BlockSpec((1, 1, k), mod_idx),
                  pl.BlockSpec((k, tn), lambda i, j: (0, j)),
                  pl.BlockSpec((3, tn), lambda i, j: (0, j)),
                  pl.BlockSpec((1, tn), lambda i, j: (0, j))],
        out_specs=pl.BlockSpec((tm, tn), lambda i, j: (i, j)),
        out_shape=jax.ShapeDtypeStruct((m, n), F32),
        scratch_shapes=[pltpu.VMEM((tm + 2 * HALO, k), BF16)],
        compiler_params=_cparams("parallel", "arbitrary"),
        name="norm_mod_matmul_conv",
    )(x, x, x, g * (1.0 + scale), shift, w, conv_w, conv_b)


def _outproj_kernel(a0_ref, a1_ref, w_ref, res_ref, gate_ref, o_ref, *, k0):
    acc = _dot(a0_ref[...], w_ref[0:k0, :]) + _dot(a1_ref[...], w_ref[k0:, :])
    o_ref[...] = res_ref[...] + gate_ref[0] * acc


def _out_proj(a0, a1, w, res, gate, rows_per_mod):
    m, k0 = a0.shape
    k1 = a1.shape[1]
    n = w.shape[1]
    tm = _pick(math.gcd(m, rows_per_mod), 1024, 16)
    tn = _pick(n, 1024, LANES)
    return pl.pallas_call(
        functools.partial(_outproj_kernel, k0=k0),
        grid=(m // tm, n // tn),
        in_specs=[pl.BlockSpec((tm, k0), lambda i, j: (i, 0)),
                  pl.BlockSpec((tm, k1), lambda i, j: (i, 0)),
                  pl.BlockSpec((k0 + k1, tn), lambda i, j: (0, j)),
                  pl.BlockSpec((tm, tn), lambda i, j: (i, j)),
                  pl.BlockSpec((1, 1, tn), lambda i, j: ((i * tm) // rows_per_mod, 0, j))],
        out_specs=pl.BlockSpec((tm, tn), lambda i, j: (i, j)),
        out_shape=jax.ShapeDtypeStruct((m, n), F32),
        compiler_params=_cparams("parallel", "parallel"),
        name="out_proj",
    )(a0, a1, w, res, gate)


def _mmres_kernel(a_ref, w_ref, res_ref, gate_ref, o_ref, acc_ref):
    kk = pl.program_id(2)

    @pl.when(kk == 0)
    def _():
        acc_ref[...] = jnp.zeros_like(acc_ref)

    acc_ref[...] += _dot(a_ref[...], w_ref[...])

    @pl.when(kk == pl.num_programs(2) - 1)
    def _():
        o_ref[...] = res_ref[...] + gate_ref[0] * acc_ref[...]


def _matmul_gated_residual(a, w, res, gate, rows_per_mod):
    m, k = a.shape
    n = w.shape[1]
    tm = _pick(math.gcd(m, rows_per_mod), 1024, 16)
    tn = _pick(n, 1024, LANES)
    tk = _pick(k, 2048, LANES)
    return pl.pallas_call(
        _mmres_kernel,
        grid=(m // tm, n // tn, k // tk),
        in_specs=[pl.BlockSpec((tm, tk), lambda i, j, kk: (i, kk)),
                  pl.BlockSpec((tk, tn), lambda i, j, kk: (kk, j)),
                  pl.BlockSpec((tm, tn), lambda i, j, kk: (i, j)),
                  pl.BlockSpec((1, 1, tn), lambda i, j, kk: ((i * tm) // rows_per_mod, 0, j))],
        out_specs=pl.BlockSpec((tm, tn), lambda i, j, kk: (i, j)),
        out_shape=jax.ShapeDtypeStruct((m, n), F32),
        scratch_shapes=[pltpu.VMEM((tm, tn), F32)],
        compiler_params=_cparams("parallel", "parallel", "arbitrary"),
        name="matmul_gated_residual",
    )(a, w, res, gate)


def _filt_kernel(feat_ref, t_ref, w1_ref, b1_ref, w2_ref, b2_ref, fr_ref, w3_ref, dl_ref, o_ref, h_ref, *, seq):
    i = pl.program_id(0)
    j = pl.program_id(1)

    @pl.when(j == 0)
    def _():
        fr = fr_ref[...]
        h = jnp.sin(fr * (_dot(feat_ref[...].astype(BF16), w1_ref[...]) + b1_ref[...]))
        h = jnp.sin(fr * (_dot(h.astype(BF16), w2_ref[...]) + b2_ref[...]))
        h_ref[...] = h.astype(BF16)

    f = _dot(h_ref[...], w3_ref[...]) * jnp.exp(-t_ref[...] * dl_ref[...])
    rows = lax.broadcasted_iota(jnp.int32, f.shape, 0) + i * f.shape[0]
    o_ref[0] = jnp.where(rows == seq, 0.0, f).astype(o_ref.dtype)


def _hyena_filters(seq, w1, b1, w2, b2, w3, freq, width):
    two_sided = lambda a: np.concatenate([a, a[:1], a[:0:-1]], axis=0)
    pos = np.arange(seq, dtype=np.float32)
    t = np.linspace(0.0, 1.0, seq, dtype=np.float32)[:, None]
    bands = np.linspace(1e-4, HY_POS_BANDS - 1, HY_POS_BANDS, dtype=np.float32)
    ang = np.float32(2.0 * math.pi / seq) * pos[:, None] * bands[None, :]
    feats = np.concatenate([t, np.cos(ang), -np.sin(ang)], axis=-1).astype(np.float32)
    pd = feats.shape[1]
    hid = w1.shape[1]
    hp = LANES
    feats = jnp.asarray(two_sided(np.pad(feats, ((0, 0), (0, hp - pd)))))
    t = jnp.asarray(two_sided(t))
    w1p = jnp.pad(w1, ((0, hp - pd), (0, hp - hid))).astype(BF16)
    w2p = jnp.pad(w2, ((0, hp - hid), (0, hp - hid))).astype(BF16)
    w3p = jnp.pad(w3, ((0, hp - hid), (0, 0))).astype(BF16)
    b1p = jnp.pad(b1, (0, hp - hid))[None, :]
    b2p = jnp.pad(b2, (0, hp - hid))[None, :]
    frp = jnp.pad(freq, (0, hp - hid))[None, :]
    deltas = np.abs(np.linspace(math.log(HY_DECAY_TARGET) / HY_SLOW_DECAY_PCT,
                                math.log(HY_DECAY_TARGET) / HY_FAST_DECAY_PCT, width, dtype=np.float32))
    ncol = w3.shape[1] // 2
    dl = jnp.asarray(np.tile(deltas, ncol // width)[None, :])
    tl = _pick(seq, 512, SUBLANES)
    tn = _pick(ncol, 2048, LANES)
    fwd_tiles = seq // tl
    return pl.pallas_call(
        functools.partial(_filt_kernel, seq=seq),
        grid=(2 * seq // tl, ncol // tn),
        in_specs=[pl.BlockSpec((tl, hp), lambda i, j: (i, 0)),
                  pl.BlockSpec((tl, 1), lambda i, j: (i, 0)),
                  pl.BlockSpec((hp, hp), lambda i, j: (0, 0)),
                  pl.BlockSpec((1, hp), lambda i, j: (0, 0)),
                  pl.BlockSpec((hp, hp), lambda i, j: (0, 0)),
                  pl.BlockSpec((1, hp), lambda i, j: (0, 0)),
                  pl.BlockSpec((1, hp), lambda i, j: (0, 0)),
                  pl.BlockSpec((hp, tn), lambda i, j: (0, (i // fwd_tiles) * (ncol // tn) + j)),
                  pl.BlockSpec((1, tn), lambda i, j: (0, j))],
        out_specs=pl.BlockSpec((1, tl, tn), lambda i, j: (0, i, j)),
        out_shape=jax.ShapeDtypeStruct((1, 2 * seq, ncol), BF16),
        scratch_shapes=[pltpu.VMEM((tl, hp), BF16)],
        compiler_params=_cparams("parallel", "arbitrary"),
        name="hyena_filters",
    )(feats, t, w1p, b1p, w2p, b2p, frp, w3p, dl)


K1_GROUP = SUBLANES


def _dft_tables(n1, n2):
    n = n1 * n2
    hl = n1 // 2
    k1 = np.arange(n1)[None, :, None]
    l1 = np.arange(hl)[None, None, :]
    l2 = np.arange(n2)[:, None, None]
    ang = -2.0 * np.pi * ((k1 * (n2 * l1 + l2)) % n) / n
    gr, gi = np.cos(ang), np.sin(ang)
    gm = np.concatenate([np.concatenate([gr, -gi], axis=2), np.concatenate([gi, gr], axis=2)], axis=1)
    a2 = -2.0 * np.pi * ((np.arange(n2)[:, None] * np.arange(n2)[None, :]) % n2) / n2
    fr, fi = np.cos(a2), np.sin(a2)
    f2 = np.block([[fr, -fi], [fi, fr]])
    if2 = np.block([[fr, fi], [-fi, fr]])
    ir, ii = np.transpose(gr, (0, 2, 1)) / n, -np.transpose(gi, (0, 2, 1)) / n
    igm = np.concatenate([np.concatenate([ir, -ii], axis=2), np.concatenate([ii, ir], axis=2)], axis=1)
    angf = -2.0 * np.pi * ((k1 * (n2 * np.arange(n1)[None, None, :] + l2)) % n) / n
    gmr = np.concatenate([np.cos(angf), np.sin(angf)], axis=1)
    to = lambda a: jnp.asarray(a.astype(np.float32)).astype(BF16)
    return to(gm), to(f2), to(if2), to(igm), to(gmr)


def _fft_split(seq):
    n = 2 * seq
    n2 = LANES if n % (LANES * 2 * SUBLANES) == 0 and n // LANES >= 2 * SUBLANES else 2 * SUBLANES
    n1 = n // n2
    assert n1 * n2 == n and n1 % (2 * SUBLANES) == 0 and n2 % SUBLANES == 0, (n1, n2)
    return n1, n2


def _fill_pitched(src_ref, p_ref, *, nb, rows, n2, pitch):
    def body(l1, carry):
        for b in range(nb):
            rows = src_ref[b, pl.ds(pl.multiple_of(l1 * n2, n2), n2), :]
            p_ref[b, pl.ds(l1, n2, stride=pitch), :] = rows.astype(F32)
        return carry

    lax.fori_loop(0, rows, body, 0, unroll=2)


def _stage1_to_yv(p_ref, gm_ref, yv_ref, step, *, nb, n1, l2c, pitch):
    rows = n1 // nb

    def body(t, carry):
        l2 = step * l2c + t
        base = pl.multiple_of(l2 * pitch, SUBLANES)
        rhs = jnp.concatenate([p_ref[b, pl.ds(base, rows), :] for b in range(nb)], axis=0)
        res = _dot(gm_ref[t], rhs.astype(BF16))
        row0 = pl.multiple_of(l2 * K1_GROUP, K1_GROUP)
        for g in range(n1 // K1_GROUP):
            for ri in range(2):
                lo = ri * n1 + g * K1_GROUP
                yv_ref[g * 2 + ri, pl.ds(row0, K1_GROUP), :] = res[lo:lo + K1_GROUP]
        return carry

    lax.fori_loop(0, l2c, body, 0, unroll=8)


def _yv_pair(yv_ref, g, t, n2):
    rows = lambda tt: pl.ds(tt, n2, stride=K1_GROUP)
    re = jnp.concatenate([yv_ref[g * 2, rows(t), :], yv_ref[g * 2, rows(t + 1), :]], axis=1)
    im = jnp.concatenate([yv_ref[g * 2 + 1, rows(t), :], yv_ref[g * 2 + 1, rows(t + 1), :]], axis=1)
    return jnp.concatenate([re, im], axis=0).astype(BF16)


def _filtspec_kernel(filt_ref, gmr_ref, f2_ref, o_ref, yv_ref, p_ref, *, n1, n2, l2c, g2, s1, pitch):
    s = pl.program_id(1)

    @pl.when(s == 0)
    def _():
        _fill_pitched(filt_ref, p_ref, nb=1, rows=n1, n2=n2, pitch=pitch)

    @pl.when(s < s1)
    def _():
        _stage1_to_yv(p_ref, gmr_ref, yv_ref, s, nb=1, n1=n1, l2c=l2c, pitch=pitch)

    @pl.when(s >= s1)
    def _():
        for gg in range(g2):
            g = (s - s1) * g2 + gg
            for t in range(0, K1_GROUP, 2):
                x = _dot(f2_ref[...], _yv_pair(yv_ref, g, t, n2))
                for d in range(2):
                    o_ref[gg * K1_GROUP + t + d, 0] = x[:n2, d * LANES:(d + 1) * LANES].astype(o_ref.dtype)
                    o_ref[gg * K1_GROUP + t + d, 1] = x[n2:, d * LANES:(d + 1) * LANES].astype(o_ref.dtype)


def _filter_spectrum(filt, gmr, f2, n1, n2):
    c = filt.shape[-1]
    ng = n1 // K1_GROUP
    l2c = _pick(n2, 32, 4)
    g2 = _pick(ng, 4, 1)
    s1, s2 = n2 // l2c, ng // g2
    pitch = n1 + SUBLANES
    return pl.pallas_call(
        functools.partial(_filtspec_kernel, n1=n1, n2=n2, l2c=l2c, g2=g2, s1=s1, pitch=pitch),
        grid=(c // LANES, s1 + s2),
        in_specs=[pl.BlockSpec((1, n1 * n2, LANES), lambda cb, s: (0, 0, cb), pipeline_mode=pl.Buffered(1)),
                  pl.BlockSpec((l2c, 2 * n1, n1), lambda cb, s: (jnp.minimum(s, s1 - 1), 0, 0)),
                  pl.BlockSpec((2 * n2, 2 * n2), lambda cb, s: (0, 0))],
        out_specs=pl.BlockSpec((g2 * K1_GROUP, 2, n2, LANES), lambda cb, s: (jnp.maximum(s - s1, 0), 0, 0, cb)),
        out_shape=jax.ShapeDtypeStruct((n1, 2, n2, c), BF16),
        scratch_shapes=[pltpu.VMEM((2 * ng, n2 * K1_GROUP, LANES), F32), pltpu.VMEM((1, n2 * pitch, LANES), F32)],
        compiler_params=_cparams("parallel", "arbitrary"),
        name="filter_spectrum",
    )(filt, gmr, f2)


def _fftconv_kernel(sig_ref, gm_ref, k_ref, f2_ref, if2_ref, igm_ref, gate_ref, bias_ref, o_ref, yv_ref, p_ref,
                    *, n1, n2, l2c, g2, l3, rows4, s1, s2, s3, pitch):
    s = pl.program_id(1)
    hl = n1 // 2
    ng = n1 // K1_GROUP

    @pl.when(s == 0)
    def _():
        _fill_pitched(sig_ref, p_ref, nb=2, rows=hl, n2=n2, pitch=pitch)

    @pl.when(s < s1)
    def _():
        _stage1_to_yv(p_ref, gm_ref, yv_ref, s, nb=2, n1=n1, l2c=l2c, pitch=pitch)

    @pl.when((s >= s1) & (s < s1 + s2))
    def _():
        for gg in range(g2):
            g = (s - s1) * g2 + gg
            rows = lambda tt: pl.ds(tt, n2, stride=K1_GROUP)
            for t in range(0, K1_GROUP, 2):
                x = _dot(f2_ref[...], _yv_pair(yv_ref, g, t, n2))
                xr, xi = x[:n2], x[n2:]
                kt = gg * K1_GROUP + t
                kr = jnp.concatenate([k_ref[kt, 0], k_ref[kt + 1, 0]], axis=1).astype(F32)
                ki = jnp.concatenate([k_ref[kt, 1], k_ref[kt + 1, 1]], axis=1).astype(F32)
                p = jnp.concatenate([xr * kr - xi * ki, xr * ki + xi * kr], axis=0).astype(BF16)
                yp = _dot(if2_ref[...], p)
                for d in range(2):
                    for ri in range(2):
                        yv_ref[g * 2 + ri, rows(t + d), :] = yp[ri * n2:(ri + 1) * n2, d * LANES:(d + 1) * LANES]

    @pl.when((s >= s1 + s2) & (s < s1 + s2 + s3))
    def _():
        def body(i, carry):
            l2 = (s - s1 - s2) * (l3 * SUBLANES) + i
            row0 = pl.multiple_of(l2 * K1_GROUP, K1_GROUP)
            tiles = [yv_ref[g * 2 + ri, pl.ds(row0, K1_GROUP), :] for ri in range(2) for g in range(ng)]
            res = _dot(igm_ref[i], jnp.concatenate(tiles, axis=0).astype(BF16))
            base = pl.multiple_of(l2 * pitch, SUBLANES)
            for b in range(2):
                p_ref[b, pl.ds(base, hl), :] = res[b * hl:(b + 1) * hl]
            return carry

        lax.fori_loop(0, l3 * SUBLANES, body, 0, unroll=8)

    @pl.when(s >= s1 + s2 + s3)
    def _():
        r = s - s1 - s2 - s3
        for j in range(rows4 // n2):
            l1 = r * (rows4 // n2) + j
            nat = pl.ds(pl.multiple_of(l1 * n2, n2), n2)
            for b in range(2):
                conv = p_ref[b, pl.ds(l1, n2, stride=pitch), :] + bias_ref[...] * sig_ref[b, nat, :]
                o_ref[b, j * n2:(j + 1) * n2, :] = (gate_ref[b, j * n2:(j + 1) * n2, :] * conv).astype(o_ref.dtype)


def _fft_conv_gated(sig, kspec, kcol_block0, gate, gate_col_block0, bias, tables, n1, n2, out_dtype):
    gm, f2, if2, igm = tables
    c = bias.shape[1]
    seq = sig.shape[1]
    ng = n1 // K1_GROUP
    l2c = _pick(n2, 32, 4)
    g2 = _pick(ng, 4, 1)
    l3 = _pick(n2 // SUBLANES, 4, 1)
    rows4 = _pick(seq, 2048, n2)
    s1, s2, s3, s4 = n2 // l2c, ng // g2, n2 // (l3 * SUBLANES), seq // rows4
    pitch = n1 // 2 + SUBLANES
    clamp = lambda v, n: jnp.clip(v, 0, n - 1)
    return pl.pallas_call(
        functools.partial(_fftconv_kernel, n1=n1, n2=n2, l2c=l2c, g2=g2, l3=l3, rows4=rows4, s1=s1, s2=s2, s3=s3,
                          pitch=pitch),
        grid=(c // LANES, s1 + s2 + s3 + s4),
        in_specs=[pl.BlockSpec((2, seq, LANES), lambda cb, s: (0, 0, cb), pipeline_mode=pl.Buffered(1)),
                  pl.BlockSpec((l2c, 2 * n1, n1), lambda cb, s: (clamp(s, s1), 0, 0)),
                  pl.BlockSpec((g2 * K1_GROUP, 2, n2, LANES),
                               lambda cb, s: (clamp(s - s1, s2), 0, 0, kcol_block0 + cb)),
                  pl.BlockSpec((2 * n2, 2 * n2), lambda cb, s: (0, 0)),
                  pl.BlockSpec((2 * n2, 2 * n2), lambda cb, s: (0, 0)),
                  pl.BlockSpec((l3 * SUBLANES, n1, 2 * n1), lambda cb, s: (clamp(s - s1 - s2, s3), 0, 0)),
                  pl.BlockSpec((2, rows4, LANES),
                               lambda cb, s: (0, clamp(s - s1 - s2 - s3, s4), gate_col_block0 + cb)),
                  pl.BlockSpec((1, LANES), lambda cb, s: (0, cb))],
        out_specs=pl.BlockSpec((2, rows4, LANES), lambda cb, s: (0, clamp(s - s1 - s2 - s3, s4), cb)),
        out_shape=jax.ShapeDtypeStruct((2, seq, c), out_dtype),
        scratch_shapes=[pltpu.VMEM((2 * ng, n2 * K1_GROUP, LANES), F32), pltpu.VMEM((2, n2 * pitch, LANES), F32)],
        compiler_params=_cparams("parallel", "arbitrary"),
        name="fft_conv_gated",
    )(sig, gm, kspec, f2, if2, igm, gate, bias)


def _hyena(u, fw1, fb1, fw2, fb2, fw3, freq, hy_bias):
    bsz, seq, _ = u.shape
    assert bsz == 2, "the FFT convolution packs exactly two batch elements into one complex signal"
    width = hy_bias.shape[1]
    n1, n2 = _fft_split(seq)
    gm, f2, if2, igm, gmr = _dft_tables(n1, n2)

    filt = _hyena_filters(seq, fw1, fb1, fw2, fb2, fw3, freq, width)
    kspec = _filter_spectrum(filt, gmr, f2, n1, n2)

    tables = (gm, f2, if2, igm)
    wb = width // LANES
    z = _fft_conv_gated(u, kspec, 0, u, wb, hy_bias[0:1], tables, n1, n2, F32)
    return _fft_conv_gated(z, kspec, wb, u, 2 * wb, hy_bias[1:2], tables, n1, n2, BF16)


HEAD_LANES = 2 * LANES


ROPE_NF = MLA_ROPE // 4


def _head_layout(w, heads, nope):
    r = w.shape[0]
    w3 = jnp.pad(w.reshape(r, heads, nope + MLA_ROPE), ((0, 0), (0, 0), (0, HEAD_LANES - nope - MLA_ROPE)))
    return w3.reshape(r, heads * HEAD_LANES)


def _rope_tables(seq):
    rows = seq // GRID_W
    row = np.repeat(np.arange(rows, dtype=np.float32), GRID_W)
    col = np.tile(np.arange(GRID_W, dtype=np.float32), rows)
    half = MLA_ROPE // 2
    inv = (np.float32(ROPE_THETA) ** (-np.arange(0, half, 2, dtype=np.float32) / np.float32(half))).astype(np.float32)
    zero = np.zeros((seq, ROPE_NF), np.float32)
    cos, sin_lo, sin_hi = [], [], []
    for pos in (row, col):
        ang = pos[:, None] * inv
        cos += [np.cos(ang), np.cos(ang)]
        sin_lo += [-np.sin(ang), zero]
        sin_hi += [zero, np.sin(ang)]
    pad = [np.zeros((seq, LANES - MLA_ROPE), np.float32)]
    return tuple(jnp.asarray(np.concatenate(t + pad, axis=-1).astype(np.float32)) for t in (cos, sin_lo, sin_hi))


def _rope(t, cos, sin_lo, sin_hi):
    return t * cos + pltpu.roll(t, LANES - ROPE_NF, 1) * sin_lo + pltpu.roll(t, ROPE_NF, 1) * sin_hi


def _rms(x, g):
    return x * lax.rsqrt(jnp.mean(x * x, axis=-1, keepdims=True) + NORM_EPS) * g


def _q_kernel(qa_ref, g_ref, w_ref, gn_ref, cos_ref, slo_ref, shi_ref, o_ref, *, heads, qk_dim, out_scale):
    xn = _rms(qa_ref[...], g_ref[...]).astype(BF16)
    q = _dot(xn, w_ref[...])
    gn = gn_ref[...]
    cos, slo, shi = cos_ref[...], slo_ref[...], shi_ref[...]
    for h in range(heads):
        qh = q[:, h * HEAD_LANES:(h + 1) * HEAD_LANES]
        inv = lax.rsqrt(jnp.sum(qh * qh, axis=-1, keepdims=True) / qk_dim + NORM_EPS)
        qn = qh * inv * gn
        o_ref[0, h, :, 0:LANES] = (qn[:, :LANES] * out_scale).astype(BF16)
        o_ref[0, h, :, LANES:] = (_rope(qn[:, LANES:], cos, slo, shi) * out_scale).astype(BF16)


def _mla_queries(proj, col_block, g_qa, w_q, gn, rope, heads, qk_dim, out_scale):
    bsz, seq, _ = proj.shape
    r = g_qa.shape[1]
    tm = _pick(seq, 512, 16)
    return pl.pallas_call(
        functools.partial(_q_kernel, heads=heads, qk_dim=qk_dim, out_scale=out_scale),
        grid=(bsz, seq // tm),
        in_specs=[pl.BlockSpec((None, tm, r), lambda b, i: (b, i, col_block)),
                  pl.BlockSpec((1, r), lambda b, i: (0, 0)),
                  pl.BlockSpec((r, heads * HEAD_LANES), lambda b, i: (0, 0)),
                  pl.BlockSpec((1, HEAD_LANES), lambda b, i: (0, 0))]
                 + [pl.BlockSpec((tm, LANES), lambda b, i: (i, 0))] * 3,
        out_specs=pl.BlockSpec((1, heads, tm, HEAD_LANES), lambda b, i: (b, 0, i, 0)),
        out_shape=jax.ShapeDtypeStruct((bsz, heads, seq, HEAD_LANES), BF16),
        compiler_params=_cparams("parallel", "parallel"),
        name="mla_queries",
    )(proj, g_qa, w_q, gn, *rope)


V_ROWS = LANES + 16


def _kv_kernel(kva_ref, kr_ref, g_ref, wk_ref, wvt_ref, gn_ref, *rest, heads, qk_dim, rope):
    if rope:
        cos_ref, slo_ref, shi_ref, k_ref, v_ref = rest
    else:
        k_ref, v_ref = rest
    xn = _rms(kva_ref[...], g_ref[...]).astype(BF16)
    kk = _dot(xn, wk_ref[...])
    vt = lax.dot_general(wvt_ref[...], xn, (((1,), (1,)), ((), ())), preferred_element_type=F32)
    gn = gn_ref[...]
    kr = kr_ref[...]
    ssr = jnp.sum(kr * kr, axis=-1, keepdims=True)
    krg = kr * gn[:, LANES:]
    if rope:
        krg = _rope(krg, cos_ref[...], slo_ref[...], shi_ref[...])
    ones_row = (lax.broadcasted_iota(jnp.int32, (V_ROWS - LANES, kr.shape[0]), 0) == 0).astype(BF16)
    for h in range(heads):
        kn = kk[:, h * LANES:(h + 1) * LANES]
        inv = lax.rsqrt((jnp.sum(kn * kn, axis=-1, keepdims=True) + ssr) / qk_dim + NORM_EPS)
        k_ref[0, h, :, 0:LANES] = (kn * inv * gn[:, :LANES]).astype(BF16)
        k_ref[0, h, :, LANES:] = (krg * inv).astype(BF16)
        v_ref[0, h, 0:LANES, :] = vt[h * LANES:(h + 1) * LANES].astype(BF16)
        v_ref[0, h, LANES:, :] = ones_row


def _mla_keys_values(proj, kv_block, kr_block, g_kva, w_k, w_vt, gn, rope, heads, qk_dim):
    bsz, seq, _ = proj.shape
    r = g_kva.shape[1]
    tm = _pick(seq, 512, LANES)
    in_specs = [pl.BlockSpec((None, tm, r), lambda b, i: (b, i, kv_block)),
                pl.BlockSpec((None, tm, LANES), lambda b, i: (b, i, kr_block)),
                pl.BlockSpec((1, r), lambda b, i: (0, 0)),
                pl.BlockSpec((r, heads * LANES), lambda b, i: (0, 0)),
                pl.BlockSpec((heads * LANES, r), lambda b, i: (0, 0)),
                pl.BlockSpec((1, HEAD_LANES), lambda b, i: (0, 0))]
    args = [proj, proj, g_kva, w_k, w_vt, gn]
    if rope is not None:
        in_specs += [pl.BlockSpec((tm, LANES), lambda b, i: (i, 0))] * 3
        args += list(rope)
    return pl.pallas_call(
        functools.partial(_kv_kernel, heads=heads, qk_dim=qk_dim, rope=rope is not None),
        grid=(bsz, seq // tm),
        in_specs=in_specs,
        out_specs=[pl.BlockSpec((1, heads, tm, HEAD_LANES), lambda b, i: (b, 0, i, 0)),
                   pl.BlockSpec((1, heads, V_ROWS, tm), lambda b, i: (b, 0, 0, i))],
        out_shape=[jax.ShapeDtypeStruct((bsz, heads, seq, HEAD_LANES), BF16),
                   jax.ShapeDtypeStruct((bsz, heads, V_ROWS, seq), BF16)],
        compiler_params=_cparams("parallel", "parallel"),
        name="mla_keys_values",
    )(*args)


ATTN_HEADS_PER_STEP = 2
ATTN_LOOKAHEAD = 2


def _attn_kernel(q_ref, k_ref, v_ref, kc_ref, vc_ref, o_ref, m_ref, acc_ref, *, hps, kchunk):
    kj = pl.program_id(3)

    def scores(g, k):
        return lax.dot_general(k, q_ref[0, g], (((1,), (1,)), ((), ())), preferred_element_type=F32)

    def update(g, st, vt):
        m_old = m_ref[g]
        m_new = jnp.maximum(m_old, jnp.max(st, axis=0, keepdims=True))
        p = jnp.exp2(st - m_new)
        acc_ref[g] = jnp.exp2(m_old - m_new) * acc_ref[g] + _dot(vt, p.astype(BF16))
        m_ref[g] = m_new

    def sweep(units):
        ahead = [scores(g, k()) for g, k, _ in units[:ATTN_LOOKAHEAD]]
        for i, (g, _, vt) in enumerate(units):
            st = ahead.pop(0)
            if i + ATTN_LOOKAHEAD < len(units):
                nxt = units[i + ATTN_LOOKAHEAD]
                ahead.append(scores(nxt[0], nxt[1]()))
            update(g, st, vt())

    @pl.when(kj == 0)
    def _():
        m_ref[...] = jnp.full_like(m_ref, -jnp.inf)
        acc_ref[...] = jnp.zeros_like(acc_ref)
        sweep([(g, functools.partial(lambda g: kc_ref[0, g], g), functools.partial(lambda g: vc_ref[0, g], g))
               for g in range(hps)])

    tk = k_ref.shape[2]
    sweep([(g,
            functools.partial(lambda g, c: k_ref[0, g, c * kchunk:(c + 1) * kchunk, :], g, c),
            functools.partial(lambda g, c: v_ref[0, g, :, c * kchunk:(c + 1) * kchunk], g, c))
           for c in range(tk // kchunk) for g in range(hps)])

    @pl.when(kj == pl.num_programs(3) - 1)
    def _():
        for g in range(hps):
            acc = acc_ref[g]
            out_t = acc[:LANES] / acc[LANES:LANES + 1]
            o_ref[0, :, g * LANES:(g + 1) * LANES] = out_t.T.astype(o_ref.dtype)


def _attention(q, k, vt, kc, vct, tq_pref, tk_pref, kchunk_pref):
    bsz, heads, seq, _ = q.shape
    lc = kc.shape[2]
    hps = ATTN_HEADS_PER_STEP if heads % ATTN_HEADS_PER_STEP == 0 else 1
    tq = _pick(seq, tq_pref, LANES)
    tk = _pick(seq, tk_pref, LANES)
    kchunk = _pick(tk, kchunk_pref, LANES)
    return pl.pallas_call(
        functools.partial(_attn_kernel, hps=hps, kchunk=kchunk),
        grid=(bsz, heads // hps, seq // tq, seq // tk),
        in_specs=[pl.BlockSpec((1, hps, tq, HEAD_LANES), lambda b, h, i, j: (b, h, i, 0)),
                  pl.BlockSpec((1, hps, tk, HEAD_LANES), lambda b, h, i, j: (b, h, j, 0)),
                  pl.BlockSpec((1, hps, V_ROWS, tk), lambda b, h, i, j: (b, h, 0, j)),
                  pl.BlockSpec((1, hps, lc, HEAD_LANES), lambda b, h, i, j: (b, h, 0, 0)),
                  pl.BlockSpec((1, hps, V_ROWS, lc), lambda b, h, i, j: (b, h, 0, 0))],
        out_specs=pl.BlockSpec((1, tq, hps * LANES), lambda b, h, i, j: (b, i, h)),
        out_shape=jax.ShapeDtypeStruct((bsz, seq, heads * LANES), BF16),
        scratch_shapes=[pltpu.VMEM((hps, 1, tq), F32), pltpu.VMEM((hps, V_ROWS, tq), F32)],
        compiler_params=_cparams("parallel", "parallel", "parallel", "arbitrary"),
        name="attention",
    )(q, k, vt, kc, vct)


def kernel(x, c, ctx, c_ctx, norm1_g, norm2_g, w_ada, b_ada, w_in, hy_conv_w, hy_conv_b, hy_filt_w1, hy_filt_b1, hy_filt_w2, hy_filt_b2, hy_filt_w3, hy_freq, hy_bias, mla_g_qa, mla_w_qb, mla_g_kva, mla_w_kvb, mla_q_norm_g, mla_k_norm_g, w_out, w_mlp1, w_mlp2):
    assert w_ada.shape[0] == 1, "single-layer block"
    bsz, seq, d = x.shape
    lc = ctx.shape[1]
    hyc = hy_conv_b.shape[1]
    width = hy_bias.shape[2]
    q_lora = mla_g_qa.shape[1]
    kv_lora = mla_g_kva.shape[1]
    qk_dim = mla_q_norm_g.shape[1]
    nope = qk_dim - MLA_ROPE
    heads = mla_w_qb.shape[2] // qk_dim
    v_dim = mla_w_kvb.shape[2] // heads - nope
    assert nope == LANES and v_dim == LANES and seq % GRID_W == 0
    q0, kv0, kr0 = hyc, hyc + q_lora, hyc + q_lora + kv_lora
    assert q0 % q_lora == 0 and kv0 % kv_lora == 0 and kr0 % LANES == 0

    cc = jnp.zeros((8, d), F32).at[:bsz].set(c).at[bsz].set(c_ctx)
    mod = _adaln(cc, w_ada[0], b_ada)
    chunk = lambda i: mod[:bsz, i * d:(i + 1) * d][:, None, :]
    sh1, sc1, g1, sh2, sc2, g2 = [chunk(i) for i in range(6)]
    csh1 = mod[bsz:bsz + 1, 0:d][:, None, :]
    csc1 = mod[bsz:bsz + 1, d:2 * d][:, None, :]

    w_in0 = w_in[0]
    np_cols = kr0 + LANES
    np_cols += (-np_cols) % 1024
    w_pad = jnp.pad(w_in0, ((0, 0), (0, np_cols - w_in0.shape[1]))).astype(BF16)
    w_q = _head_layout(mla_w_qb[0], heads, nope).astype(BF16)
    w_kv = mla_w_kvb[0].reshape(kv_lora, heads, nope + v_dim).astype(BF16)
    w_k = w_kv[..., :nope].reshape(kv_lora, heads * nope)
    w_vt = w_kv[..., nope:].reshape(kv_lora, heads * v_dim).T
    gq = _head_layout(mla_q_norm_g, 1, nope)
    gk = _head_layout(mla_k_norm_g, 1, nope)
    rope = _rope_tables(seq)

    ctx_cols = kv0 - kv0 % 1024
    proj_c = _norm_mod_matmul(ctx.reshape(bsz * lc, d), norm1_g, csh1, csc1, w_pad, bsz * lc, F32, False,
                              512, 1024, col0=ctx_cols).reshape(bsz, lc, np_cols - ctx_cols)
    k_c, v_c = _mla_keys_values(proj_c, (kv0 - ctx_cols) // kv_lora, (kr0 - ctx_cols) // LANES,
                                mla_g_kva, w_k, w_vt, gk, None, heads, qk_dim)

    x2 = x.reshape(bsz * seq, d)
    taps = jnp.zeros((3, np_cols), F32).at[1].set(1.0).at[:, :hyc].set(hy_conv_w[0])
    tap_bias = jnp.zeros((1, np_cols), F32).at[0, :hyc].set(hy_conv_b[0])
    proj = _norm_mod_matmul_conv(x2, norm1_g, sh1, sc1, w_pad, taps, tap_bias, seq, 512, 1024)
    proj = proj.reshape(bsz, seq, np_cols)
    y_hy = _hyena(proj, hy_filt_w1[0], hy_filt_b1[0], hy_filt_w2[0], hy_filt_b2[0], hy_filt_w3[0], hy_freq[0],
                  hy_bias[0])
    q = _mla_queries(proj, q0 // q_lora, mla_g_qa, w_q, gq, rope, heads, qk_dim, qk_dim ** -0.5 * math.log2(math.e))
    k, v = _mla_keys_values(proj, kv0 // kv_lora, kr0 // LANES, mla_g_kva, w_k, w_vt, gk, rope, heads, qk_dim)
    y_att = _attention(q, k, v, k_c, v_c, 2048, 2048, 512)

    x1 = _out_proj(y_hy.reshape(bsz * seq, width), y_att.reshape(bsz * seq, heads * v_dim),
                   w_out[0].astype(BF16), x2, g1, seq)
    hmid = _norm_mod_matmul(x1, norm2_g, sh2, sc2, w_mlp1[0].astype(BF16), seq, BF16, True, 1024, 512)
    out = _matmul_gated_residual(hmid, w_mlp2[0].astype(BF16), x1, g2, seq)
    return out.reshape(bsz, seq, d)
```

```python
import functools
import math

import numpy as np
import jax
import jax.numpy as jnp
from jax import lax
from jax.experimental import pallas as pl
from jax.experimental.pallas import tpu as pltpu

F32 = jnp.float32
BF16 = jnp.bfloat16

NORM_EPS = 1e-6
GRID_W = 64
MLA_ROPE = 64
ROPE_THETA = 10000.0
HY_POS_BANDS = 16
HY_DECAY_TARGET = 1e-2
HY_FAST_DECAY_PCT = 0.3
HY_SLOW_DECAY_PCT = 1.5

LANES = 128
SUBLANES = 8
VMEM_LIMIT = 56 * 1024 * 1024


def _cparams(*sem):
    return pltpu.CompilerParams(dimension_semantics=sem, vmem_limit_bytes=VMEM_LIMIT)


def _pick(dim, pref, align):
    t = min(pref, dim)
    t -= t % align
    while t >= align:
        if dim % t == 0:
            return t
        t -= align
    return dim


def _dot(a, b):
    return jnp.dot(a, b, preferred_element_type=F32)


def _adaln_kernel(c_ref, w_ref, b_ref, o_ref):
    c = c_ref[...]
    s = c * jax.nn.sigmoid(c)
    o_ref[...] = _dot(s.astype(BF16), w_ref[...].astype(BF16)) + b_ref[...]


def _adaln(cc, w, b):
    d, n = w.shape
    tn = _pick(n, 512, LANES)
    return pl.pallas_call(
        _adaln_kernel,
        grid=(n // tn,),
        in_specs=[pl.BlockSpec((8, d), lambda j: (0, 0)),
                  pl.BlockSpec((d, tn), lambda j: (0, j)),
                  pl.BlockSpec((1, tn), lambda j: (0, j))],
        out_specs=pl.BlockSpec((8, tn), lambda j: (0, j)),
        out_shape=jax.ShapeDtypeStruct((8, n), F32),
        compiler_params=_cparams("parallel"),
        name="adaln",
    )(cc, w, b)


NORM_ROWS = 16


def _norm_mod_rows(x_ref, gm_ref, sh_ref, h_ref, dst0, nrows, keep=None):
    def body(c, carry):
        r = pl.multiple_of(c * NORM_ROWS, NORM_ROWS)
        x = x_ref[pl.ds(r, NORM_ROWS), :]
        inv = lax.rsqrt(jnp.mean(x * x, axis=-1, keepdims=True) + NORM_EPS)
        y = x * inv * gm_ref[0] + sh_ref[0]
        if keep is not None:
            y = jnp.where(keep, y, 0.0)
        h_ref[pl.ds(dst0 + r, NORM_ROWS), :] = y.astype(BF16)
        return carry

    lax.fori_loop(0, nrows // NORM_ROWS, body, 0, unroll=2 if nrows > NORM_ROWS else 1)


def _nmm_kernel(x_ref, gm_ref, sh_ref, w_ref, o_ref, h_ref, *, sq_relu):
    @pl.when(pl.program_id(1) == 0)
    def _():
        _norm_mod_rows(x_ref, gm_ref, sh_ref, h_ref, 0, x_ref.shape[0])

    acc = _dot(h_ref[...], w_ref[...])
    if sq_relu:
        acc = jnp.square(jnp.maximum(acc, 0.0))
    o_ref[...] = acc.astype(o_ref.dtype)


def _norm_mod_matmul(x, g, shift, scale, w, rows_per_mod, out_dtype, sq_relu, tm_pref, tn_pref, col0=0):
    m, k = x.shape
    n = w.shape[1] - col0
    tm = _pick(math.gcd(m, rows_per_mod), tm_pref, 16)
    tn = _pick(math.gcd(n, col0) if col0 else n, tn_pref, LANES)
    mod_idx = lambda i, j: ((i * tm) // rows_per_mod, 0, 0)
    return pl.pallas_call(
        functools.partial(_nmm_kernel, sq_relu=sq_relu),
        grid=(m // tm, n // tn),
        in_specs=[pl.BlockSpec((tm, k), lambda i, j: (i, 0)),
                  pl.BlockSpec((1, 1, k), mod_idx),
                  pl.BlockSpec((1, 1, k), mod_idx),
                  pl.BlockSpec((k, tn), lambda i, j: (0, col0 // tn + j))],
        out_specs=pl.BlockSpec((tm, tn), lambda i, j: (i, j)),
        out_shape=jax.ShapeDtypeStruct((m, n), out_dtype),
        scratch_shapes=[pltpu.VMEM((tm, k), BF16)],
        compiler_params=_cparams("parallel", "arbitrary"),
        name="norm_mod_matmul",
    )(x, g * (1.0 + scale), shift, w)


HALO = 16


def _nmm_conv_kernel(x_ref, xp_ref, xn_ref, gm_ref, sh_ref, w_ref, cw_ref, cb_ref, o_ref, h_ref,
                     *, tm, rows_per_seq):
    i = pl.program_id(0)

    @pl.when(pl.program_id(1) == 0)
    def _():
        not_first = (i * tm) % rows_per_seq != 0
        not_last = ((i + 1) * tm) % rows_per_seq != 0
        _norm_mod_rows(xp_ref, gm_ref, sh_ref, h_ref, 0, HALO, keep=not_first)
        _norm_mod_rows(x_ref, gm_ref, sh_ref, h_ref, HALO, tm)
        _norm_mod_rows(xn_ref, gm_ref, sh_ref, h_ref, HALO + tm, HALO, keep=not_last)

    acc = _dot(h_ref[...], w_ref[...])
    rows = acc.shape[0]
    mid = acc[HALO:HALO + tm]
    up = pltpu.roll(acc, 1, 0)[HALO:HALO + tm]
    dn = pltpu.roll(acc, rows - 1, 0)[HALO:HALO + tm]
    o_ref[...] = up * cw_ref[0:1, :] + mid * cw_ref[1:2, :] + dn * cw_ref[2:3, :] + cb_ref[...]


def _norm_mod_matmul_conv(x, g, shift, scale, w, conv_w, conv_b, rows_per_seq, tm_pref, tn_pref):
    m, k = x.shape
    n = w.shape[1]
    tm = _pick(rows_per_seq, tm_pref, HALO)
    tn = _pick(n, tn_pref, LANES)
    r = tm // HALO
    last = m // HALO - 1
    mod_idx = lambda i, j: ((i * tm) // rows_per_seq, 0, 0)
    return pl.pallas_call(
        functools.partial(_nmm_conv_kernel, tm=tm, rows_per_seq=rows_per_seq),
        grid=(m // tm, n // tn),
        in_specs=[pl.BlockSpec((tm, k), lambda i, j: (i, 0)),
                  pl.BlockSpec((HALO, k), lambda i, j: (jnp.maximum(i * r - 1, 0), 0)),
                  pl.BlockSpec((HALO, k), lambda i, j: (jnp.minimum((i + 1) * r, last), 0)),
                  pl.BlockSpec((1, 1, k), mod_idx),
                  pl.BlockSpec((1, 1, k), mod_idx),
                  pl.BlockSpec((k, tn), lambda i, j: (0, j)),
                  pl.BlockSpec((3, tn), lambda i, j: (0, j)),
                  pl.BlockSpec((1, tn), lambda i, j: (0, j))],
        out_specs=pl.BlockSpec((tm, tn), lambda i, j: (i, j)),
        out_shape=jax.ShapeDtypeStruct((m, n), F32),
        scratch_shapes=[pltpu.VMEM((tm + 2 * HALO, k), BF16)],
        compiler_params=_cparams("parallel", "arbitrary"),
        name="norm_mod_matmul_conv",
    )(x, x, x, g * (1.0 + scale), shift, w, conv_w, conv_b)


def _outproj_kernel(a0_ref, a1_ref, w_ref, res_ref, gate_ref, o_ref, *, k0):
    acc = _dot(a0_ref[...], w_ref[0:k0, :]) + _dot(a1_ref[...], w_ref[k0:, :])
    o_ref[...] = res_ref[...] + gate_ref[0] * acc


def _out_proj(a0, a1, w, res, gate, rows_per_mod):
    m, k0 = a0.shape
    k1 = a1.shape[1]
    n = w.shape[1]
    tm = _pick(math.gcd(m, rows_per_mod), 1024, 16)
    tn = _pick(n, 1024, LANES)
    return pl.pallas_call(
        functools.partial(_outproj_kernel, k0=k0),
        grid=(m // tm, n // tn),
        in_specs=[pl.BlockSpec((tm, k0), lambda i, j: (i, 0)),
                  pl.BlockSpec((tm, k1), lambda i, j: (i, 0)),
                  pl.BlockSpec((k0 + k1, tn), lambda i, j: (0, j)),
                  pl.BlockSpec((tm, tn), lambda i, j: (i, j)),
                  pl.BlockSpec((1, 1, tn), lambda i, j: ((i * tm) // rows_per_mod, 0, j))],
        out_specs=pl.BlockSpec((tm, tn), lambda i, j: (i, j)),
        out_shape=jax.ShapeDtypeStruct((m, n), F32),
        compiler_params=_cparams("parallel", "parallel"),
        name="out_proj",
    )(a0, a1, w, res, gate)


def _mmres_kernel(a_ref, w_ref, res_ref, gate_ref, o_ref, acc_ref):
    kk = pl.program_id(2)

    @pl.when(kk == 0)
    def _():
        acc_ref[...] = jnp.zeros_like(acc_ref)

    acc_ref[...] += _dot(a_ref[...], w_ref[...])

    @pl.when(kk == pl.num_programs(2) - 1)
    def _():
        o_ref[...] = res_ref[...] + gate_ref[0] * acc_ref[...]


def _matmul_gated_residual(a, w, res, gate, rows_per_mod):
    m, k = a.shape
    n = w.shape[1]
    tm = _pick(math.gcd(m, rows_per_mod), 1024, 16)
    tn = _pick(n, 1024, LANES)
    tk = _pick(k, 2048, LANES)
    return pl.pallas_call(
        _mmres_kernel,
        grid=(m // tm, n // tn, k // tk),
        in_specs=[pl.BlockSpec((tm, tk), lambda i, j, kk: (i, kk)),
                  pl.BlockSpec((tk, tn), lambda i, j, kk: (kk, j)),
                  pl.BlockSpec((tm, tn), lambda i, j, kk: (i, j)),
                  pl.BlockSpec((1, 1, tn), lambda i, j, kk: ((i * tm) // rows_per_mod, 0, j))],
        out_specs=pl.BlockSpec((tm, tn), lambda i, j, kk: (i, j)),
        out_shape=jax.ShapeDtypeStruct((m, n), F32),
        scratch_shapes=[pltpu.VMEM((tm, tn), F32)],
        compiler_params=_cparams("parallel", "parallel", "arbitrary"),
        name="matmul_gated_residual",
    )(a, w, res, gate)


def _filt_hidden_kernel(feat_ref, w1_ref, b1_ref, w2_ref, b2_ref, fr_ref, o_ref):
    fr = fr_ref[...]
    h = jnp.sin(fr * (_dot(feat_ref[...].astype(BF16), w1_ref[...]) + b1_ref[...]))
    h = jnp.sin(fr * (_dot(h.astype(BF16), w2_ref[...]) + b2_ref[...]))
    o_ref[...] = h.astype(BF16)


def _hyena_filter_hidden(seq, w1, b1, w2, b2, freq):
    two_sided = lambda a: np.concatenate([a, a[:1], a[:0:-1]], axis=0)
    pos = np.arange(seq, dtype=np.float32)
    t = np.linspace(0.0, 1.0, seq, dtype=np.float32)[:, None]
    bands = np.linspace(1e-4, HY_POS_BANDS - 1, HY_POS_BANDS, dtype=np.float32)
    ang = np.float32(2.0 * math.pi / seq) * pos[:, None] * bands[None, :]
    feats = np.concatenate([t, np.cos(ang), -np.sin(ang)], axis=-1).astype(np.float32)
    pd = feats.shape[1]
    hid = w1.shape[1]
    hp = LANES
    feats = jnp.asarray(two_sided(np.pad(feats, ((0, 0), (0, hp - pd)))))
    w1p = jnp.pad(w1, ((0, hp - pd), (0, hp - hid))).astype(BF16)
    w2p = jnp.pad(w2, ((0, hp - hid), (0, hp - hid))).astype(BF16)
    b1p = jnp.pad(b1, (0, hp - hid))[None, :]
    b2p = jnp.pad(b2, (0, hp - hid))[None, :]
    frp = jnp.pad(freq, (0, hp - hid))[None, :]
    tl = _pick(seq, 512, 16)
    const = lambda shape: pl.BlockSpec(shape, lambda i: (0, 0))
    return pl.pallas_call(
        _filt_hidden_kernel,
        grid=(2 * seq // tl,),
        in_specs=[pl.BlockSpec((tl, hp), lambda i: (i, 0)), const((hp, hp)), const((1, hp)), const((hp, hp)),
                  const((1, hp)), const((1, hp))],
        out_specs=pl.BlockSpec((tl, hp), lambda i: (i, 0)),
        out_shape=jax.ShapeDtypeStruct((2 * seq, hp), BF16),
        compiler_params=_cparams("parallel"),
        name="hyena_filter_hidden",
    )(feats, w1p, b1p, w2p, b2p, frp)


K1_GROUP = SUBLANES


def _dft_tables(n1, n2):
    n = n1 * n2
    hl = n1 // 2
    k1 = np.arange(n1)[None, :, None]
    l1 = np.arange(hl)[None, None, :]
    l2 = np.arange(n2)[:, None, None]
    ang = -2.0 * np.pi * ((k1 * (n2 * l1 + l2)) % n) / n
    gr, gi = np.cos(ang), np.sin(ang)
    gm = np.concatenate([np.concatenate([gr, -gi], axis=2), np.concatenate([gi, gr], axis=2)], axis=1)
    a2 = -2.0 * np.pi * ((np.arange(n2)[:, None] * np.arange(n2)[None, :]) % n2) / n2
    fr, fi = np.cos(a2), np.sin(a2)
    f2 = np.block([[fr, -fi], [fi, fr]])
    if2 = np.block([[fr, fi], [-fi, fr]])
    ir, ii = np.transpose(gr, (0, 2, 1)) / n, -np.transpose(gi, (0, 2, 1)) / n
    igm = np.concatenate([np.concatenate([ir, -ii], axis=2), np.concatenate([ii, ir], axis=2)], axis=1)
    angf = -2.0 * np.pi * ((k1 * (n2 * np.arange(n1)[None, None, :] + l2)) % n) / n
    gmr = np.concatenate([np.cos(angf), np.sin(angf)], axis=1)
    to = lambda a: jnp.asarray(a.astype(np.float32)).astype(BF16)
    return to(gm), to(f2), to(if2), to(igm), to(gmr)


def _fft_split(seq):
    n = 2 * seq
    n2 = LANES if n % (LANES * 2 * SUBLANES) == 0 and n // LANES >= 2 * SUBLANES else 2 * SUBLANES
    n1 = n // n2
    assert n1 * n2 == n and n1 % (2 * SUBLANES) == 0 and n2 % SUBLANES == 0, (n1, n2)
    return n1, n2


def _fill_pitched(src_ref, p_ref, *, nb, rows, n2, pitch):
    def body(l1, carry):
        for b in range(nb):
            rows = src_ref[b, pl.ds(pl.multiple_of(l1 * n2, n2), n2), :]
            p_ref[b, pl.ds(l1, n2, stride=pitch), :] = rows.astype(F32)
        return carry

    lax.fori_loop(0, rows, body, 0, unroll=2)


def _stage1_to_yv(p_ref, gm_ref, yv_ref, step, *, nb, n1, l2c, pitch):
    rows = n1 // nb

    def body(t, carry):
        l2 = step * l2c + t
        base = pl.multiple_of(l2 * pitch, SUBLANES)
        rhs = jnp.concatenate([p_ref[b, pl.ds(base, rows), :] for b in range(nb)], axis=0)
        res = _dot(gm_ref[t], rhs.astype(BF16))
        row0 = pl.multiple_of(l2 * K1_GROUP, K1_GROUP)
        for g in range(n1 // K1_GROUP):
            for ri in range(2):
                lo = ri * n1 + g * K1_GROUP
                yv_ref[g * 2 + ri, pl.ds(row0, K1_GROUP), :] = res[lo:lo + K1_GROUP]
        return carry

    lax.fori_loop(0, l2c, body, 0, unroll=8)


def _yv_pair(yv_ref, g, t, n2):
    rows = lambda tt: pl.ds(tt, n2, stride=K1_GROUP)
    re = jnp.concatenate([yv_ref[g * 2, rows(t), :], yv_ref[g * 2, rows(t + 1), :]], axis=1)
    im = jnp.concatenate([yv_ref[g * 2 + 1, rows(t), :], yv_ref[g * 2 + 1, rows(t + 1), :]], axis=1)
    return jnp.concatenate([re, im], axis=0).astype(BF16)


def _fill_filters(h_ref, w3f_ref, w3b_ref, dl_ref, p_ref, *, n1, n2, pitch):
    seq = n1 * n2 // 2
    inv = 1.0 / (seq - 1)
    dl = dl_ref[...]
    row = lax.broadcasted_iota(jnp.int32, (n2, LANES), 0)
    within = row.astype(F32) * inv * dl

    def fill(w_ref, backward):
        e_row = jnp.exp(within) if backward else jnp.exp(-within)

        def body(l1, carry):
            first = (2 * seq - l1 * n2 if backward else l1 * n2).astype(F32) * inv
            decay = e_row * jnp.exp(-first * dl)
            hid = h_ref[pl.ds(pl.multiple_of(l1 * n2, n2), n2), :]
            f = _dot(hid, w_ref[...]) * decay
            if backward:
                f = jnp.where(l1 * n2 + row == seq, 0.0, f)
            p_ref[0, pl.ds(l1, n2, stride=pitch), :] = f
            return carry
        return body

    lax.fori_loop(0, n1 // 2, fill(w3f_ref, False), 0, unroll=8)
    lax.fori_loop(n1 // 2, n1, fill(w3b_ref, True), 0, unroll=8)


def _filtspec_kernel(h_ref, w3f_ref, w3b_ref, dl_ref, gmr_ref, f2_ref, o_ref, yv_ref, p_ref,
                     *, n1, n2, l2c, g2, s1, pitch):
    s = pl.program_id(1)

    @pl.when(s == 0)
    def _():
        _fill_filters(h_ref, w3f_ref, w3b_ref, dl_ref, p_ref, n1=n1, n2=n2, pitch=pitch)

    @pl.when(s < s1)
    def _():
        _stage1_to_yv(p_ref, gmr_ref, yv_ref, s, nb=1, n1=n1, l2c=l2c, pitch=pitch)

    @pl.when(s >= s1)
    def _():
        for gg in range(g2):
            g = (s - s1) * g2 + gg
            for t in range(0, K1_GROUP, 2):
                x = _dot(f2_ref[...], _yv_pair(yv_ref, g, t, n2))
                for d in range(2):
                    o_ref[gg * K1_GROUP + t + d, 0] = x[:n2, d * LANES:(d + 1) * LANES].astype(o_ref.dtype)
                    o_ref[gg * K1_GROUP + t + d, 1] = x[n2:, d * LANES:(d + 1) * LANES].astype(o_ref.dtype)


def _filter_spectrum(hidden, w3, width, gmr, f2, n1, n2):
    hp = hidden.shape[1]
    c = w3.shape[1] // 2
    w3p = jnp.pad(w3, ((0, hp - w3.shape[0]), (0, 0))).astype(BF16)
    deltas = np.abs(np.linspace(math.log(HY_DECAY_TARGET) / HY_SLOW_DECAY_PCT,
                                math.log(HY_DECAY_TARGET) / HY_FAST_DECAY_PCT, width, dtype=np.float32))
    dl = jnp.asarray(np.tile(deltas, c // width)[None, :])
    ng = n1 // K1_GROUP
    l2c = _pick(n2, 32, 4)
    g2 = _pick(ng, 4, 1)
    s1, s2 = n2 // l2c, ng // g2
    pitch = n1 + SUBLANES
    return pl.pallas_call(
        functools.partial(_filtspec_kernel, n1=n1, n2=n2, l2c=l2c, g2=g2, s1=s1, pitch=pitch),
        grid=(c // LANES, s1 + s2),
        in_specs=[pl.BlockSpec((n1 * n2, hp), lambda cb, s: (0, 0)),
                  pl.BlockSpec((hp, LANES), lambda cb, s: (0, cb)),
                  pl.BlockSpec((hp, LANES), lambda cb, s: (0, c // LANES + cb)),
                  pl.BlockSpec((1, LANES), lambda cb, s: (0, cb)),
                  pl.BlockSpec((l2c, 2 * n1, n1), lambda cb, s: (jnp.minimum(s, s1 - 1), 0, 0)),
                  pl.BlockSpec((2 * n2, 2 * n2), lambda cb, s: (0, 0))],
        out_specs=pl.BlockSpec((g2 * K1_GROUP, 2, n2, LANES), lambda cb, s: (jnp.maximum(s - s1, 0), 0, 0, cb)),
        out_shape=jax.ShapeDtypeStruct((n1, 2, n2, c), BF16),
        scratch_shapes=[pltpu.VMEM((2 * ng, n2 * K1_GROUP, LANES), F32), pltpu.VMEM((1, n2 * pitch, LANES), F32)],
        compiler_params=_cparams("parallel", "arbitrary"),
        name="filter_spectrum",
    )(hidden, w3p, w3p, dl, gmr, f2)


def _fftconv_kernel(sig_ref, gm_ref, k_ref, f2_ref, if2_ref, igm_ref, gate_ref, bias_ref, o_ref, yv_ref, p_ref,
                    *, n1, n2, l2c, g2, l3, rows4, s1, s2, s3, pitch):
    s = pl.program_id(1)
    hl = n1 // 2
    ng = n1 // K1_GROUP

    @pl.when(s == 0)
    def _():
        _fill_pitched(sig_ref, p_ref, nb=2, rows=hl, n2=n2, pitch=pitch)

    @pl.when(s < s1)
    def _():
        _stage1_to_yv(p_ref, gm_ref, yv_ref, s, nb=2, n1=n1, l2c=l2c, pitch=pitch)

    @pl.when((s >= s1) & (s < s1 + s2))
    def _():
        for gg in range(g2):
            g = (s - s1) * g2 + gg
            rows = lambda tt: pl.ds(tt, n2, stride=K1_GROUP)
            for t in range(0, K1_GROUP, 2):
                x = _dot(f2_ref[...], _yv_pair(yv_ref, g, t, n2))
                xr, xi = x[:n2], x[n2:]
                kt = gg * K1_GROUP + t
                kr = jnp.concatenate([k_ref[kt, 0], k_ref[kt + 1, 0]], axis=1).astype(F32)
                ki = jnp.concatenate([k_ref[kt, 1], k_ref[kt + 1, 1]], axis=1).astype(F32)
                p = jnp.concatenate([xr * kr - xi * ki, xr * ki + xi * kr], axis=0).astype(BF16)
                yp = _dot(if2_ref[...], p)
                for d in range(2):
                    for ri in range(2):
                        yv_ref[g * 2 + ri, rows(t + d), :] = yp[ri * n2:(ri + 1) * n2, d * LANES:(d + 1) * LANES]

    @pl.when((s >= s1 + s2) & (s < s1 + s2 + s3))
    def _():
        def body(i, carry):
            l2 = (s - s1 - s2) * (l3 * SUBLANES) + i
            row0 = pl.multiple_of(l2 * K1_GROUP, K1_GROUP)
            tiles = [yv_ref[g * 2 + ri, pl.ds(row0, K1_GROUP), :] for ri in range(2) for g in range(ng)]
            res = _dot(igm_ref[i], jnp.concatenate(tiles, axis=0).astype(BF16))
            base = pl.multiple_of(l2 * pitch, SUBLANES)
            for b in range(2):
                p_ref[b, pl.ds(base, hl), :] = res[b * hl:(b + 1) * hl]
            return carry

        lax.fori_loop(0, l3 * SUBLANES, body, 0, unroll=8)

    @pl.when(s >= s1 + s2 + s3)
    def _():
        r = s - s1 - s2 - s3
        for j in range(rows4 // n2):
            l1 = r * (rows4 // n2) + j
            nat = pl.ds(pl.multiple_of(l1 * n2, n2), n2)
            for b in range(2):
                conv = p_ref[b, pl.ds(l1, n2, stride=pitch), :] + bias_ref[...] * sig_ref[b, nat, :]
                o_ref[b, j * n2:(j + 1) * n2, :] = (gate_ref[b, j * n2:(j + 1) * n2, :] * conv).astype(o_ref.dtype)


def _fft_conv_gated(sig, kspec, kcol_block0, gate, gate_col_block0, bias, tables, n1, n2, out_dtype):
    gm, f2, if2, igm = tables
    c = bias.shape[1]
    seq = sig.shape[1]
    ng = n1 // K1_GROUP
    l2c = _pick(n2, 32, 4)
    g2 = _pick(ng, 4, 1)
    l3 = _pick(n2 // SUBLANES, 4, 1)
    rows4 = _pick(seq, 2048, n2)
    s1, s2, s3, s4 = n2 // l2c, ng // g2, n2 // (l3 * SUBLANES), seq // rows4
    pitch = n1 // 2 + SUBLANES
    clamp = lambda v, n: jnp.clip(v, 0, n - 1)
    return pl.pallas_call(
        functools.partial(_fftconv_kernel, n1=n1, n2=n2, l2c=l2c, g2=g2, l3=l3, rows4=rows4, s1=s1, s2=s2, s3=s3,
                          pitch=pitch),
        grid=(c // LANES, s1 + s2 + s3 + s4),
        in_specs=[pl.BlockSpec((2, seq, LANES), lambda cb, s: (0, 0, cb), pipeline_mode=pl.Buffered(1)),
                  pl.BlockSpec((l2c, 2 * n1, n1), lambda cb, s: (clamp(s, s1), 0, 0)),
                  pl.BlockSpec((g2 * K1_GROUP, 2, n2, LANES),
                               lambda cb, s: (clamp(s - s1, s2), 0, 0, kcol_block0 + cb)),
                  pl.BlockSpec((2 * n2, 2 * n2), lambda cb, s: (0, 0)),
                  pl.BlockSpec((2 * n2, 2 * n2), lambda cb, s: (0, 0)),
                  pl.BlockSpec((l3 * SUBLANES, n1, 2 * n1), lambda cb, s: (clamp(s - s1 - s2, s3), 0, 0)),
                  pl.BlockSpec((2, rows4, LANES),
                               lambda cb, s: (0, clamp(s - s1 - s2 - s3, s4), gate_col_block0 + cb)),
                  pl.BlockSpec((1, LANES), lambda cb, s: (0, cb))],
        out_specs=pl.BlockSpec((2, rows4, LANES), lambda cb, s: (0, clamp(s - s1 - s2 - s3, s4), cb)),
        out_shape=jax.ShapeDtypeStruct((2, seq, c), out_dtype),
        scratch_shapes=[pltpu.VMEM((2 * ng, n2 * K1_GROUP, LANES), F32), pltpu.VMEM((2, n2 * pitch, LANES), F32)],
        compiler_params=_cparams("parallel", "arbitrary"),
        name="fft_conv_gated",
    )(sig, gm, kspec, f2, if2, igm, gate, bias)


def _hyena(u, fw1, fb1, fw2, fb2, fw3, freq, hy_bias):
    bsz, seq, _ = u.shape
    assert bsz == 2, "the FFT convolution packs exactly two batch elements into one complex signal"
    width = hy_bias.shape[1]
    n1, n2 = _fft_split(seq)
    gm, f2, if2, igm, gmr = _dft_tables(n1, n2)

    hidden = _hyena_filter_hidden(seq, fw1, fb1, fw2, fb2, freq)
    kspec = _filter_spectrum(hidden, fw3, width, gmr, f2, n1, n2)

    tables = (gm, f2, if2, igm)
    wb = width // LANES
    z = _fft_conv_gated(u, kspec, 0, u, wb, hy_bias[0:1], tables, n1, n2, F32)
    return _fft_conv_gated(z, kspec, wb, u, 2 * wb, hy_bias[1:2], tables, n1, n2, BF16)


HEAD_LANES = 2 * LANES


ROPE_NF = MLA_ROPE // 4


def _head_layout(w, heads, nope):
    r = w.shape[0]
    w3 = jnp.pad(w.reshape(r, heads, nope + MLA_ROPE), ((0, 0), (0, 0), (0, HEAD_LANES - nope - MLA_ROPE)))
    return w3.reshape(r, heads * HEAD_LANES)


def _rope_tables(seq):
    rows = seq // GRID_W
    row = np.repeat(np.arange(rows, dtype=np.float32), GRID_W)
    col = np.tile(np.arange(GRID_W, dtype=np.float32), rows)
    half = MLA_ROPE // 2
    inv = (np.float32(ROPE_THETA) ** (-np.arange(0, half, 2, dtype=np.float32) / np.float32(half))).astype(np.float32)
    zero = np.zeros((seq, ROPE_NF), np.float32)
    cos, sin_lo, sin_hi = [], [], []
    for pos in (row, col):
        ang = pos[:, None] * inv
        cos += [np.cos(ang), np.cos(ang)]
        sin_lo += [-np.sin(ang), zero]
        sin_hi += [zero, np.sin(ang)]
    pad = [np.zeros((seq, LANES - MLA_ROPE), np.float32)]
    return tuple(jnp.asarray(np.concatenate(t + pad, axis=-1).astype(np.float32)) for t in (cos, sin_lo, sin_hi))


def _rope(t, cos, sin_lo, sin_hi):
    return t * cos + pltpu.roll(t, LANES - ROPE_NF, 1) * sin_lo + pltpu.roll(t, ROPE_NF, 1) * sin_hi


def _rms(x, g):
    return x * lax.rsqrt(jnp.mean(x * x, axis=-1, keepdims=True) + NORM_EPS) * g


def _q_kernel(qa_ref, g_ref, w_ref, gn_ref, cos_ref, slo_ref, shi_ref, o_ref, *, heads, qk_dim, out_scale):
    xn = _rms(qa_ref[...], g_ref[...]).astype(BF16)
    q = _dot(xn, w_ref[...])
    gn = gn_ref[...]
    cos, slo, shi = cos_ref[...], slo_ref[...], shi_ref[...]
    for h in range(heads):
        qh = q[:, h * HEAD_LANES:(h + 1) * HEAD_LANES]
        inv = lax.rsqrt(jnp.sum(qh * qh, axis=-1, keepdims=True) / qk_dim + NORM_EPS)
        qn = qh * inv * gn
        o_ref[0, h, :, 0:LANES] = (qn[:, :LANES] * out_scale).astype(BF16)
        o_ref[0, h, :, LANES:] = (_rope(qn[:, LANES:], cos, slo, shi) * out_scale).astype(BF16)


def _mla_queries(proj, col_block, g_qa, w_q, gn, rope, heads, qk_dim, out_scale):
    bsz, seq, _ = proj.shape
    r = g_qa.shape[1]
    tm = _pick(seq, 512, 16)
    return pl.pallas_call(
        functools.partial(_q_kernel, heads=heads, qk_dim=qk_dim, out_scale=out_scale),
        grid=(bsz, seq // tm),
        in_specs=[pl.BlockSpec((None, tm, r), lambda b, i: (b, i, col_block)),
                  pl.BlockSpec((1, r), lambda b, i: (0, 0)),
                  pl.BlockSpec((r, heads * HEAD_LANES), lambda b, i: (0, 0)),
                  pl.BlockSpec((1, HEAD_LANES), lambda b, i: (0, 0))]
                 + [pl.BlockSpec((tm, LANES), lambda b, i: (i, 0))] * 3,
        out_specs=pl.BlockSpec((1, heads, tm, HEAD_LANES), lambda b, i: (b, 0, i, 0)),
        out_shape=jax.ShapeDtypeStruct((bsz, heads, seq, HEAD_LANES), BF16),
        compiler_params=_cparams("parallel", "parallel"),
        name="mla_queries",
    )(proj, g_qa, w_q, gn, *rope)


V_ROWS = LANES + 16


def _kv_kernel(kva_ref, kr_ref, g_ref, wk_ref, wvt_ref, gn_ref, *rest, heads, qk_dim, rope):
    if rope:
        cos_ref, slo_ref, shi_ref, k_ref, v_ref = rest
    else:
        k_ref, v_ref = rest
    xn = _rms(kva_ref[...], g_ref[...]).astype(BF16)
    kk = _dot(xn, wk_ref[...])
    vt = lax.dot_general(wvt_ref[...], xn, (((1,), (1,)), ((), ())), preferred_element_type=F32)
    gn = gn_ref[...]
    kr = kr_ref[...]
    ssr = jnp.sum(kr * kr, axis=-1, keepdims=True)
    krg = kr * gn[:, LANES:]
    if rope:
        krg = _rope(krg, cos_ref[...], slo_ref[...], shi_ref[...])
    ones_row = (lax.broadcasted_iota(jnp.int32, (V_ROWS - LANES, kr.shape[0]), 0) == 0).astype(BF16)
    for h in range(heads):
        kn = kk[:, h * LANES:(h + 1) * LANES]
        inv = lax.rsqrt((jnp.sum(kn * kn, axis=-1, keepdims=True) + ssr) / qk_dim + NORM_EPS)
        k_ref[0, h, :, 0:LANES] = (kn * inv * gn[:, :LANES]).astype(BF16)
        k_ref[0, h, :, LANES:] = (krg * inv).astype(BF16)
        v_ref[0, h, 0:LANES, :] = vt[h * LANES:(h + 1) * LANES].astype(BF16)
        v_ref[0, h, LANES:, :] = ones_row


def _mla_keys_values(proj, kv_block, kr_block, g_kva, w_k, w_vt, gn, rope, heads, qk_dim):
    bsz, seq, _ = proj.shape
    r = g_kva.shape[1]
    tm = _pick(seq, 512, LANES)
    in_specs = [pl.BlockSpec((None, tm, r), lambda b, i: (b, i, kv_block)),
                pl.BlockSpec((None, tm, LANES), lambda b, i: (b, i, kr_block)),
                pl.BlockSpec((1, r), lambda b, i: (0, 0)),
                pl.BlockSpec((r, heads * LANES), lambda b, i: (0, 0)),
                pl.BlockSpec((heads * LANES, r), lambda b, i: (0, 0)),
                pl.BlockSpec((1, HEAD_LANES), lambda b, i: (0, 0))]
    args = [proj, proj, g_kva, w_k, w_vt, gn]
    if rope is not None:
        in_specs += [pl.BlockSpec((tm, LANES), lambda b, i: (i, 0))] * 3
        args += list(rope)
    return pl.pallas_call(
        functools.partial(_kv_kernel, heads=heads, qk_dim=qk_dim, rope=rope is not None),
        grid=(bsz, seq // tm),
        in_specs=in_specs,
        out_specs=[pl.BlockSpec((1, heads, tm, HEAD_LANES), lambda b, i: (b, 0, i, 0)),
                   pl.BlockSpec((1, heads, V_ROWS, tm), lambda b, i: (b, 0, 0, i))],
        out_shape=[jax.ShapeDtypeStruct((bsz, heads, seq, HEAD_LANES), BF16),
                   jax.ShapeDtypeStruct((bsz, heads, V_ROWS, seq), BF16)],
        compiler_params=_cparams("parallel", "parallel"),
        name="mla_keys_values",
    )(*args)


ATTN_HEADS_PER_STEP = 2
ATTN_LOOKAHEAD = 2


def _attn_kernel(q_ref, k_ref, v_ref, kc_ref, vc_ref, o_ref, m_ref, acc_ref, *, hps, kchunk):
    kj = pl.program_id(3)

    def scores(g, k):
        return lax.dot_general(k, q_ref[0, g], (((1,), (1,)), ((), ())), preferred_element_type=F32)

    def update(g, st, vt):
        m_old = m_ref[g]
        m_new = jnp.maximum(m_old, jnp.max(st, axis=0, keepdims=True))
        p = jnp.exp2(st - m_new)
        acc_ref[g] = jnp.exp2(m_old - m_new) * acc_ref[g] + _dot(vt, p.astype(BF16))
        m_ref[g] = m_new

    def sweep(units):
        ahead = [scores(g, k()) for g, k, _ in units[:ATTN_LOOKAHEAD]]
        for i, (g, _, vt) in enumerate(units):
            st = ahead.pop(0)
            if i + ATTN_LOOKAHEAD < len(units):
                nxt = units[i + ATTN_LOOKAHEAD]
                ahead.append(scores(nxt[0], nxt[1]()))
            update(g, st, vt())

    @pl.when(kj == 0)
    def _():
        m_ref[...] = jnp.full_like(m_ref, -jnp.inf)
        acc_ref[...] = jnp.zeros_like(acc_ref)
        sweep([(g, functools.partial(lambda g: kc_ref[0, g], g), functools.partial(lambda g: vc_ref[0, g], g))
               for g in range(hps)])

    tk = k_ref.shape[2]
    sweep([(g,
            functools.partial(lambda g, c: k_ref[0, g, c * kchunk:(c + 1) * kchunk, :], g, c),
            functools.partial(lambda g, c: v_ref[0, g, :, c * kchunk:(c + 1) * kchunk], g, c))
           for c in range(tk // kchunk) for g in range(hps)])

    @pl.when(kj == pl.num_programs(3) - 1)
    def _():
        for g in range(hps):
            acc = acc_ref[g]
            out_t = acc[:LANES] / acc[LANES:LANES + 1]
            o_ref[0, :, g * LANES:(g + 1) * LANES] = out_t.T.astype(o_ref.dtype)


def _attention(q, k, vt, kc, vct, tq_pref, tk_pref, kchunk_pref):
    bsz, heads, seq, _ = q.shape
    lc = kc.shape[2]
    hps = ATTN_HEADS_PER_STEP if heads % ATTN_HEADS_PER_STEP == 0 else 1
    tq = _pick(seq, tq_pref, LANES)
    tk = _pick(seq, tk_pref, LANES)
    kchunk = _pick(tk, kchunk_pref, LANES)
    return pl.pallas_call(
        functools.partial(_attn_kernel, hps=hps, kchunk=kchunk),
        grid=(bsz, heads // hps, seq // tq, seq // tk),
        in_specs=[pl.BlockSpec((1, hps, tq, HEAD_LANES), lambda b, h, i, j: (b, h, i, 0)),
                  pl.BlockSpec((1, hps, tk, HEAD_LANES), lambda b, h, i, j: (b, h, j, 0)),
                  pl.BlockSpec((1, hps, V_ROWS, tk), lambda b, h, i, j: (b, h, 0, j)),
                  pl.BlockSpec((1, hps, lc, HEAD_LANES), lambda b, h, i, j: (b, h, 0, 0)),
                  pl.BlockSpec((1, hps, V_ROWS, lc), lambda b, h, i, j: (b, h, 0, 0))],
        out_specs=pl.BlockSpec((1, tq, hps * LANES), lambda b, h, i, j: (b, i, h)),
        out_shape=jax.ShapeDtypeStruct((bsz, seq, heads * LANES), BF16),
        scratch_shapes=[pltpu.VMEM((hps, 1, tq), F32), pltpu.VMEM((hps, V_ROWS, tq), F32)],
        compiler_params=_cparams("parallel", "parallel", "parallel", "arbitrary"),
        name="attention",
    )(q, k, vt, kc, vct)


def kernel(x, c, ctx, c_ctx, norm1_g, norm2_g, w_ada, b_ada, w_in, hy_conv_w, hy_conv_b, hy_filt_w1, hy_filt_b1, hy_filt_w2, hy_filt_b2, hy_filt_w3, hy_freq, hy_bias, mla_g_qa, mla_w_qb, mla_g_kva, mla_w_kvb, mla_q_norm_g, mla_k_norm_g, w_out, w_mlp1, w_mlp2):
    assert w_ada.shape[0] == 1, "single-layer block"
    bsz, seq, d = x.shape
    lc = ctx.shape[1]
    hyc = hy_conv_b.shape[1]
    width = hy_bias.shape[2]
    q_lora = mla_g_qa.shape[1]
    kv_lora = mla_g_kva.shape[1]
    qk_dim = mla_q_norm_g.shape[1]
    nope = qk_dim - MLA_ROPE
    heads = mla_w_qb.shape[2] // qk_dim
    v_dim = mla_w_kvb.shape[2] // heads - nope
    assert nope == LANES and v_dim == LANES and seq % GRID_W == 0
    q0, kv0, kr0 = hyc, hyc + q_lora, hyc + q_lora + kv_lora
    assert q0 % q_lora == 0 and kv0 % kv_lora == 0 and kr0 % LANES == 0

    cc = jnp.zeros((8, d), F32).at[:bsz].set(c).at[bsz].set(c_ctx)
    mod = _adaln(cc, w_ada[0], b_ada)
    chunk = lambda i: mod[:bsz, i * d:(i + 1) * d][:, None, :]
    sh1, sc1, g1, sh2, sc2, g2 = [chunk(i) for i in range(6)]
    csh1 = mod[bsz:bsz + 1, 0:d][:, None, :]
    csc1 = mod[bsz:bsz + 1, d:2 * d][:, None, :]

    w_in0 = w_in[0]
    np_cols = kr0 + LANES
    np_cols += (-np_cols) % 1024
    w_pad = jnp.pad(w_in0, ((0, 0), (0, np_cols - w_in0.shape[1]))).astype(BF16)
    w_q = _head_layout(mla_w_qb[0], heads, nope).astype(BF16)
    w_kv = mla_w_kvb[0].reshape(kv_lora, heads, nope + v_dim).astype(BF16)
    w_k = w_kv[..., :nope].reshape(kv_lora, heads * nope)
    w_vt = w_kv[..., nope:].reshape(kv_lora, heads * v_dim).T
    gq = _head_layout(mla_q_norm_g, 1, nope)
    gk = _head_layout(mla_k_norm_g, 1, nope)
    rope = _rope_tables(seq)

    ctx_cols = kv0 - kv0 % 1024
    proj_c = _norm_mod_matmul(ctx.reshape(bsz * lc, d), norm1_g, csh1, csc1, w_pad, bsz * lc, F32, False,
                              512, 1024, col0=ctx_cols).reshape(bsz, lc, np_cols - ctx_cols)
    k_c, v_c = _mla_keys_values(proj_c, (kv0 - ctx_cols) // kv_lora, (kr0 - ctx_cols) // LANES,
                                mla_g_kva, w_k, w_vt, gk, None, heads, qk_dim)

    x2 = x.reshape(bsz * seq, d)
    taps = jnp.zeros((3, np_cols), F32).at[1].set(1.0).at[:, :hyc].set(hy_conv_w[0])
    tap_bias = jnp.zeros((1, np_cols), F32).at[0, :hyc].set(hy_conv_b[0])
    proj = _norm_mod_matmul_conv(x2, norm1_g, sh1, sc1, w_pad, taps, tap_bias, seq, 512, 1024)
    proj = proj.reshape(bsz, seq, np_cols)
    y_hy = _hyena(proj, hy_filt_w1[0], hy_filt_b1[0], hy_filt_w2[0], hy_filt_b2[0], hy_filt_w3[0], hy_freq[0],
                  hy_bias[0])
    q = _mla_queries(proj, q0 // q_lora, mla_g_qa, w_q, gq, rope, heads, qk_dim, qk_dim ** -0.5 * math.log2(math.e))
    k, v = _mla_keys_values(proj, kv0 // kv_lora, kr0 // LANES, mla_g_kva, w_k, w_vt, gk, rope, heads, qk_dim)
    y_att = _attention(q, k, v, k_c, v_c, 2048, 2048, 512)

    x1 = _out_proj(y_hy.reshape(bsz * seq, width), y_att.reshape(bsz * seq, heads * v_dim),
                   w_out[0].astype(BF16), x2, g1, seq)
    hmid = _norm_mod_matmul(x1, norm2_g, sh2, sc2, w_mlp1[0].astype(BF16), seq, BF16, True, 1024, 512)
    out = _matmul_gated_residual(hmid, w_mlp2[0].astype(BF16), x1, g2, seq)
    return out.reshape(bsz, seq, d)
```

```python
import functools
import math

import numpy as np
import jax
import jax.numpy as jnp
from jax import lax
from jax.experimental import pallas as pl
from jax.experimental.pallas import tpu as pltpu

F32 = jnp.float32
BF16 = jnp.bfloat16

NORM_EPS = 1e-6
GRID_W = 64
MLA_ROPE = 64
ROPE_THETA = 10000.0
HY_POS_BANDS = 16
HY_DECAY_TARGET = 1e-2
HY_FAST_DECAY_PCT = 0.3
HY_SLOW_DECAY_PCT = 1.5

LANES = 128
SUBLANES = 8
VMEM_LIMIT = 56 * 1024 * 1024


def _cparams(*sem):
    return pltpu.CompilerParams(dimension_semantics=sem, vmem_limit_bytes=VMEM_LIMIT)


def _pick(dim, pref, align):
    t = min(pref, dim)
    t -= t % align
    while t >= align:
        if dim % t == 0:
            return t
        t -= align
    return dim


def _dot(a, b):
    return jnp.dot(a, b, preferred_element_type=F32)


def _adaln_kernel(c_ref, w_ref, b_ref, o_ref):
    c = c_ref[...]
    s = c * jax.nn.sigmoid(c)
    o_ref[...] = _dot(s.astype(BF16), w_ref[...].astype(BF16)) + b_ref[...]


def _adaln(cc, w, b):
    d, n = w.shape
    tn = _pick(n, 512, LANES)
    return pl.pallas_call(
        _adaln_kernel,
        grid=(n // tn,),
        in_specs=[pl.BlockSpec((8, d), lambda j: (0, 0)),
                  pl.BlockSpec((d, tn), lambda j: (0, j)),
                  pl.BlockSpec((1, tn), lambda j: (0, j))],
        out_specs=pl.BlockSpec((8, tn), lambda j: (0, j)),
        out_shape=jax.ShapeDtypeStruct((8, n), F32),
        compiler_params=_cparams("parallel"),
        name="adaln",
    )(cc, w, b)


NORM_ROWS = 16


def _norm_mod_rows(x_ref, gm_ref, sh_ref, h_ref, dst0, nrows, keep=None):
    def body(c, carry):
        r = pl.multiple_of(c * NORM_ROWS, NORM_ROWS)
        x = x_ref[pl.ds(r, NORM_ROWS), :]
        inv = lax.rsqrt(jnp.mean(x * x, axis=-1, keepdims=True) + NORM_EPS)
        y = x * inv * gm_ref[0] + sh_ref[0]
        if keep is not None:
            y = jnp.where(keep, y, 0.0)
        h_ref[pl.ds(dst0 + r, NORM_ROWS), :] = y.astype(BF16)
        return carry

    lax.fori_loop(0, nrows // NORM_ROWS, body, 0, unroll=2 if nrows > NORM_ROWS else 1)


def _nmm_kernel(x_ref, gm_ref, sh_ref, w_ref, o_ref, h_ref, *, sq_relu):
    @pl.when(pl.program_id(1) == 0)
    def _():
        _norm_mod_rows(x_ref, gm_ref, sh_ref, h_ref, 0, x_ref.shape[0])

    acc = _dot(h_ref[...], w_ref[...])
    if sq_relu:
        acc = jnp.square(jnp.maximum(acc, 0.0))
    o_ref[...] = acc.astype(o_ref.dtype)


def _norm_mod_matmul(x, g, shift, scale, w, rows_per_mod, out_dtype, sq_relu, tm_pref, tn_pref, col0=0):
    m, k = x.shape
    n = w.shape[1] - col0
    tm = _pick(math.gcd(m, rows_per_mod), tm_pref, 16)
    tn = _pick(math.gcd(n, col0) if col0 else n, tn_pref, LANES)
    mod_idx = lambda i, j: ((i * tm) // rows_per_mod, 0, 0)
    return pl.pallas_call(
        functools.partial(_nmm_kernel, sq_relu=sq_relu),
        grid=(m // tm, n // tn),
        in_specs=[pl.BlockSpec((tm, k), lambda i, j: (i, 0)),
                  pl.BlockSpec((1, 1, k), mod_idx),
                  pl.BlockSpec((1, 1, k), mod_idx),
                  pl.BlockSpec((k, tn), lambda i, j: (0, col0 // tn + j))],
        out_specs=pl.BlockSpec((tm, tn), lambda i, j: (i, j)),
        out_shape=jax.ShapeDtypeStruct((m, n), out_dtype),
        scratch_shapes=[pltpu.VMEM((tm, k), BF16)],
        compiler_params=_cparams("parallel", "arbitrary"),
        name="norm_mod_matmul",
    )(x, g * (1.0 + scale), shift, w)


HALO = 16


def _nmm_conv_kernel(x_ref, xp_ref, xn_ref, gm_ref, sh_ref, w_ref, cw_ref, cb_ref, o_ref, h_ref,
                     *, tm, rows_per_seq):
    i = pl.program_id(0)

    @pl.when(pl.program_id(1) == 0)
    def _():
        not_first = (i * tm) % rows_per_seq != 0
        not_last = ((i + 1) * tm) % rows_per_seq != 0
        _norm_mod_rows(xp_ref, gm_ref, sh_ref, h_ref, 0, HALO, keep=not_first)
        _norm_mod_rows(x_ref, gm_ref, sh_ref, h_ref, HALO, tm)
        _norm_mod_rows(xn_ref, gm_ref, sh_ref, h_ref, HALO + tm, HALO, keep=not_last)

    acc = _dot(h_ref[...], w_ref[...])
    rows = acc.shape[0]
    mid = acc[HALO:HALO + tm]
    up = pltpu.roll(acc, 1, 0)[HALO:HALO + tm]
    dn = pltpu.roll(acc, rows - 1, 0)[HALO:HALO + tm]
    o_ref[...] = up * cw_ref[0:1, :] + mid * cw_ref[1:2, :] + dn * cw_ref[2:3, :] + cb_ref[...]


def _norm_mod_matmul_conv(x, g, shift, scale, w, conv_w, conv_b, rows_per_seq, tm_pref, tn_pref):
    m, k = x.shape
    n = w.shape[1]
    tm = _pick(rows_per_seq, tm_pref, HALO)
    tn = _pick(n, tn_pref, LANES)
    r = tm // HALO
    last = m // HALO - 1
    mod_idx = lambda i, j: ((i * tm) // rows_per_seq, 0, 0)
    return pl.pallas_call(
        functools.partial(_nmm_conv_kernel, tm=tm, rows_per_seq=rows_per_seq),
        grid=(m // tm, n // tn),
        in_specs=[pl.BlockSpec((tm, k), lambda i, j: (i, 0)),
                  pl.BlockSpec((HALO, k), lambda i, j: (jnp.maximum(i * r - 1, 0), 0)),
                  pl.BlockSpec((HALO, k), lambda i, j: (jnp.minimum((i + 1) * r, last), 0)),
                  pl.BlockSpec((1, 1, k), mod_idx),
                  pl.BlockSpec((1, 1, k), mod_idx),
                  pl.BlockSpec((k, tn), lambda i, j: (0, j)),
                  pl.BlockSpec((3, tn), lambda i, j: (0, j)),
                  pl.BlockSpec((1, tn), lambda i, j: (0, j))],
        out_specs=pl.BlockSpec((tm, tn), lambda i, j: (i, j)),
        out_shape=jax.ShapeDtypeStruct((m, n), F32),
        scratch_shapes=[pltpu.VMEM((tm + 2 * HALO, k), BF16)],
        compiler_params=_cparams("parallel", "arbitrary"),
        name="norm_mod_matmul_conv",
    )(x, x, x, g * (1.0 + scale), shift, w, conv_w, conv_b)


def _outproj_kernel(a0_ref, a1_ref, w_ref, res_ref, gate_ref, o_ref, *, k0):
    acc = _dot(a0_ref[...], w_ref[0:k0, :]) + _dot(a1_ref[...], w_ref[k0:, :])
    o_ref[...] = res_ref[...] + gate_ref[0] * acc


def _out_proj(a0, a1, w, res, gate, rows_per_mod):
    m, k0 = a0.shape
    k1 = a1.shape[1]
    n = w.shape[1]
    tm = _pick(math.gcd(m, rows_per_mod), 1024, 16)
    tn = _pick(n, 1024, LANES)
    return pl.pallas_call(
        functools.partial(_outproj_kernel, k0=k0),
        grid=(m // tm, n // tn),
        in_specs=[pl.BlockSpec((tm, k0), lambda i, j: (i, 0)),
                  pl.BlockSpec((tm, k1), lambda i, j: (i, 0)),
                  pl.BlockSpec((k0 + k1, tn), lambda i, j: (0, j)),
                  pl.BlockSpec((tm, tn), lambda i, j: (i, j)),
                  pl.BlockSpec((1, 1, tn), lambda i, j: ((i * tm) // rows_per_mod, 0, j))],
        out_specs=pl.BlockSpec((tm, tn), lambda i, j: (i, j)),
        out_shape=jax.ShapeDtypeStruct((m, n), F32),
        compiler_params=_cparams("parallel", "parallel"),
        name="out_proj",
    )(a0, a1, w, res, gate)


def _mmres_kernel(a_ref, w_ref, res_ref, gate_ref, o_ref, acc_ref):
    kk = pl.program_id(2)

    @pl.when(kk == 0)
    def _():
        acc_ref[...] = jnp.zeros_like(acc_ref)

    acc_ref[...] += _dot(a_ref[...], w_ref[...])

    @pl.when(kk == pl.num_programs(2) - 1)
    def _():
        o_ref[...] = res_ref[...] + gate_ref[0] * acc_ref[...]


def _matmul_gated_residual(a, w, res, gate, rows_per_mod):
    m, k = a.shape
    n = w.shape[1]
    tm = _pick(math.gcd(m, rows_per_mod), 1024, 16)
    tn = _pick(n, 1024, LANES)
    tk = _pick(k, 2048, LANES)
    return pl.pallas_call(
        _mmres_kernel,
        grid=(m // tm, n // tn, k // tk),
        in_specs=[pl.BlockSpec((tm, tk), lambda i, j, kk: (i, kk)),
                  pl.BlockSpec((tk, tn), lambda i, j, kk: (kk, j)),
                  pl.BlockSpec((tm, tn), lambda i, j, kk: (i, j)),
                  pl.BlockSpec((1, 1, tn), lambda i, j, kk: ((i * tm) // rows_per_mod, 0, j))],
        out_specs=pl.BlockSpec((tm, tn), lambda i, j, kk: (i, j)),
        out_shape=jax.ShapeDtypeStruct((m, n), F32),
        scratch_shapes=[pltpu.VMEM((tm, tn), F32)],
        compiler_params=_cparams("parallel", "parallel", "arbitrary"),
        name="matmul_gated_residual",
    )(a, w, res, gate)


def _filt_hidden_kernel(feat_ref, w1_ref, b1_ref, w2_ref, b2_ref, fr_ref, o_ref):
    fr = fr_ref[...]
    h = jnp.sin(fr * (_dot(feat_ref[...].astype(BF16), w1_ref[...]) + b1_ref[...]))
    h = jnp.sin(fr * (_dot(h.astype(BF16), w2_ref[...]) + b2_ref[...]))
    o_ref[...] = h.astype(BF16)


def _hyena_filter_hidden(seq, w1, b1, w2, b2, freq):
    two_sided = lambda a: np.concatenate([a, a[:1], a[:0:-1]], axis=0)
    pos = np.arange(seq, dtype=np.float32)
    t = np.linspace(0.0, 1.0, seq, dtype=np.float32)[:, None]
    bands = np.linspace(1e-4, HY_POS_BANDS - 1, HY_POS_BANDS, dtype=np.float32)
    ang = np.float32(2.0 * math.pi / seq) * pos[:, None] * bands[None, :]
    feats = np.concatenate([t, np.cos(ang), -np.sin(ang)], axis=-1).astype(np.float32)
    pd = feats.shape[1]
    hid = w1.shape[1]
    hp = LANES
    feats = jnp.asarray(two_sided(np.pad(feats, ((0, 0), (0, hp - pd)))))
    w1p = jnp.pad(w1, ((0, hp - pd), (0, hp - hid))).astype(BF16)
    w2p = jnp.pad(w2, ((0, hp - hid), (0, hp - hid))).astype(BF16)
    b1p = jnp.pad(b1, (0, hp - hid))[None, :]
    b2p = jnp.pad(b2, (0, hp - hid))[None, :]
    frp = jnp.pad(freq, (0, hp - hid))[None, :]
    tl = _pick(seq, 512, 16)
    const = lambda shape: pl.BlockSpec(shape, lambda i: (0, 0))
    return pl.pallas_call(
        _filt_hidden_kernel,
        grid=(2 * seq // tl,),
        in_specs=[pl.BlockSpec((tl, hp), lambda i: (i, 0)), const((hp, hp)), const((1, hp)), const((hp, hp)),
                  const((1, hp)), const((1, hp))],
        out_specs=pl.BlockSpec((tl, hp), lambda i: (i, 0)),
        out_shape=jax.ShapeDtypeStruct((2 * seq, hp), BF16),
        compiler_params=_cparams("parallel"),
        name="hyena_filter_hidden",
    )(feats, w1p, b1p, w2p, b2p, frp)


K1_GROUP = SUBLANES


def _dft_tables(n1, n2):
    n = n1 * n2
    hl = n1 // 2
    k1 = np.arange(n1)[None, :, None]
    l1 = np.arange(hl)[None, None, :]
    l2 = np.arange(n2)[:, None, None]
    ang = -2.0 * np.pi * ((k1 * (n2 * l1 + l2)) % n) / n
    gr, gi = np.cos(ang), np.sin(ang)
    gm = np.concatenate([np.concatenate([gr, -gi], axis=2), np.concatenate([gi, gr], axis=2)], axis=1)
    a2 = -2.0 * np.pi * ((np.arange(n2)[:, None] * np.arange(n2)[None, :]) % n2) / n2
    fr, fi = np.cos(a2), np.sin(a2)
    f2 = np.block([[fr, -fi], [fi, fr]])
    if2 = np.block([[fr, fi], [-fi, fr]])
    ir, ii = np.transpose(gr, (0, 2, 1)) / n, -np.transpose(gi, (0, 2, 1)) / n
    igm = np.concatenate([np.concatenate([ir, -ii], axis=2), np.concatenate([ii, ir], axis=2)], axis=1)
    angf = -2.0 * np.pi * ((k1 * (n2 * np.arange(n1)[None, None, :] + l2)) % n) / n
    gmr = np.concatenate([np.cos(angf), np.sin(angf)], axis=1)
    to = lambda a: jnp.asarray(a.astype(np.float32)).astype(BF16)
    return to(gm), to(f2), to(if2), to(igm), to(gmr)


def _fft_split(seq):
    n = 2 * seq
    n2 = LANES if n % (LANES * 2 * SUBLANES) == 0 and n // LANES >= 2 * SUBLANES else 2 * SUBLANES
    n1 = n // n2
    assert n1 * n2 == n and n1 % (2 * SUBLANES) == 0 and n2 % SUBLANES == 0, (n1, n2)
    return n1, n2


def _fill_pitched(src_ref, p_ref, *, nb, rows, n2, pitch):
    def body(l1, carry):
        for b in range(nb):
            rows = src_ref[b, pl.ds(pl.multiple_of(l1 * n2, n2), n2), :]
            p_ref[b, pl.ds(l1, n2, stride=pitch), :] = rows.astype(F32)
        return carry

    lax.fori_loop(0, rows, body, 0, unroll=2)


def _stage1_to_yv(p_ref, gm_ref, yv_ref, step, *, nb, n1, l2c, pitch):
    rows = n1 // nb

    def body(t, carry):
        l2 = step * l2c + t
        base = pl.multiple_of(l2 * pitch, SUBLANES)
        rhs = jnp.concatenate([p_ref[b, pl.ds(base, rows), :] for b in range(nb)], axis=0)
        res = _dot(gm_ref[t], rhs.astype(BF16))
        row0 = pl.multiple_of(l2 * K1_GROUP, K1_GROUP)
        for g in range(n1 // K1_GROUP):
            for ri in range(2):
                lo = ri * n1 + g * K1_GROUP
                yv_ref[g * 2 + ri, pl.ds(row0, K1_GROUP), :] = res[lo:lo + K1_GROUP]
        return carry

    lax.fori_loop(0, l2c, body, 0, unroll=8)


def _yv_pair(yv_ref, g, t, n2):
    rows = lambda tt: pl.ds(tt, n2, stride=K1_GROUP)
    re = jnp.concatenate([yv_ref[g * 2, rows(t), :], yv_ref[g * 2, rows(t + 1), :]], axis=1)
    im = jnp.concatenate([yv_ref[g * 2 + 1, rows(t), :], yv_ref[g * 2 + 1, rows(t + 1), :]], axis=1)
    return jnp.concatenate([re, im], axis=0).astype(BF16)


def _fill_filters(h_ref, w3f_ref, w3b_ref, dl_ref, p_ref, *, n1, n2, pitch):
    seq = n1 * n2 // 2
    inv = 1.0 / (seq - 1)
    dl = dl_ref[...]
    row = lax.broadcasted_iota(jnp.int32, (n2, LANES), 0)
    within = row.astype(F32) * inv * dl

    def fill(w_ref, backward):
        e_row = jnp.exp(within) if backward else jnp.exp(-within)

        def body(l1, carry):
            first = jnp.asarray(2 * seq - l1 * n2 if backward else l1 * n2, F32) * inv
            decay = e_row * jnp.exp(-first * dl)
            hid = h_ref[pl.ds(pl.multiple_of(l1 * n2, n2), n2), :]
            f = _dot(hid, w_ref[...]) * decay
            if backward:
                f = jnp.where(l1 * n2 + row == seq, 0.0, f)
            p_ref[0, pl.ds(l1, n2, stride=pitch), :] = f
            return carry
        return body

    lax.fori_loop(0, n1 // 2, fill(w3f_ref, False), 0, unroll=8)
    lax.fori_loop(n1 // 2, n1, fill(w3b_ref, True), 0, unroll=8)


def _filtspec_kernel(h_ref, w3f_ref, w3b_ref, dl_ref, gmr_ref, f2_ref, o_ref, yv_ref, p_ref,
                     *, n1, n2, l2c, g2, s1, pitch):
    s = pl.program_id(1)

    @pl.when(s == 0)
    def _():
        _fill_filters(h_ref, w3f_ref, w3b_ref, dl_ref, p_ref, n1=n1, n2=n2, pitch=pitch)

    @pl.when(s < s1)
    def _():
        _stage1_to_yv(p_ref, gmr_ref, yv_ref, s, nb=1, n1=n1, l2c=l2c, pitch=pitch)

    @pl.when(s >= s1)
    def _():
        for gg in range(g2):
            g = (s - s1) * g2 + gg
            for t in range(0, K1_GROUP, 2):
                x = _dot(f2_ref[...], _yv_pair(yv_ref, g, t, n2))
                for d in range(2):
                    o_ref[gg * K1_GROUP + t + d, 0] = x[:n2, d * LANES:(d + 1) * LANES].astype(o_ref.dtype)
                    o_ref[gg * K1_GROUP + t + d, 1] = x[n2:, d * LANES:(d + 1) * LANES].astype(o_ref.dtype)


def _filter_spectrum(hidden, w3, width, gmr, f2, n1, n2):
    hp = hidden.shape[1]
    c = w3.shape[1] // 2
    w3p = jnp.pad(w3, ((0, hp - w3.shape[0]), (0, 0))).astype(BF16)
    deltas = np.abs(np.linspace(math.log(HY_DECAY_TARGET) / HY_SLOW_DECAY_PCT,
                                math.log(HY_DECAY_TARGET) / HY_FAST_DECAY_PCT, width, dtype=np.float32))
    dl = jnp.asarray(np.tile(deltas, c // width)[None, :])
    ng = n1 // K1_GROUP
    l2c = _pick(n2, 32, 4)
    g2 = _pick(ng, 4, 1)
    s1, s2 = n2 // l2c, ng // g2
    pitch = n1 + SUBLANES
    return pl.pallas_call(
        functools.partial(_filtspec_kernel, n1=n1, n2=n2, l2c=l2c, g2=g2, s1=s1, pitch=pitch),
        grid=(c // LANES, s1 + s2),
        in_specs=[pl.BlockSpec((n1 * n2, hp), lambda cb, s: (0, 0)),
                  pl.BlockSpec((hp, LANES), lambda cb, s: (0, cb)),
                  pl.BlockSpec((hp, LANES), lambda cb, s: (0, c // LANES + cb)),
                  pl.BlockSpec((1, LANES), lambda cb, s: (0, cb)),
                  pl.BlockSpec((l2c, 2 * n1, n1), lambda cb, s: (jnp.minimum(s, s1 - 1), 0, 0)),
                  pl.BlockSpec((2 * n2, 2 * n2), lambda cb, s: (0, 0))],
        out_specs=pl.BlockSpec((g2 * K1_GROUP, 2, n2, LANES), lambda cb, s: (jnp.maximum(s - s1, 0), 0, 0, cb)),
        out_shape=jax.ShapeDtypeStruct((n1, 2, n2, c), BF16),
        scratch_shapes=[pltpu.VMEM((2 * ng, n2 * K1_GROUP, LANES), F32), pltpu.VMEM((1, n2 * pitch, LANES), F32)],
        compiler_params=_cparams("parallel", "arbitrary"),
        name="filter_spectrum",
    )(hidden, w3p, w3p, dl, gmr, f2)


def _fftconv_kernel(sig_ref, gm_ref, k_ref, f2_ref, if2_ref, igm_ref, gate_ref, bias_ref, o_ref, yv_ref, p_ref,
                    *, n1, n2, l2c, g2, l3, rows4, s1, s2, s3, pitch):
    s = pl.program_id(1)
    hl = n1 // 2
    ng = n1 // K1_GROUP

    @pl.when(s == 0)
    def _():
        _fill_pitched(sig_ref, p_ref, nb=2, rows=hl, n2=n2, pitch=pitch)

    @pl.when(s < s1)
    def _():
        _stage1_to_yv(p_ref, gm_ref, yv_ref, s, nb=2, n1=n1, l2c=l2c, pitch=pitch)

    @pl.when((s >= s1) & (s < s1 + s2))
    def _():
        for gg in range(g2):
            g = (s - s1) * g2 + gg
            rows = lambda tt: pl.ds(tt, n2, stride=K1_GROUP)
            for t in range(0, K1_GROUP, 2):
                x = _dot(f2_ref[...], _yv_pair(yv_ref, g, t, n2))
                xr, xi = x[:n2], x[n2:]
                kt = gg * K1_GROUP + t
                kr = jnp.concatenate([k_ref[kt, 0], k_ref[kt + 1, 0]], axis=1).astype(F32)
                ki = jnp.concatenate([k_ref[kt, 1], k_ref[kt + 1, 1]], axis=1).astype(F32)
                p = jnp.concatenate([xr * kr - xi * ki, xr * ki + xi * kr], axis=0).astype(BF16)
                yp = _dot(if2_ref[...], p)
                for d in range(2):
                    for ri in range(2):
                        yv_ref[g * 2 + ri, rows(t + d), :] = yp[ri * n2:(ri + 1) * n2, d * LANES:(d + 1) * LANES]

    @pl.when((s >= s1 + s2) & (s < s1 + s2 + s3))
    def _():
        def body(i, carry):
            l2 = (s - s1 - s2) * (l3 * SUBLANES) + i
            row0 = pl.multiple_of(l2 * K1_GROUP, K1_GROUP)
            tiles = [yv_ref[g * 2 + ri, pl.ds(row0, K1_GROUP), :] for ri in range(2) for g in range(ng)]
            res = _dot(igm_ref[i], jnp.concatenate(tiles, axis=0).astype(BF16))
            base = pl.multiple_of(l2 * pitch, SUBLANES)
            for b in range(2):
                p_ref[b, pl.ds(base, hl), :] = res[b * hl:(b + 1) * hl]
            return carry

        lax.fori_loop(0, l3 * SUBLANES, body, 0, unroll=8)

    @pl.when(s >= s1 + s2 + s3)
    def _():
        r = s - s1 - s2 - s3
        for j in range(rows4 // n2):
            l1 = r * (rows4 // n2) + j
            nat = pl.ds(pl.multiple_of(l1 * n2, n2), n2)
            for b in range(2):
                conv = p_ref[b, pl.ds(l1, n2, stride=pitch), :] + bias_ref[...] * sig_ref[b, nat, :]
                o_ref[b, j * n2:(j + 1) * n2, :] = (gate_ref[b, j * n2:(j + 1) * n2, :] * conv).astype(o_ref.dtype)


def _fft_conv_gated(sig, kspec, kcol_block0, gate, gate_col_block0, bias, tables, n1, n2, out_dtype):
    gm, f2, if2, igm = tables
    c = bias.shape[1]
    seq = sig.shape[1]
    ng = n1 // K1_GROUP
    l2c = _pick(n2, 32, 4)
    g2 = _pick(ng, 4, 1)
    l3 = _pick(n2 // SUBLANES, 4, 1)
    rows4 = _pick(seq, 2048, n2)
    s1, s2, s3, s4 = n2 // l2c, ng // g2, n2 // (l3 * SUBLANES), seq // rows4
    pitch = n1 // 2 + SUBLANES
    clamp = lambda v, n: jnp.clip(v, 0, n - 1)
    return pl.pallas_call(
        functools.partial(_fftconv_kernel, n1=n1, n2=n2, l2c=l2c, g2=g2, l3=l3, rows4=rows4, s1=s1, s2=s2, s3=s3,
                          pitch=pitch),
        grid=(c // LANES, s1 + s2 + s3 + s4),
        in_specs=[pl.BlockSpec((2, seq, LANES), lambda cb, s: (0, 0, cb), pipeline_mode=pl.Buffered(1)),
                  pl.BlockSpec((l2c, 2 * n1, n1), lambda cb, s: (clamp(s, s1), 0, 0)),
                  pl.BlockSpec((g2 * K1_GROUP, 2, n2, LANES),
                               lambda cb, s: (clamp(s - s1, s2), 0, 0, kcol_block0 + cb)),
                  pl.BlockSpec((2 * n2, 2 * n2), lambda cb, s: (0, 0)),
                  pl.BlockSpec((2 * n2, 2 * n2), lambda cb, s: (0, 0)),
                  pl.BlockSpec((l3 * SUBLANES, n1, 2 * n1), lambda cb, s: (clamp(s - s1 - s2, s3), 0, 0)),
                  pl.BlockSpec((2, rows4, LANES),
                               lambda cb, s: (0, clamp(s - s1 - s2 - s3, s4), gate_col_block0 + cb)),
                  pl.BlockSpec((1, LANES), lambda cb, s: (0, cb))],
        out_specs=pl.BlockSpec((2, rows4, LANES), lambda cb, s: (0, clamp(s - s1 - s2 - s3, s4), cb)),
        out_shape=jax.ShapeDtypeStruct((2, seq, c), out_dtype),
        scratch_shapes=[pltpu.VMEM((2 * ng, n2 * K1_GROUP, LANES), F32), pltpu.VMEM((2, n2 * pitch, LANES), F32)],
        compiler_params=_cparams("parallel", "arbitrary"),
        name="fft_conv_gated",
    )(sig, gm, kspec, f2, if2, igm, gate, bias)


def _hyena(u, fw1, fb1, fw2, fb2, fw3, freq, hy_bias):
    bsz, seq, _ = u.shape
    assert bsz == 2, "the FFT convolution packs exactly two batch elements into one complex signal"
    width = hy_bias.shape[1]
    n1, n2 = _fft_split(seq)
    gm, f2, if2, igm, gmr = _dft_tables(n1, n2)

    hidden = _hyena_filter_hidden(seq, fw1, fb1, fw2, fb2, freq)
    kspec = _filter_spectrum(hidden, fw3, width, gmr, f2, n1, n2)

    tables = (gm, f2, if2, igm)
    wb = width // LANES
    z = _fft_conv_gated(u, kspec, 0, u, wb, hy_bias[0:1], tables, n1, n2, F32)
    return _fft_conv_gated(z, kspec, wb, u, 2 * wb, hy_bias[1:2], tables, n1, n2, BF16)


HEAD_LANES = 2 * LANES


ROPE_NF = MLA_ROPE // 4


def _head_layout(w, heads, nope):
    r = w.shape[0]
    w3 = jnp.pad(w.reshape(r, heads, nope + MLA_ROPE), ((0, 0), (0, 0), (0, HEAD_LANES - nope - MLA_ROPE)))
    return w3.reshape(r, heads * HEAD_LANES)


def _rope_tables(seq):
    rows = seq // GRID_W
    row = np.repeat(np.arange(rows, dtype=np.float32), GRID_W)
    col = np.tile(np.arange(GRID_W, dtype=np.float32), rows)
    half = MLA_ROPE // 2
    inv = (np.float32(ROPE_THETA) ** (-np.arange(0, half, 2, dtype=np.float32) / np.float32(half))).astype(np.float32)
    zero = np.zeros((seq, ROPE_NF), np.float32)
    cos, sin_lo, sin_hi = [], [], []
    for pos in (row, col):
        ang = pos[:, None] * inv
        cos += [np.cos(ang), np.cos(ang)]
        sin_lo += [-np.sin(ang), zero]
        sin_hi += [zero, np.sin(ang)]
    pad = [np.zeros((seq, LANES - MLA_ROPE), np.float32)]
    return tuple(jnp.asarray(np.concatenate(t + pad, axis=-1).astype(np.float32)) for t in (cos, sin_lo, sin_hi))


def _rope(t, cos, sin_lo, sin_hi):
    return t * cos + pltpu.roll(t, LANES - ROPE_NF, 1) * sin_lo + pltpu.roll(t, ROPE_NF, 1) * sin_hi


def _rms(x, g):
    return x * lax.rsqrt(jnp.mean(x * x, axis=-1, keepdims=True) + NORM_EPS) * g


def _q_kernel(qa_ref, g_ref, w_ref, gn_ref, cos_ref, slo_ref, shi_ref, o_ref, *, heads, qk_dim, out_scale):
    xn = _rms(qa_ref[...], g_ref[...]).astype(BF16)
    q = _dot(xn, w_ref[...])
    gn = gn_ref[...]
    cos, slo, shi = cos_ref[...], slo_ref[...], shi_ref[...]
    for h in range(heads):
        qh = q[:, h * HEAD_LANES:(h + 1) * HEAD_LANES]
        inv = lax.rsqrt(jnp.sum(qh * qh, axis=-1, keepdims=True) / qk_dim + NORM_EPS)
        qn = qh * inv * gn
        o_ref[0, h, :, 0:LANES] = (qn[:, :LANES] * out_scale).astype(BF16)
        o_ref[0, h, :, LANES:] = (_rope(qn[:, LANES:], cos, slo, shi) * out_scale).astype(BF16)


def _mla_queries(proj, col_block, g_qa, w_q, gn, rope, heads, qk_dim, out_scale):
    bsz, seq, _ = proj.shape
    r = g_qa.shape[1]
    tm = _pick(seq, 512, 16)
    return pl.pallas_call(
        functools.partial(_q_kernel, heads=heads, qk_dim=qk_dim, out_scale=out_scale),
        grid=(bsz, seq // tm),
        in_specs=[pl.BlockSpec((None, tm, r), lambda b, i: (b, i, col_block)),
                  pl.BlockSpec((1, r), lambda b, i: (0, 0)),
                  pl.BlockSpec((r, heads * HEAD_LANES), lambda b, i: (0, 0)),
                  pl.BlockSpec((1, HEAD_LANES), lambda b, i: (0, 0))]
                 + [pl.BlockSpec((tm, LANES), lambda b, i: (i, 0))] * 3,
        out_specs=pl.BlockSpec((1, heads, tm, HEAD_LANES), lambda b, i: (b, 0, i, 0)),
        out_shape=jax.ShapeDtypeStruct((bsz, heads, seq, HEAD_LANES), BF16),
        compiler_params=_cparams("parallel", "parallel"),
        name="mla_queries",
    )(proj, g_qa, w_q, gn, *rope)


V_ROWS = LANES + 16


def _kv_kernel(kva_ref, kr_ref, g_ref, wk_ref, wvt_ref, gn_ref, *rest, heads, qk_dim, rope):
    if rope:
        cos_ref, slo_ref, shi_ref, k_ref, v_ref = rest
    else:
        k_ref, v_ref = rest
    xn = _rms(kva_ref[...], g_ref[...]).astype(BF16)
    kk = _dot(xn, wk_ref[...])
    vt = lax.dot_general(wvt_ref[...], xn, (((1,), (1,)), ((), ())), preferred_element_type=F32)
    gn = gn_ref[...]
    kr = kr_ref[...]
    ssr = jnp.sum(kr * kr, axis=-1, keepdims=True)
    krg = kr * gn[:, LANES:]
    if rope:
        krg = _rope(krg, cos_ref[...], slo_ref[...], shi_ref[...])
    ones_row = (lax.broadcasted_iota(jnp.int32, (V_ROWS - LANES, kr.shape[0]), 0) == 0).astype(BF16)
    for h in range(heads):
        kn = kk[:, h * LANES:(h + 1) * LANES]
        inv = lax.rsqrt((jnp.sum(kn * kn, axis=-1, keepdims=True) + ssr) / qk_dim + NORM_EPS)
        k_ref[0, h, :, 0:LANES] = (kn * inv * gn[:, :LANES]).astype(BF16)
        k_ref[0, h, :, LANES:] = (krg * inv).astype(BF16)
        v_ref[0, h, 0:LANES, :] = vt[h * LANES:(h + 1) * LANES].astype(BF16)
        v_ref[0, h, LANES:, :] = ones_row


def _mla_keys_values(proj, kv_block, kr_block, g_kva, w_k, w_vt, gn, rope, heads, qk_dim):
    bsz, seq, _ = proj.shape
    r = g_kva.shape[1]
    tm = _pick(seq, 512, LANES)
    in_specs = [pl.BlockSpec((None, tm, r), lambda b, i: (b, i, kv_block)),
                pl.BlockSpec((None, tm, LANES), lambda b, i: (b, i, kr_block)),
                pl.BlockSpec((1, r), lambda b, i: (0, 0)),
                pl.BlockSpec((r, heads * LANES), lambda b, i: (0, 0)),
                pl.BlockSpec((heads * LANES, r), lambda b, i: (0, 0)),
                pl.BlockSpec((1, HEAD_LANES), lambda b, i: (0, 0))]
    args = [proj, proj, g_kva, w_k, w_vt, gn]
    if rope is not None:
        in_specs += [pl.BlockSpec((tm, LANES), lambda b, i: (i, 0))] * 3
        args += list(rope)
    return pl.pallas_call(
        functools.partial(_kv_kernel, heads=heads, qk_dim=qk_dim, rope=rope is not None),
        grid=(bsz, seq // tm),
        in_specs=in_specs,
        out_specs=[pl.BlockSpec((1, heads, tm, HEAD_LANES), lambda b, i: (b, 0, i, 0)),
                   pl.BlockSpec((1, heads, V_ROWS, tm), lambda b, i: (b, 0, 0, i))],
        out_shape=[jax.ShapeDtypeStruct((bsz, heads, seq, HEAD_LANES), BF16),
                   jax.ShapeDtypeStruct((bsz, heads, V_ROWS, seq), BF16)],
        compiler_params=_cparams("parallel", "parallel"),
        name="mla_keys_values",
    )(*args)


ATTN_HEADS_PER_STEP = 2
ATTN_LOOKAHEAD = 2


def _attn_kernel(q_ref, k_ref, v_ref, kc_ref, vc_ref, o_ref, m_ref, acc_ref, *, hps, kchunk):
    kj = pl.program_id(3)

    def scores(g, k):
        return lax.dot_general(k, q_ref[0, g], (((1,), (1,)), ((), ())), preferred_element_type=F32)

    def update(g, st, vt):
        m_old = m_ref[g]
        m_new = jnp.maximum(m_old, jnp.max(st, axis=0, keepdims=True))
        p = jnp.exp2(st - m_new)
        acc_ref[g] = jnp.exp2(m_old - m_new) * acc_ref[g] + _dot(vt, p.astype(BF16))
        m_ref[g] = m_new

    def sweep(units):
        ahead = [scores(g, k()) for g, k, _ in units[:ATTN_LOOKAHEAD]]
        for i, (g, _, vt) in enumerate(units):
            st = ahead.pop(0)
            if i + ATTN_LOOKAHEAD < len(units):
                nxt = units[i + ATTN_LOOKAHEAD]
                ahead.append(scores(nxt[0], nxt[1]()))
            update(g, st, vt())

    @pl.when(kj == 0)
    def _():
        m_ref[...] = jnp.full_like(m_ref, -jnp.inf)
        acc_ref[...] = jnp.zeros_like(acc_ref)
        sweep([(g, functools.partial(lambda g: kc_ref[0, g], g), functools.partial(lambda g: vc_ref[0, g], g))
               for g in range(hps)])

    tk = k_ref.shape[2]
    sweep([(g,
            functools.partial(lambda g, c: k_ref[0, g, c * kchunk:(c + 1) * kchunk, :], g, c),
            functools.partial(lambda g, c: v_ref[0, g, :, c * kchunk:(c + 1) * kchunk], g, c))
           for c in range(tk // kchunk) for g in range(hps)])

    @pl.when(kj == pl.num_programs(3) - 1)
    def _():
        for g in range(hps):
            acc = acc_ref[g]
            out_t = acc[:LANES] / acc[LANES:LANES + 1]
            o_ref[0, :, g * LANES:(g + 1) * LANES] = out_t.T.astype(o_ref.dtype)


def _attention(q, k, vt, kc, vct, tq_pref, tk_pref, kchunk_pref):
    bsz, heads, seq, _ = q.shape
    lc = kc.shape[2]
    hps = ATTN_HEADS_PER_STEP if heads % ATTN_HEADS_PER_STEP == 0 else 1
    tq = _pick(seq, tq_pref, LANES)
    tk = _pick(seq, tk_pref, LANES)
    kchunk = _pick(tk, kchunk_pref, LANES)
    return pl.pallas_call(
        functools.partial(_attn_kernel, hps=hps, kchunk=kchunk),
        grid=(bsz, heads // hps, seq // tq, seq // tk),
        in_specs=[pl.BlockSpec((1, hps, tq, HEAD_LANES), lambda b, h, i, j: (b, h, i, 0)),
                  pl.BlockSpec((1, hps, tk, HEAD_LANES), lambda b, h, i, j: (b, h, j, 0)),
                  pl.BlockSpec((1, hps, V_ROWS, tk), lambda b, h, i, j: (b, h, 0, j)),
                  pl.BlockSpec((1, hps, lc, HEAD_LANES), lambda b, h, i, j: (b, h, 0, 0)),
                  pl.BlockSpec((1, hps, V_ROWS, lc), lambda b, h, i, j: (b, h, 0, 0))],
        out_specs=pl.BlockSpec((1, tq, hps * LANES), lambda b, h, i, j: (b, i, h)),
        out_shape=jax.ShapeDtypeStruct((bsz, seq, heads * LANES), BF16),
        scratch_shapes=[pltpu.VMEM((hps, 1, tq), F32), pltpu.VMEM((hps, V_ROWS, tq), F32)],
        compiler_params=_cparams("parallel", "parallel", "parallel", "arbitrary"),
        name="attention",
    )(q, k, vt, kc, vct)


def kernel(x, c, ctx, c_ctx, norm1_g, norm2_g, w_ada, b_ada, w_in, hy_conv_w, hy_conv_b, hy_filt_w1, hy_filt_b1, hy_filt_w2, hy_filt_b2, hy_filt_w3, hy_freq, hy_bias, mla_g_qa, mla_w_qb, mla_g_kva, mla_w_kvb, mla_q_norm_g, mla_k_norm_g, w_out, w_mlp1, w_mlp2):
    assert w_ada.shape[0] == 1, "single-layer block"
    bsz, seq, d = x.shape
    lc = ctx.shape[1]
    hyc = hy_conv_b.shape[1]
    width = hy_bias.shape[2]
    q_lora = mla_g_qa.shape[1]
    kv_lora = mla_g_kva.shape[1]
    qk_dim = mla_q_norm_g.shape[1]
    nope = qk_dim - MLA_ROPE
    heads = mla_w_qb.shape[2] // qk_dim
    v_dim = mla_w_kvb.shape[2] // heads - nope
    assert nope == LANES and v_dim == LANES and seq % GRID_W == 0
    q0, kv0, kr0 = hyc, hyc + q_lora, hyc + q_lora + kv_lora
    assert q0 % q_lora == 0 and kv0 % kv_lora == 0 and kr0 % LANES == 0

    cc = jnp.zeros((8, d), F32).at[:bsz].set(c).at[bsz].set(c_ctx)
    mod = _adaln(cc, w_ada[0], b_ada)
    chunk = lambda i: mod[:bsz, i * d:(i + 1) * d][:, None, :]
    sh1, sc1, g1, sh2, sc2, g2 = [chunk(i) for i in range(6)]
    csh1 = mod[bsz:bsz + 1, 0:d][:, None, :]
    csc1 = mod[bsz:bsz + 1, d:2 * d][:, None, :]

    w_in0 = w_in[0]
    np_cols = kr0 + LANES
    np_cols += (-np_cols) % 1024
    w_pad = jnp.pad(w_in0, ((0, 0), (0, np_cols - w_in0.shape[1]))).astype(BF16)
    w_q = _head_layout(mla_w_qb[0], heads, nope).astype(BF16)
    w_kv = mla_w_kvb[0].reshape(kv_lora, heads, nope + v_dim).astype(BF16)
    w_k = w_kv[..., :nope].reshape(kv_lora, heads * nope)
    w_vt = w_kv[..., nope:].reshape(kv_lora, heads * v_dim).T
    gq = _head_layout(mla_q_norm_g, 1, nope)
    gk = _head_layout(mla_k_norm_g, 1, nope)
    rope = _rope_tables(seq)

    ctx_cols = kv0 - kv0 % 1024
    proj_c = _norm_mod_matmul(ctx.reshape(bsz * lc, d), norm1_g, csh1, csc1, w_pad, bsz * lc, F32, False,
                              512, 1024, col0=ctx_cols).reshape(bsz, lc, np_cols - ctx_cols)
    k_c, v_c = _mla_keys_values(proj_c, (kv0 - ctx_cols) // kv_lora, (kr0 - ctx_cols) // LANES,
                                mla_g_kva, w_k, w_vt, gk, None, heads, qk_dim)

    x2 = x.reshape(bsz * seq, d)
    taps = jnp.zeros((3, np_cols), F32).at[1].set(1.0).at[:, :hyc].set(hy_conv_w[0])
    tap_bias = jnp.zeros((1, np_cols), F32).at[0, :hyc].set(hy_conv_b[0])
    proj = _norm_mod_matmul_conv(x2, norm1_g, sh1, sc1, w_pad, taps, tap_bias, seq, 512, 1024)
    proj = proj.reshape(bsz, seq, np_cols)
    y_hy = _hyena(proj, hy_filt_w1[0], hy_filt_b1[0], hy_filt_w2[0], hy_filt_b2[0], hy_filt_w3[0], hy_freq[0],
                  hy_bias[0])
    q = _mla_queries(proj, q0 // q_lora, mla_g_qa, w_q, gq, rope, heads, qk_dim, qk_dim ** -0.5 * math.log2(math.e))
    k, v = _mla_keys_values(proj, kv0 // kv_lora, kr0 // LANES, mla_g_kva, w_k, w_vt, gk, rope, heads, qk_dim)
    y_att = _attention(q, k, v, k_c, v_c, 2048, 2048, 512)

    x1 = _out_proj(y_hy.reshape(bsz * seq, width), y_att.reshape(bsz * seq, heads * v_dim),
                   w_out[0].astype(BF16), x2, g1, seq)
    hmid = _norm_mod_matmul(x1, norm2_g, sh2, sc2, w_mlp1[0].astype(BF16), seq, BF16, True, 512, 1024)
    out = _matmul_gated_residual(hmid, w_mlp2[0].astype(BF16), x1, g2, seq)
    return out.reshape(bsz, seq, d)
```
